```python
import math
import jax, jax.numpy as jnp
from jax import lax
import numpy as np

D_MODEL = 2048
BATCH = 2
SEQ = 4096
DEPTH = 1

D_MIX = D_MODEL
MLA_HEADS = 8
MLA_NOPE_DIM = 128
MLA_ROPE_DIM = 64
MLA_V_DIM = 128
MLA_WIDTH = MLA_HEADS * MLA_V_DIM
Q_LORA_RANK = 512
KV_LORA_RANK = 256
CONV_GROUPS = 8
CONV_WIDTH = D_MIX - MLA_WIDTH
CONV_K = 3
FFN_DIM = 5632
FFN_CONV_K = 3
ROPE_THETA = 10000.0
Q_BLOCK = 128
LN_EPS = 1e-5
RMS_EPS = 1e-6
DEEPNORM_ALPHA = (2.0 * DEPTH) ** 0.25
DEEPNORM_BETA = (8.0 * DEPTH) ** -0.25
IN_PROJ_DIM = Q_LORA_RANK + KV_LORA_RANK + MLA_ROPE_DIM + 3 * CONV_WIDTH
IN_SPLITS = (Q_LORA_RANK,
             Q_LORA_RANK + KV_LORA_RANK,
             Q_LORA_RANK + KV_LORA_RANK + MLA_ROPE_DIM,
             Q_LORA_RANK + KV_LORA_RANK + MLA_ROPE_DIM + CONV_WIDTH,
             Q_LORA_RANK + KV_LORA_RANK + MLA_ROPE_DIM + 2 * CONV_WIDTH)

kernel_name = "hymba_mla_shortconv_convffn_deepnorm"


def layer_norm(x, g, b):
    x32 = x.astype(jnp.float32)
    mu = jnp.mean(x32, axis=-1, keepdims=True)
    var = jnp.mean(jnp.square(x32 - mu), axis=-1, keepdims=True)
    y = (x32 - mu) * lax.rsqrt(var + LN_EPS) * g.astype(jnp.float32) + b.astype(jnp.float32)
    return y.astype(x.dtype)


def rms_norm(x, g):
    x32 = x.astype(jnp.float32)
    y = x32 * lax.rsqrt(jnp.mean(jnp.square(x32), axis=-1, keepdims=True) + RMS_EPS)
    return (y * g.astype(jnp.float32)).astype(x.dtype)


def causal_dwconv(x, w):
    K = w.shape[0]
    S = x.shape[1]
    xp = jnp.pad(x, ((0, 0), (K - 1, 0), (0, 0)))
    y = w[K - 1] * x
    for k in range(K - 1):
        y = y + w[k] * xp[:, k:k + S]
    return y


def apply_rope(x, positions):
    D = x.shape[-1]
    inv_freq = ROPE_THETA ** (-jnp.arange(0, D, 2, dtype=jnp.float32) / D)
    ang = positions.astype(jnp.float32)[..., None] * inv_freq
    cos = jnp.cos(ang)[:, :, None, :]
    sin = jnp.sin(ang)[:, :, None, :]
    x32 = x.astype(jnp.float32)
    x1, x2 = x32[..., : D // 2], x32[..., D // 2:]
    out = jnp.concatenate([x1 * cos - x2 * sin, x1 * sin + x2 * cos], axis=-1)
    return out.astype(x.dtype)


def causal_block_attention(q, k, v):
    B, S, H, Dqk = q.shape
    Dv = v.shape[-1]
    nb = S // Q_BLOCK
    scale = Dqk ** -0.5
    qb = q.reshape(B, nb, Q_BLOCK, H, Dqk).transpose(1, 0, 2, 3, 4)
    key_pos = jnp.arange(S)

    def one_block(args):
        i, q_i = args
        s = jnp.einsum('bqhd,bkhd->bhqk', q_i, k,
                       preferred_element_type=jnp.float32) * scale
        q_pos = i * Q_BLOCK + jnp.arange(Q_BLOCK)
        mask = key_pos[None, :] <= q_pos[:, None]
        s = jnp.where(mask[None, None], s, -1e30)
        p = jax.nn.softmax(s, axis=-1)
        return jnp.einsum('bhqk,bkhd->bqhd', p.astype(v.dtype), v)

    out = lax.map(one_block, (jnp.arange(nb), qb))
    return out.transpose(1, 0, 2, 3, 4).reshape(B, S, H * Dv)


def hybrid_mixer(x, positions, w_in, q_norm_g, w_uq, kv_norm_g, w_ukv, conv_w, w_out):
    B, S, _ = x.shape
    h = jnp.einsum('bsd,de->bse', x, w_in)
    c_q, c_kv, k_r, gate_b, gate_c, x_c = jnp.split(h, IN_SPLITS, axis=-1)

    q = jnp.einsum('bsr,re->bse', rms_norm(c_q, q_norm_g), w_uq)
    q = q.reshape(B, S, MLA_HEADS, MLA_NOPE_DIM + MLA_ROPE_DIM)
    q_nope, q_rope = q[..., :MLA_NOPE_DIM], q[..., MLA_NOPE_DIM:]
    q_rope = apply_rope(q_rope, positions)
    kv = jnp.einsum('bsr,re->bse', rms_norm(c_kv, kv_norm_g), w_ukv)
    kv = kv.reshape(B, S, MLA_HEADS, MLA_NOPE_DIM + MLA_V_DIM)
    k_nope, v = kv[..., :MLA_NOPE_DIM], kv[..., MLA_NOPE_DIM:]
    k_rope = apply_rope(k_r[:, :, None, :], positions)
    q_full = jnp.concatenate([q_nope, q_rope], axis=-1)
    k_full = jnp.concatenate(
        [k_nope, jnp.broadcast_to(k_rope, (B, S, MLA_HEADS, MLA_ROPE_DIM))], axis=-1)
    attn_out = causal_block_attention(q_full, k_full, v)

    conv_out = gate_b * causal_dwconv(gate_c * x_c, conv_w)

    mixed = jnp.concatenate([attn_out, conv_out], axis=-1)
    return jnp.einsum('bse,ed->bsd', mixed, w_out)


def conv_ffn(x, w_gate_up, ffn_conv_w, w_down):
    gu = causal_dwconv(jnp.einsum('bsd,df->bsf', x, w_gate_up), ffn_conv_w)
    g, u = gu[..., :FFN_DIM], gu[..., FFN_DIM:]
    return jnp.einsum('bsf,fd->bsd', jax.nn.silu(g) * u, w_down)


def setup_inputs(seed: int = 0) -> dict:
    key = jax.random.key(seed)
    ks = jax.random.split(key, 18)
    f32 = jnp.float32

    def nrm(k, shape, scale):
        return jax.random.normal(k, shape, f32) * scale

    x = jax.random.normal(ks[0], (BATCH, SEQ, D_MODEL), f32)
    positions = jnp.broadcast_to(jnp.arange(SEQ, dtype=jnp.int32), (BATCH, SEQ))
    return {
        "x": x,
        "positions": positions,
        "w_in": nrm(ks[1], (DEPTH, D_MODEL, IN_PROJ_DIM), D_MODEL ** -0.5),
        "q_norm_g": 1.0 + nrm(ks[2], (DEPTH, Q_LORA_RANK), 0.02),
        "w_uq": nrm(ks[3], (DEPTH, Q_LORA_RANK, MLA_HEADS * (MLA_NOPE_DIM + MLA_ROPE_DIM)),
                    Q_LORA_RANK ** -0.5),
        "kv_norm_g": 1.0 + nrm(ks[4], (DEPTH, KV_LORA_RANK), 0.02),
        "w_ukv": nrm(ks[5], (DEPTH, KV_LORA_RANK, MLA_HEADS * (MLA_NOPE_DIM + MLA_V_DIM)),
                     KV_LORA_RANK ** -0.5),
        "conv_w": nrm(ks[6], (DEPTH, CONV_K, CONV_WIDTH), CONV_K ** -0.5),
        "w_out": nrm(ks[7], (DEPTH, D_MIX, D_MODEL), D_MIX ** -0.5) * DEEPNORM_BETA,
        "ln1_g": 1.0 + nrm(ks[8], (DEPTH, D_MODEL), 0.02),
        "ln1_b": nrm(ks[9], (DEPTH, D_MODEL), 0.02),
        "w_gate_up": nrm(ks[10], (DEPTH, D_MODEL, 2 * FFN_DIM), D_MODEL ** -0.5),
        "ffn_conv_w": nrm(ks[11], (DEPTH, FFN_CONV_K, 2 * FFN_DIM), FFN_CONV_K ** -0.5),
        "w_down": nrm(ks[12], (DEPTH, FFN_DIM, D_MODEL), FFN_DIM ** -0.5) * DEEPNORM_BETA,
        "ln2_g": 1.0 + nrm(ks[13], (DEPTH, D_MODEL), 0.02),
        "ln2_b": nrm(ks[14], (DEPTH, D_MODEL), 0.02),
    }


def reference(x, positions, w_in, q_norm_g, w_uq, kv_norm_g, w_ukv, conv_w, w_out,
              ln1_g, ln1_b, w_gate_up, ffn_conv_w, w_down, ln2_g, ln2_b):
    for l in range(DEPTH):
        mix = hybrid_mixer(x, positions, w_in[l], q_norm_g[l], w_uq[l], kv_norm_g[l],
                           w_ukv[l], conv_w[l], w_out[l])
        x = layer_norm(DEEPNORM_ALPHA * x + mix, ln1_g[l], ln1_b[l])
        ffn = conv_ffn(x, w_gate_up[l], ffn_conv_w[l], w_down[l])
        x = layer_norm(DEEPNORM_ALPHA * x + ffn, ln2_g[l], ln2_b[l])
    return x
```

```python
import functools
import math

import jax
import jax.numpy as jnp
from jax import lax
from jax.experimental import pallas as pl
from jax.experimental.pallas import tpu as pltpu

D_MODEL = 2048
MLA_HEADS = 8
NOPE = 128
ROPE = 64
V_DIM = 128
MLA_WIDTH = MLA_HEADS * V_DIM
Q_RANK = 512
KV_RANK = 256
CONV_WIDTH = 1024
FFN_DIM = 5632
ROPE_THETA = 10000.0
LN_EPS = 1e-5
RMS_EPS = 1e-6
DEPTH = 1
ALPHA = (2.0 * DEPTH) ** 0.25
QK_DIM = NOPE + ROPE
Q_SCALE = (QK_DIM ** -0.5) * math.log2(math.e)
QK_PAD = 256
HALO = 8
NEG = -1e30

F32 = jnp.float32
BF16 = jnp.bfloat16
MIB = 1024 * 1024


def _dot(a, b):
    return jnp.dot(a, b, preferred_element_type=F32)


def _dot_nt(a, b):
    return lax.dot_general(a, b, (((1,), (1,)), ((), ())), preferred_element_type=F32)


def _rms(x, g):
    return x * lax.rsqrt(jnp.mean(x * x, axis=-1, keepdims=True) + RMS_EPS) * g


def _layer_norm(y, g, b):
    mu = jnp.mean(y, axis=-1, keepdims=True)
    d = y - mu
    var = jnp.mean(d * d, axis=-1, keepdims=True)
    return d * lax.rsqrt(var + LN_EPS) * g + b


def _causal_conv3(z, zs_ref, cw, cols, tm):
    zs_ref[HALO:HALO + tm, cols] = z
    z1 = zs_ref[HALO - 1:HALO - 1 + tm, cols]
    z2 = zs_ref[HALO - 2:HALO - 2 + tm, cols]
    return cw[2:3, :] * z + cw[1:2, :] * z1 + cw[0:1, :] * z2


def _latent_kernel(x_ref, posc_ref, posr_ref, invf_row_ref, invf_col_ref, wlat_ref, wkr_ref,
                   gq_ref, gkv_ref, wqn_ref, wqr_ref, wqrot_ref, wkt_ref, wv_ref,
                   q_ref, kt_ref, v_ref, *, tm):
    xb = x_ref[...].astype(BF16)
    h = _dot(xb, wlat_ref[...])
    cqn = _rms(h[:, :Q_RANK], gq_ref[...]).astype(BF16)
    ckvn = _rms(h[:, Q_RANK:], gkv_ref[...]).astype(BF16)

    qn = _dot(cqn, wqn_ref[...])
    qr = _dot(cqn, wqr_ref[...])
    qrot = _dot(cqn, wqrot_ref[...])
    ang = posc_ref[...].astype(F32) * invf_row_ref[...]
    cos_r, sin_r = jnp.cos(ang), jnp.sin(ang)
    for hp in range(MLA_HEADS // 2):
        lanes = slice(128 * hp, 128 * hp + 128)
        rope_pair = (qr[:, lanes] * cos_r + qrot[:, lanes] * sin_r).astype(BF16)
        for hh in (2 * hp, 2 * hp + 1):
            q_ref[0, hh, :, 0:NOPE] = qn[:, NOPE * hh:NOPE * hh + NOPE].astype(BF16)
            q_ref[0, hh, :, NOPE:QK_PAD] = rope_pair

    vv = _dot(ckvn, wv_ref[...])
    for hh in range(MLA_HEADS):
        v_ref[0, hh] = vv[:, V_DIM * hh:V_DIM * hh + V_DIM].astype(BF16)

    ktn = _dot_nt(wkt_ref[...], ckvn)
    kr2 = _dot_nt(wkr_ref[...], xb)
    ang_t = invf_col_ref[...] * posr_ref[...].astype(F32)
    ktr = (kr2[:ROPE] * jnp.cos(ang_t) + kr2[ROPE:] * jnp.sin(ang_t)).astype(BF16)
    zeros = jnp.zeros((ROPE, tm), BF16)
    for hh in range(MLA_HEADS):
        kt_ref[0, hh, 0, 0:NOPE, :] = ktn[NOPE * hh:NOPE * hh + NOPE].astype(BF16)
        lo, hi = (ktr, zeros) if hh % 2 == 0 else (zeros, ktr)
        kt_ref[0, hh, 0, NOPE:NOPE + ROPE, :] = lo
        kt_ref[0, hh, 0, NOPE + ROPE:QK_PAD, :] = hi


def _convmix_kernel(x_ref, wb_ref, wc_ref, wx_ref, cw_ref, o_ref, zs_ref, *, tm, tc, tiles_per_seq):
    @pl.when(pl.program_id(0) % tiles_per_seq == 0)
    def _():
        zs_ref[0:HALO, :] = jnp.zeros((HALO, CONV_WIDTH), F32)

    xb = x_ref[...].astype(BF16)
    for c in range(CONV_WIDTH // tc):
        cols = slice(c * tc, (c + 1) * tc)
        gate_b = _dot(xb, wb_ref[:, cols])
        z = _dot(xb, wc_ref[:, cols]) * _dot(xb, wx_ref[:, cols])
        y = _causal_conv3(z, zs_ref, cw_ref[:, cols], cols, tm)
        o_ref[:, cols] = (gate_b * y).astype(BF16)
    zs_ref[0:HALO, :] = zs_ref[tm:tm + HALO, :]


def _attn_kernel(q_ref, kt_ref, v_ref, o_ref, *, tq, heads):
    qi = pl.program_id(2)
    qs = [q_ref[0, g] for g in range(heads)]

    def step(j, carry, masked):
        out = []
        for g in range(heads):
            m, l, acc = carry[g]
            s = _dot(qs[g], kt_ref[0, g, j])
            if masked:
                row = lax.broadcasted_iota(jnp.int32, s.shape, 0)
                col = lax.broadcasted_iota(jnp.int32, s.shape, 1)
                s = jnp.where(col <= row, s, NEG)
            m_new = jnp.maximum(m, jnp.max(s, axis=-1, keepdims=True))
            p = jnp.exp2(s - m_new)
            a = jnp.exp2(m - m_new)
            l = a * l + jnp.sum(p, axis=-1, keepdims=True)
            start = pl.multiple_of(j * tq, tq)
            acc = a * acc + _dot(p.astype(BF16), v_ref[0, g, pl.ds(start, tq), :])
            out.append((m_new, l, acc))
        return tuple(out)

    init = tuple((jnp.full((tq, 1), NEG, F32), jnp.zeros((tq, 1), F32), jnp.zeros((tq, V_DIM), F32))
                 for _ in range(heads))
    carry = lax.fori_loop(0, qi, lambda j, c: step(j, c, False), init)
    carry = step(qi, carry, True)
    for g in range(heads):
        _, l, acc = carry[g]
        o_ref[0, :, V_DIM * g:V_DIM * g + V_DIM] = (acc / l).astype(BF16)


def _outproj_kernel(a_ref, c_ref, x_ref, w_ref, g_ref, b_ref, o32_ref, o16_ref):
    y = _dot(a_ref[...], w_ref[0:MLA_WIDTH, :]) + _dot(c_ref[...], w_ref[MLA_WIDTH:, :])
    y = y + ALPHA * x_ref[...]
    out = _layer_norm(y, g_ref[...], b_ref[...])
    o32_ref[...] = out
    o16_ref[...] = out.astype(BF16)


def _ffn_up_kernel(x_ref, wg_ref, wu_ref, cwg_ref, cwu_ref, o_ref, gs_ref, us_ref, *, tm, tn, tiles_per_seq):
    @pl.when(pl.program_id(1) % tiles_per_seq == 0)
    def _():
        gs_ref[0:HALO, :] = jnp.zeros((HALO, tn), F32)
        us_ref[0:HALO, :] = jnp.zeros((HALO, tn), F32)

    xb = x_ref[...]
    allc = slice(0, tn)
    g = _causal_conv3(_dot(xb, wg_ref[...]), gs_ref, cwg_ref[...], allc, tm)
    u = _causal_conv3(_dot(xb, wu_ref[...]), us_ref, cwu_ref[...], allc, tm)
    o_ref[...] = (g / (1.0 + jnp.exp(-g)) * u).astype(BF16)
    gs_ref[0:HALO, :] = gs_ref[tm:tm + HALO, :]
    us_ref[0:HALO, :] = us_ref[tm:tm + HALO, :]


def _ffn_down_kernel(h_ref, w_ref, x_ref, g_ref, b_ref, o_ref, acc_ref, *, nk):
    k = pl.program_id(1)

    @pl.when(k == 0)
    def _():
        acc_ref[...] = jnp.zeros_like(acc_ref)

    acc_ref[...] += _dot(h_ref[...], w_ref[...])

    @pl.when(k == nk - 1)
    def _():
        y = acc_ref[...] + ALPHA * x_ref[...]
        o_ref[...] = _layer_norm(y, g_ref[...], b_ref[...])


def _params(sem, vmem_mib):
    return pltpu.CompilerParams(dimension_semantics=sem, vmem_limit_bytes=vmem_mib * MIB)


def _layer(x2, pos, w_in, q_norm_g, w_uq, kv_norm_g, w_ukv, conv_w, w_out, ln1_g, ln1_b,
           w_gate_up, ffn_conv_w, w_down, ln2_g, ln2_b, batch, seq):
    T = batch * seq

    wlat = w_in[:, :Q_RANK + KV_RANK].astype(BF16)
    wkr = w_in[:, 768:832]
    wkr2 = jnp.concatenate([wkr, -wkr[:, ROPE // 2:], wkr[:, :ROPE // 2]], axis=1).T.astype(BF16)
    wb = w_in[:, 832:832 + CONV_WIDTH].astype(BF16)
    wc = w_in[:, 832 + CONV_WIDTH:832 + 2 * CONV_WIDTH].astype(BF16)
    wx = w_in[:, 832 + 2 * CONV_WIDTH:].astype(BF16)
    uq = w_uq.reshape(Q_RANK, MLA_HEADS, QK_DIM)
    wqn = uq[:, :, :NOPE].reshape(Q_RANK, MLA_HEADS * NOPE).astype(BF16)
    wqr = uq[:, :, NOPE:].reshape(Q_RANK, MLA_HEADS * ROPE).astype(BF16)
    wqrot = jnp.concatenate([-uq[:, :, NOPE + ROPE // 2:], uq[:, :, NOPE:NOPE + ROPE // 2]], axis=-1)
    wqrot = wqrot.reshape(Q_RANK, MLA_HEADS * ROPE).astype(BF16)
    ukv = w_ukv.reshape(KV_RANK, MLA_HEADS, NOPE + V_DIM)
    wkt = ukv[:, :, :NOPE].reshape(KV_RANK, MLA_HEADS * NOPE).T.astype(BF16)
    wv = ukv[:, :, NOPE:].reshape(KV_RANK, MLA_HEADS * V_DIM).astype(BF16)
    gq = (q_norm_g * Q_SCALE).reshape(1, Q_RANK)
    gkv = kv_norm_g.reshape(1, KV_RANK)
    inv_freq = ROPE_THETA ** (-jnp.arange(0, ROPE, 2, dtype=F32) / ROPE)
    invf_row = jnp.tile(inv_freq, 4).reshape(1, 128)
    invf_col = jnp.tile(inv_freq, 2).reshape(ROPE, 1)
    posc = pos.reshape(T, 1)
    posr = pos.reshape(1, T)

    tm = 512
    nblk = seq // tm
    full = lambda shape: pl.BlockSpec(shape, lambda i: (0,) * len(shape))
    q, kt, v = pl.pallas_call(
        functools.partial(_latent_kernel, tm=tm),
        grid=(T // tm,),
        in_specs=[
            pl.BlockSpec((tm, D_MODEL), lambda i: (i, 0)),
            pl.BlockSpec((tm, 1), lambda i: (i, 0)),
            pl.BlockSpec((1, tm), lambda i: (0, i)),
            full((1, 128)), full((ROPE, 1)),
            full((D_MODEL, Q_RANK + KV_RANK)), full((128, D_MODEL)),
            full((1, Q_RANK)), full((1, KV_RANK)),
            full((Q_RANK, MLA_HEADS * NOPE)), full((Q_RANK, MLA_HEADS * ROPE)), full((Q_RANK, MLA_HEADS * ROPE)),
            full((MLA_HEADS * NOPE, KV_RANK)), full((KV_RANK, MLA_HEADS * V_DIM)),
        ],
        out_specs=[
            pl.BlockSpec((1, MLA_HEADS, tm, QK_PAD), lambda i: (i // nblk, 0, i % nblk, 0)),
            pl.BlockSpec((1, MLA_HEADS, 1, QK_PAD, tm), lambda i: (i // nblk, 0, i % nblk, 0, 0)),
            pl.BlockSpec((1, MLA_HEADS, tm, V_DIM), lambda i: (i // nblk, 0, i % nblk, 0)),
        ],
        out_shape=[
            jax.ShapeDtypeStruct((batch, MLA_HEADS, seq, QK_PAD), BF16),
            jax.ShapeDtypeStruct((batch, MLA_HEADS, nblk, QK_PAD, tm), BF16),
            jax.ShapeDtypeStruct((batch, MLA_HEADS, seq, V_DIM), BF16),
        ],
        compiler_params=_params(("arbitrary",), 48),
        name="latent",
    )(x2, posc, posr, invf_row, invf_col, wlat, wkr2, gq, gkv, wqn, wqr, wqrot, wkt, wv)

    tm2 = 512
    conv_out = pl.pallas_call(
        functools.partial(_convmix_kernel, tm=tm2, tc=256, tiles_per_seq=seq // tm2),
        grid=(T // tm2,),
        in_specs=[
            pl.BlockSpec((tm2, D_MODEL), lambda i: (i, 0)),
            full((D_MODEL, CONV_WIDTH)), full((D_MODEL, CONV_WIDTH)), full((D_MODEL, CONV_WIDTH)),
            full((3, CONV_WIDTH)),
        ],
        out_specs=pl.BlockSpec((tm2, CONV_WIDTH), lambda i: (i, 0)),
        out_shape=jax.ShapeDtypeStruct((T, CONV_WIDTH), BF16),
        scratch_shapes=[pltpu.VMEM((tm2 + HALO, CONV_WIDTH), F32)],
        compiler_params=_params(("arbitrary",), 48),
        name="convmix",
    )(x2, wb, wc, wx, conv_w)

    tq = tm
    hg = 2
    attn = pl.pallas_call(
        functools.partial(_attn_kernel, tq=tq, heads=hg),
        grid=(batch, MLA_HEADS // hg, seq // tq),
        in_specs=[
            pl.BlockSpec((1, hg, tq, QK_PAD), lambda b, h, i: (b, h, i, 0)),
            pl.BlockSpec((1, hg, nblk, QK_PAD, tq), lambda b, h, i: (b, h, 0, 0, 0)),
            pl.BlockSpec((1, hg, seq, V_DIM), lambda b, h, i: (b, h, 0, 0)),
        ],
        out_specs=pl.BlockSpec((1, tq, hg * V_DIM), lambda b, h, i: (b, i, h)),
        out_shape=jax.ShapeDtypeStruct((batch, seq, MLA_WIDTH), BF16),
        compiler_params=_params(("arbitrary", "arbitrary", "arbitrary"), 48),
        name="attn",
    )(q, kt, v)
    attn = attn.reshape(T, MLA_WIDTH)

    tm4 = 512
    x1, x1b = pl.pallas_call(
        _outproj_kernel,
        grid=(T // tm4,),
        in_specs=[
            pl.BlockSpec((tm4, MLA_WIDTH), lambda i: (i, 0)),
            pl.BlockSpec((tm4, CONV_WIDTH), lambda i: (i, 0)),
            pl.BlockSpec((tm4, D_MODEL), lambda i: (i, 0)),
            full((D_MODEL, D_MODEL)), full((1, D_MODEL)), full((1, D_MODEL)),
        ],
        out_specs=[pl.BlockSpec((tm4, D_MODEL), lambda i: (i, 0)), pl.BlockSpec((tm4, D_MODEL), lambda i: (i, 0))],
        out_shape=[jax.ShapeDtypeStruct((T, D_MODEL), F32), jax.ShapeDtypeStruct((T, D_MODEL), BF16)],
        compiler_params=_params(("arbitrary",), 56),
        name="outproj",
    )(attn, conv_out, x2, w_out.astype(BF16), ln1_g.reshape(1, D_MODEL), ln1_b.reshape(1, D_MODEL))

    tm5, tn5 = 1024, 512
    nn5 = FFN_DIM // tn5
    wgu = w_gate_up.astype(BF16)
    h2 = pl.pallas_call(
        functools.partial(_ffn_up_kernel, tm=tm5, tn=tn5, tiles_per_seq=seq // tm5),
        grid=(nn5, T // tm5),
        in_specs=[
            pl.BlockSpec((tm5, D_MODEL), lambda n, m: (m, 0)),
            pl.BlockSpec((D_MODEL, tn5), lambda n, m: (0, n)),
            pl.BlockSpec((D_MODEL, tn5), lambda n, m: (0, n + nn5)),
            pl.BlockSpec((3, tn5), lambda n, m: (0, n)),
            pl.BlockSpec((3, tn5), lambda n, m: (0, n + nn5)),
        ],
        out_specs=pl.BlockSpec((tm5, tn5), lambda n, m: (m, n)),
        out_shape=jax.ShapeDtypeStruct((T, FFN_DIM), BF16),
        scratch_shapes=[pltpu.VMEM((tm5 + HALO, tn5), F32), pltpu.VMEM((tm5 + HALO, tn5), F32)],
        compiler_params=_params(("arbitrary", "arbitrary"), 48),
        name="ffn_up",
    )(x1b, wgu, wgu, ffn_conv_w, ffn_conv_w)

    tm6, tk6 = 1024, 512
    nk6 = FFN_DIM // tk6
    out = pl.pallas_call(
        functools.partial(_ffn_down_kernel, nk=nk6),
        grid=(T // tm6, nk6),
        in_specs=[
            pl.BlockSpec((tm6, tk6), lambda m, k: (m, k)),
            pl.BlockSpec((tk6, D_MODEL), lambda m, k: (k, 0)),
            pl.BlockSpec((tm6, D_MODEL), lambda m, k: (m, 0)),
            pl.BlockSpec((1, D_MODEL), lambda m, k: (0, 0)),
            pl.BlockSpec((1, D_MODEL), lambda m, k: (0, 0)),
        ],
        out_specs=pl.BlockSpec((tm6, D_MODEL), lambda m, k: (m, 0)),
        out_shape=jax.ShapeDtypeStruct((T, D_MODEL), F32),
        scratch_shapes=[pltpu.VMEM((tm6, D_MODEL), F32)],
        compiler_params=_params(("arbitrary", "arbitrary"), 56),
        name="ffn_down",
    )(h2, w_down.astype(BF16), x1, ln2_g.reshape(1, D_MODEL), ln2_b.reshape(1, D_MODEL))
    return out


def kernel(x, positions, w_in, q_norm_g, w_uq, kv_norm_g, w_ukv, conv_w, w_out, ln1_g, ln1_b,
           w_gate_up, ffn_conv_w, w_down, ln2_g, ln2_b):
    batch, seq, _ = x.shape
    x2 = x.reshape(batch * seq, D_MODEL)
    for l in range(DEPTH):
        x2 = _layer(x2, positions, w_in[l], q_norm_g[l], w_uq[l], kv_norm_g[l], w_ukv[l], conv_w[l],
                    w_out[l], ln1_g[l], ln1_b[l], w_gate_up[l], ffn_conv_w[l], w_down[l], ln2_g[l], ln2_b[l],
                    batch, seq)
    return x2.reshape(batch, seq, D_MODEL)
```

```python
import functools
import math

import jax
import jax.numpy as jnp
from jax import lax
from jax.experimental import pallas as pl
from jax.experimental.pallas import tpu as pltpu

D_MODEL = 2048
MLA_HEADS = 8
NOPE = 128
ROPE = 64
V_DIM = 128
MLA_WIDTH = MLA_HEADS * V_DIM
Q_RANK = 512
KV_RANK = 256
CONV_WIDTH = 1024
FFN_DIM = 5632
ROPE_THETA = 10000.0
LN_EPS = 1e-5
RMS_EPS = 1e-6
DEPTH = 1
ALPHA = (2.0 * DEPTH) ** 0.25
QK_DIM = NOPE + ROPE
Q_SCALE = (QK_DIM ** -0.5) * math.log2(math.e)
QK_PAD = 256
HALO = 8
NEG = -1e30

F32 = jnp.float32
BF16 = jnp.bfloat16
MIB = 1024 * 1024


def _dot(a, b):
    return jnp.dot(a, b, preferred_element_type=F32)


def _dot_nt(a, b):
    return lax.dot_general(a, b, (((1,), (1,)), ((), ())), preferred_element_type=F32)


def _rms(x, g):
    return x * lax.rsqrt(jnp.mean(x * x, axis=-1, keepdims=True) + RMS_EPS) * g


def _layer_norm(y, g, b):
    mu = jnp.mean(y, axis=-1, keepdims=True)
    d = y - mu
    var = jnp.mean(d * d, axis=-1, keepdims=True)
    return d * lax.rsqrt(var + LN_EPS) * g + b


def _causal_conv3(z, zs_ref, cw, cols, tm):
    zs_ref[HALO:HALO + tm, cols] = z
    z1 = zs_ref[HALO - 1:HALO - 1 + tm, cols]
    z2 = zs_ref[HALO - 2:HALO - 2 + tm, cols]
    return cw[2:3, :] * z + cw[1:2, :] * z1 + cw[0:1, :] * z2


def _latent_kernel(x_ref, posr_ref, invf_ref, wlat_ref, gq_ref, gkv_ref, wqt_ref, wk_ref, wvt_ref,
                   qt_ref, kn_ref, kr_ref, vt_ref, *, tm, tk):
    half = ROPE // 2
    xb = x_ref[...].astype(BF16)
    h = _dot(xb, wlat_ref[...])
    cqn = _rms(h[:, :Q_RANK], gq_ref[...]).astype(BF16)
    ckvn = _rms(h[:, Q_RANK:Q_RANK + KV_RANK], gkv_ref[...]).astype(BF16)

    ang_t = invf_ref[...] * posr_ref[...].astype(F32)
    cos_t, sin_t = jnp.cos(ang_t), jnp.sin(ang_t)
    table = jnp.concatenate([cos_t, sin_t, cos_t, sin_t], axis=0).T
    kr_ref[0] = (h[:, Q_RANK + KV_RANK:] * table).astype(BF16)

    kn = _dot(ckvn, wk_ref[...])
    vt = _dot_nt(wvt_ref[...], ckvn)
    qt = _dot_nt(wqt_ref[...], cqn)
    for hh in range(MLA_HEADS):
        kn_ref[0, hh] = kn[:, NOPE * hh:NOPE * hh + NOPE].astype(BF16)
        for c in range(tm // tk):
            vt_ref[0, hh, c] = vt[V_DIM * hh:V_DIM * hh + V_DIM, tk * c:tk * c + tk].astype(BF16)
        base = QK_DIM * hh
        r1 = qt[base + NOPE:base + NOPE + half]
        r2 = qt[base + NOPE + half:base + QK_DIM]
        qr1 = (r1 * cos_t - r2 * sin_t).astype(BF16)
        qr2 = (r2 * cos_t + r1 * sin_t).astype(BF16)
        qt_ref[0, hh, 0:NOPE, :] = qt[base:base + NOPE].astype(BF16)
        qt_ref[0, hh, NOPE:NOPE + half, :] = qr1
        qt_ref[0, hh, NOPE + half:NOPE + 2 * half, :] = qr2
        qt_ref[0, hh, NOPE + 2 * half:NOPE + 3 * half, :] = qr2
        qt_ref[0, hh, NOPE + 3 * half:QK_PAD, :] = -qr1


def _convmix_kernel(x_ref, wb_ref, wc_ref, wx_ref, cw_ref, o_ref, zs_ref, *, tm, tc, tiles_per_seq):
    @pl.when(pl.program_id(0) % tiles_per_seq == 0)
    def _():
        zs_ref[0:HALO, :] = jnp.zeros((HALO, CONV_WIDTH), F32)

    xb = x_ref[...].astype(BF16)
    for c in range(CONV_WIDTH // tc):
        cols = slice(c * tc, (c + 1) * tc)
        gate_b = _dot(xb, wb_ref[:, cols])
        z = _dot(xb, wc_ref[:, cols]) * _dot(xb, wx_ref[:, cols])
        y = _causal_conv3(z, zs_ref, cw_ref[:, cols], cols, tm)
        o_ref[:, cols] = (gate_b * y).astype(BF16)
    zs_ref[0:HALO, :] = zs_ref[tm:tm + HALO, :]


def _attn_kernel(qt_ref, kn_ref, kr_ref, vt_ref, o_ref, m_ref, l_ref, acc_ref, *, tq, tk, heads, lookahead):
    qi = pl.program_id(1)
    per_q = tq // tk
    m_ref[...] = jnp.full(m_ref.shape, NEG, F32)
    l_ref[...] = jnp.zeros(l_ref.shape, F32)
    acc_ref[...] = jnp.zeros(acc_ref.shape, F32)

    def step(j, masked):
        start = pl.multiple_of(j * tk, tk)
        kr = kr_ref[0, pl.ds(start, tk), :]
        if masked:
            key = start + lax.broadcasted_iota(jnp.int32, (tk, tq), 0)
            qry = qi * tq + lax.broadcasted_iota(jnp.int32, (tk, tq), 1)
            keep = key <= qry

        def scores(g):
            k = jnp.concatenate([kn_ref[0, g, pl.ds(start, tk), :], kr], axis=1)
            return _dot(k, qt_ref[0, g])

        ahead = [scores(g) for g in range(lookahead)]
        for g in range(heads):
            s = ahead.pop(0)
            if g + lookahead < heads:
                ahead.append(scores(g + lookahead))
            if masked:
                s = jnp.where(keep, s, NEG)
            m = m_ref[g]
            m_new = jnp.maximum(m, jnp.max(s, axis=0, keepdims=True))
            p = jnp.exp2(s - m_new)
            a = jnp.exp2(m - m_new)
            m_ref[g] = m_new
            l_ref[g] = a * l_ref[g] + jnp.sum(p, axis=0, keepdims=True)
            acc_ref[g] = a * acc_ref[g] + _dot(vt_ref[0, g, j], p.astype(BF16))

    def unmasked(j, c):
        step(j, False)
        return c

    def diagonal(j, c):
        step(j, True)
        return c

    lax.fori_loop(0, qi * per_q, unmasked, 0)
    lax.fori_loop(qi * per_q, (qi + 1) * per_q, diagonal, 0)
    for g in range(heads):
        o_ref[0, :, V_DIM * g:V_DIM * g + V_DIM] = (acc_ref[g] / l_ref[g]).T.astype(BF16)


def _outproj_kernel(a_ref, c_ref, x_ref, w_ref, g_ref, b_ref, o32_ref, o16_ref):
    y = _dot(a_ref[...], w_ref[0:MLA_WIDTH, :]) + _dot(c_ref[...], w_ref[MLA_WIDTH:, :])
    y = y + ALPHA * x_ref[...]
    out = _layer_norm(y, g_ref[...], b_ref[...])
    o32_ref[...] = out
    o16_ref[...] = out.astype(BF16)


def _ffn_up_kernel(x_ref, wg_ref, wu_ref, cwg_ref, cwu_ref, o_ref, gs_ref, us_ref, *, tm, tn, tiles_per_seq):
    @pl.when(pl.program_id(1) % tiles_per_seq == 0)
    def _():
        gs_ref[0:HALO, :] = jnp.zeros((HALO, tn), F32)
        us_ref[0:HALO, :] = jnp.zeros((HALO, tn), F32)

    xb = x_ref[...]
    allc = slice(0, tn)
    g = _causal_conv3(_dot(xb, wg_ref[...]), gs_ref, cwg_ref[...], allc, tm)
    u = _causal_conv3(_dot(xb, wu_ref[...]), us_ref, cwu_ref[...], allc, tm)
    o_ref[...] = (g / (1.0 + jnp.exp(-g)) * u).astype(BF16)
    gs_ref[0:HALO, :] = gs_ref[tm:tm + HALO, :]
    us_ref[0:HALO, :] = us_ref[tm:tm + HALO, :]


def _ffn_down_kernel(h_ref, w_ref, x_ref, g_ref, b_ref, o_ref, acc_ref, *, nk):
    k = pl.program_id(1)

    @pl.when(k == 0)
    def _():
        acc_ref[...] = jnp.zeros_like(acc_ref)

    acc_ref[...] += _dot(h_ref[...], w_ref[...])

    @pl.when(k == nk - 1)
    def _():
        y = acc_ref[...] + ALPHA * x_ref[...]
        o_ref[...] = _layer_norm(y, g_ref[...], b_ref[...])


def _params(sem, vmem_mib):
    return pltpu.CompilerParams(dimension_semantics=sem, vmem_limit_bytes=vmem_mib * MIB)


def _layer(x2, pos, w_in, q_norm_g, w_uq, kv_norm_g, w_ukv, conv_w, w_out, ln1_g, ln1_b,
           w_gate_up, ffn_conv_w, w_down, ln2_g, ln2_b, batch, seq):
    T = batch * seq

    half = ROPE // 2
    wk1, wk2 = w_in[:, 768:768 + half], w_in[:, 768 + half:832]
    wlat = jnp.concatenate([w_in[:, :Q_RANK + KV_RANK], wk1, wk1, wk2, wk2], axis=1).astype(BF16)
    wb = w_in[:, 832:832 + CONV_WIDTH].astype(BF16)
    wc = w_in[:, 832 + CONV_WIDTH:832 + 2 * CONV_WIDTH].astype(BF16)
    wx = w_in[:, 832 + 2 * CONV_WIDTH:].astype(BF16)
    wqt = w_uq.T.astype(BF16)
    ukv = w_ukv.reshape(KV_RANK, MLA_HEADS, NOPE + V_DIM)
    wk = ukv[:, :, :NOPE].reshape(KV_RANK, MLA_HEADS * NOPE).astype(BF16)
    wvt = ukv[:, :, NOPE:].reshape(KV_RANK, MLA_HEADS * V_DIM).T.astype(BF16)
    gq = (q_norm_g * Q_SCALE).reshape(1, Q_RANK)
    gkv = kv_norm_g.reshape(1, KV_RANK)
    inv_freq = (ROPE_THETA ** (-jnp.arange(0, ROPE, 2, dtype=F32) / ROPE)).reshape(half, 1)
    posr = pos.reshape(1, T)

    tm = 512
    tk = 256
    nblk = seq // tm
    full = lambda shape: pl.BlockSpec(shape, lambda i: (0,) * len(shape))
    qt, kn, kr, vt = pl.pallas_call(
        functools.partial(_latent_kernel, tm=tm, tk=tk),
        grid=(T // tm,),
        in_specs=[
            pl.BlockSpec((tm, D_MODEL), lambda i: (i, 0)),
            pl.BlockSpec((1, tm), lambda i: (0, i)),
            full((half, 1)),
            full((D_MODEL, Q_RANK + KV_RANK + 128)),
            full((1, Q_RANK)), full((1, KV_RANK)),
            full((MLA_HEADS * QK_DIM, Q_RANK)),
            full((KV_RANK, MLA_HEADS * NOPE)),
            full((MLA_HEADS * V_DIM, KV_RANK)),
        ],
        out_specs=[
            pl.BlockSpec((1, MLA_HEADS, QK_PAD, tm), lambda i: (i // nblk, 0, 0, i % nblk)),
            pl.BlockSpec((1, MLA_HEADS, tm, NOPE), lambda i: (i // nblk, 0, i % nblk, 0)),
            pl.BlockSpec((1, tm, 128), lambda i: (i // nblk, i % nblk, 0)),
            pl.BlockSpec((1, MLA_HEADS, tm // tk, V_DIM, tk), lambda i: (i // nblk, 0, i % nblk, 0, 0)),
        ],
        out_shape=[
            jax.ShapeDtypeStruct((batch, MLA_HEADS, QK_PAD, seq), BF16),
            jax.ShapeDtypeStruct((batch, MLA_HEADS, seq, NOPE), BF16),
            jax.ShapeDtypeStruct((batch, seq, 128), BF16),
            jax.ShapeDtypeStruct((batch, MLA_HEADS, seq // tk, V_DIM, tk), BF16),
        ],
        compiler_params=_params(("arbitrary",), 48),
        name="latent",
    )(x2, posr, inv_freq, wlat, gq, gkv, wqt, wk, wvt)

    tm2 = 512
    conv_out = pl.pallas_call(
        functools.partial(_convmix_kernel, tm=tm2, tc=256, tiles_per_seq=seq // tm2),
        grid=(T // tm2,),
        in_specs=[
            pl.BlockSpec((tm2, D_MODEL), lambda i: (i, 0)),
            full((D_MODEL, CONV_WIDTH)), full((D_MODEL, CONV_WIDTH)), full((D_MODEL, CONV_WIDTH)),
            full((3, CONV_WIDTH)),
        ],
        out_specs=pl.BlockSpec((tm2, CONV_WIDTH), lambda i: (i, 0)),
        out_shape=jax.ShapeDtypeStruct((T, CONV_WIDTH), BF16),
        scratch_shapes=[pltpu.VMEM((tm2 + HALO, CONV_WIDTH), F32)],
        compiler_params=_params(("arbitrary",), 48),
        name="convmix",
    )(x2, wb, wc, wx, conv_w)

    tq = 512
    attn = pl.pallas_call(
        functools.partial(_attn_kernel, tq=tq, tk=tk, heads=MLA_HEADS, lookahead=2),
        grid=(batch, seq // tq),
        in_specs=[
            pl.BlockSpec((1, MLA_HEADS, QK_PAD, tq), lambda b, i: (b, 0, 0, i)),
            pl.BlockSpec((1, MLA_HEADS, seq, NOPE), lambda b, i: (b, 0, 0, 0)),
            pl.BlockSpec((1, seq, 128), lambda b, i: (b, 0, 0)),
            pl.BlockSpec((1, MLA_HEADS, seq // tk, V_DIM, tk), lambda b, i: (b, 0, 0, 0, 0)),
        ],
        out_specs=pl.BlockSpec((1, tq, MLA_WIDTH), lambda b, i: (b, i, 0)),
        out_shape=jax.ShapeDtypeStruct((batch, seq, MLA_WIDTH), BF16),
        scratch_shapes=[pltpu.VMEM((MLA_HEADS, 1, tq), F32), pltpu.VMEM((MLA_HEADS, 1, tq), F32),
                        pltpu.VMEM((MLA_HEADS, V_DIM, tq), F32)],
        compiler_params=_params(("arbitrary", "arbitrary"), 56),
        name="attn",
    )(qt, kn, kr, vt)
    attn = attn.reshape(T, MLA_WIDTH)

    tm4 = 512
    x1, x1b = pl.pallas_call(
        _outproj_kernel,
        grid=(T // tm4,),
        in_specs=[
            pl.BlockSpec((tm4, MLA_WIDTH), lambda i: (i, 0)),
            pl.BlockSpec((tm4, CONV_WIDTH), lambda i: (i, 0)),
            pl.BlockSpec((tm4, D_MODEL), lambda i: (i, 0)),
            full((D_MODEL, D_MODEL)), full((1, D_MODEL)), full((1, D_MODEL)),
        ],
        out_specs=[pl.BlockSpec((tm4, D_MODEL), lambda i: (i, 0)), pl.BlockSpec((tm4, D_MODEL), lambda i: (i, 0))],
        out_shape=[jax.ShapeDtypeStruct((T, D_MODEL), F32), jax.ShapeDtypeStruct((T, D_MODEL), BF16)],
        compiler_params=_params(("arbitrary",), 56),
        name="outproj",
    )(attn, conv_out, x2, w_out.astype(BF16), ln1_g.reshape(1, D_MODEL), ln1_b.reshape(1, D_MODEL))

    tm5, tn5 = 1024, 512
    nn5 = FFN_DIM // tn5
    wgu = w_gate_up.astype(BF16)
    h2 = pl.pallas_call(
        functools.partial(_ffn_up_kernel, tm=tm5, tn=tn5, tiles_per_seq=seq // tm5),
        grid=(nn5, T // tm5),
        in_specs=[
            pl.BlockSpec((tm5, D_MODEL), lambda n, m: (m, 0)),
            pl.BlockSpec((D_MODEL, tn5), lambda n, m: (0, n)),
            pl.BlockSpec((D_MODEL, tn5), lambda n, m: (0, n + nn5)),
            pl.BlockSpec((3, tn5), lambda n, m: (0, n)),
            pl.BlockSpec((3, tn5), lambda n, m: (0, n + nn5)),
        ],
        out_specs=pl.BlockSpec((tm5, tn5), lambda n, m: (m, n)),
        out_shape=jax.ShapeDtypeStruct((T, FFN_DIM), BF16),
        scratch_shapes=[pltpu.VMEM((tm5 + HALO, tn5), F32), pltpu.VMEM((tm5 + HALO, tn5), F32)],
        compiler_params=_params(("arbitrary", "arbitrary"), 48),
        name="ffn_up",
    )(x1b, wgu, wgu, ffn_conv_w, ffn_conv_w)

    tm6, tk6 = 1024, 512
    nk6 = FFN_DIM // tk6
    out = pl.pallas_call(
        functools.partial(_ffn_down_kernel, nk=nk6),
        grid=(T // tm6, nk6),
        in_specs=[
            pl.BlockSpec((tm6, tk6), lambda m, k: (m, k)),
            pl.BlockSpec((tk6, D_MODEL), lambda m, k: (k, 0)),
            pl.BlockSpec((tm6, D_MODEL), lambda m, k: (m, 0)),
            pl.BlockSpec((1, D_MODEL), lambda m, k: (0, 0)),
            pl.BlockSpec((1, D_MODEL), lambda m, k: (0, 0)),
        ],
        out_specs=pl.BlockSpec((tm6, D_MODEL), lambda m, k: (m, 0)),
        out_shape=jax.ShapeDtypeStruct((T, D_MODEL), F32),
        scratch_shapes=[pltpu.VMEM((tm6, D_MODEL), F32)],
        compiler_params=_params(("arbitrary", "arbitrary"), 56),
        name="ffn_down",
    )(h2, w_down.astype(BF16), x1, ln2_g.reshape(1, D_MODEL), ln2_b.reshape(1, D_MODEL))
    return out


def kernel(x, positions, w_in, q_norm_g, w_uq, kv_norm_g, w_ukv, conv_w, w_out, ln1_g, ln1_b,
           w_gate_up, ffn_conv_w, w_down, ln2_g, ln2_b):
    batch, seq, _ = x.shape
    x2 = x.reshape(batch * seq, D_MODEL)
    for l in range(DEPTH):
        x2 = _layer(x2, positions, w_in[l], q_norm_g[l], w_uq[l], kv_norm_g[l], w_ukv[l], conv_w[l],
                    w_out[l], ln1_g[l], ln1_b[l], w_gate_up[l], ffn_conv_w[l], w_down[l], ln2_g[l], ln2_b[l],
                    batch, seq)
    return x2.reshape(batch, seq, D_MODEL)
```

```python
import functools
import math

import jax
import jax.numpy as jnp
from jax import lax
from jax.experimental import pallas as pl
from jax.experimental.pallas import tpu as pltpu

D_MODEL = 2048
MLA_HEADS = 8
NOPE = 128
ROPE = 64
V_DIM = 128
MLA_WIDTH = MLA_HEADS * V_DIM
Q_RANK = 512
KV_RANK = 256
CONV_WIDTH = 1024
FFN_DIM = 5632
ROPE_THETA = 10000.0
LN_EPS = 1e-5
RMS_EPS = 1e-6
DEPTH = 1
ALPHA = (2.0 * DEPTH) ** 0.25
QK_DIM = NOPE + ROPE
Q_SCALE = (QK_DIM ** -0.5) * math.log2(math.e)
QK_PAD = 256
HALO = 8
NEG = -1e30

F32 = jnp.float32
BF16 = jnp.bfloat16
MIB = 1024 * 1024


def _dot(a, b):
    return jnp.dot(a, b, preferred_element_type=F32)


def _dot_nt(a, b):
    return lax.dot_general(a, b, (((1,), (1,)), ((), ())), preferred_element_type=F32)


def _rms(x, g):
    return x * lax.rsqrt(jnp.mean(x * x, axis=-1, keepdims=True) + RMS_EPS) * g


def _layer_norm(y, g, b):
    mu = jnp.mean(y, axis=-1, keepdims=True)
    d = y - mu
    var = jnp.mean(d * d, axis=-1, keepdims=True)
    return d * lax.rsqrt(var + LN_EPS) * g + b


def _causal_conv3(z, zs_ref, cw, cols, n, row0=0):
    zs_ref[HALO + row0:HALO + row0 + n, cols] = z
    z1 = zs_ref[HALO - 1 + row0:HALO - 1 + row0 + n, cols]
    z2 = zs_ref[HALO - 2 + row0:HALO - 2 + row0 + n, cols]
    return cw[2:3, :] * z + cw[1:2, :] * z1 + cw[0:1, :] * z2


def _latent_kernel(x_ref, posr_ref, invf_ref, wlat_ref, gq_ref, gkv_ref, wqt_ref, wk_ref, wvt_ref,
                   qt_ref, kn_ref, kr_ref, vt_ref, *, tm, tk):
    half = ROPE // 2
    xb = x_ref[...].astype(BF16)
    h = _dot(xb, wlat_ref[...])
    cqn = _rms(h[:, :Q_RANK], gq_ref[...]).astype(BF16)
    ckvn = _rms(h[:, Q_RANK:Q_RANK + KV_RANK], gkv_ref[...]).astype(BF16)

    ang_t = invf_ref[...] * posr_ref[...].astype(F32)
    cos_t, sin_t = jnp.cos(ang_t), jnp.sin(ang_t)
    table = jnp.concatenate([cos_t, sin_t, cos_t, sin_t], axis=0).T
    kr_ref[0] = (h[:, Q_RANK + KV_RANK:] * table).astype(BF16)

    kn = _dot(ckvn, wk_ref[...])
    vt = _dot_nt(wvt_ref[...], ckvn)
    qt = _dot_nt(wqt_ref[...], cqn)
    for hh in range(MLA_HEADS):
        kn_ref[0, hh] = kn[:, NOPE * hh:NOPE * hh + NOPE].astype(BF16)
        for c in range(tm // tk):
            vt_ref[0, hh, c] = vt[V_DIM * hh:V_DIM * hh + V_DIM, tk * c:tk * c + tk].astype(BF16)
        base = QK_DIM * hh
        r1 = qt[base + NOPE:base + NOPE + half]
        r2 = qt[base + NOPE + half:base + QK_DIM]
        qr1 = (r1 * cos_t - r2 * sin_t).astype(BF16)
        qr2 = (r2 * cos_t + r1 * sin_t).astype(BF16)
        qt_ref[0, hh, 0:NOPE, :] = qt[base:base + NOPE].astype(BF16)
        qt_ref[0, hh, NOPE:NOPE + half, :] = qr1
        qt_ref[0, hh, NOPE + half:NOPE + 2 * half, :] = qr2
        qt_ref[0, hh, NOPE + 2 * half:NOPE + 3 * half, :] = qr2
        qt_ref[0, hh, NOPE + 3 * half:QK_PAD, :] = -qr1


def _convmix_kernel(x_ref, w_ref, cw_ref, wo_ref, o_ref, wob_ref, zs_ref, *, tm, tc, tiles_per_seq):
    @pl.when(pl.program_id(0) % tiles_per_seq == 0)
    def _():
        zs_ref[0:HALO, :] = jnp.zeros((HALO, CONV_WIDTH), F32)

    wob_ref[...] = wo_ref[...].astype(BF16)

    xb = x_ref[...].astype(BF16)
    for c in range(CONV_WIDTH // tc):
        cols = slice(c * tc, (c + 1) * tc)
        gate_b = _dot(xb, w_ref[:, c * tc:(c + 1) * tc])
        gate_c = _dot(xb, w_ref[:, CONV_WIDTH + c * tc:CONV_WIDTH + (c + 1) * tc])
        x_c = _dot(xb, w_ref[:, 2 * CONV_WIDTH + c * tc:2 * CONV_WIDTH + (c + 1) * tc])
        y = _causal_conv3(gate_c * x_c, zs_ref, cw_ref[:, cols], cols, tm)
        o_ref[:, cols] = (gate_b * y).astype(BF16)
    zs_ref[0:HALO, :] = zs_ref[tm:tm + HALO, :]


def _attn_kernel(qt_ref, kn_ref, kr_ref, vt_ref, o_ref, m_ref, l_ref, acc_ref, *, tq, tk, heads, lookahead):
    qi = pl.program_id(1)
    per_q = tq // tk
    m_ref[...] = jnp.full(m_ref.shape, NEG, F32)
    l_ref[...] = jnp.zeros(l_ref.shape, F32)
    acc_ref[...] = jnp.zeros(acc_ref.shape, F32)

    def step(j, masked):
        start = pl.multiple_of(j * tk, tk)
        kr = kr_ref[0, pl.ds(start, tk), :]
        if masked:
            key = start + lax.broadcasted_iota(jnp.int32, (tk, tq), 0)
            qry = qi * tq + lax.broadcasted_iota(jnp.int32, (tk, tq), 1)
            keep = key <= qry

        def scores(g):
            k = jnp.concatenate([kn_ref[0, g, pl.ds(start, tk), :], kr], axis=1)
            return _dot(k, qt_ref[0, g])

        ahead = [scores(g) for g in range(lookahead)]
        for g in range(heads):
            s = ahead.pop(0)
            if g + lookahead < heads:
                ahead.append(scores(g + lookahead))
            if masked:
                s = jnp.where(keep, s, NEG)
            m = m_ref[g]
            m_new = jnp.maximum(m, jnp.max(s, axis=0, keepdims=True))
            p = jnp.exp2(s - m_new)
            a = jnp.exp2(m - m_new)
            m_ref[g] = m_new
            l_ref[g] = a * l_ref[g] + jnp.sum(p, axis=0, keepdims=True)
            acc_ref[g] = a * acc_ref[g] + _dot(vt_ref[0, g, j], p.astype(BF16))

    def unmasked(j, c):
        step(j, False)
        return c

    def diagonal(j, c):
        step(j, True)
        return c

    lax.fori_loop(0, qi * per_q, unmasked, 0)
    lax.fori_loop(qi * per_q, (qi + 1) * per_q, diagonal, 0)
    for g in range(heads):
        o_ref[0, :, V_DIM * g:V_DIM * g + V_DIM] = (acc_ref[g] / l_ref[g]).T.astype(BF16)


def _outproj_kernel(a_ref, c_ref, x_ref, w_ref, g_ref, b_ref, o32_ref, o16_ref):
    y = _dot(a_ref[...], w_ref[0:MLA_WIDTH, :]) + _dot(c_ref[...], w_ref[MLA_WIDTH:, :])
    y = y + ALPHA * x_ref[...]
    out = _layer_norm(y, g_ref[...], b_ref[...])
    o32_ref[...] = out
    o16_ref[...] = out.astype(BF16)


def _ffn_up_kernel(x_ref, wg_ref, wu_ref, cwg_ref, cwu_ref, wd_ref, o_ref, wdb_ref,
                   wgb_ref, wub_ref, gs_ref, us_ref, *, tm, tn, rc, tiles_per_seq):
    mi = pl.program_id(1)

    @pl.when(mi == 0)
    def _():
        wgb_ref[...] = wg_ref[...].astype(BF16)
        wub_ref[...] = wu_ref[...].astype(BF16)

    @pl.when(mi % tiles_per_seq == 0)
    def _():
        gs_ref[0:HALO, :] = jnp.zeros((HALO, tn), F32)
        us_ref[0:HALO, :] = jnp.zeros((HALO, tn), F32)

    wdb_ref[...] = wd_ref[...].astype(BF16)

    xb = x_ref[...]
    for c in range(tn // rc):
        cols = slice(c * rc, (c + 1) * rc)
        g = _causal_conv3(_dot(xb, wgb_ref[:, cols]), gs_ref, cwg_ref[:, cols], cols, tm)
        u = _causal_conv3(_dot(xb, wub_ref[:, cols]), us_ref, cwu_ref[:, cols], cols, tm)
        o_ref[:, cols] = (g / (1.0 + jnp.exp(-g)) * u).astype(BF16)
    gs_ref[0:HALO, :] = gs_ref[tm:tm + HALO, :]
    us_ref[0:HALO, :] = us_ref[tm:tm + HALO, :]


def _ffn_down_kernel(h_ref, w_ref, x_ref, g_ref, b_ref, o_ref):
    y = _dot(h_ref[...], w_ref[...]) + ALPHA * x_ref[...]
    o_ref[...] = _layer_norm(y, g_ref[...], b_ref[...])


def _params(sem, vmem_mib):
    return pltpu.CompilerParams(dimension_semantics=sem, vmem_limit_bytes=vmem_mib * MIB)


def _layer(x2, pos, w_in, q_norm_g, w_uq, kv_norm_g, w_ukv, conv_w, w_out, ln1_g, ln1_b,
           w_gate_up, ffn_conv_w, w_down, ln2_g, ln2_b, batch, seq):
    T = batch * seq

    half = ROPE // 2
    wk1, wk2 = w_in[:, 768:768 + half], w_in[:, 768 + half:832]
    wlat = jnp.concatenate([w_in[:, :Q_RANK + KV_RANK], wk1, wk1, wk2, wk2], axis=1).astype(BF16)
    wconv = w_in[:, 832:].astype(BF16)
    wqt = w_uq.T.astype(BF16)
    ukv = w_ukv.reshape(KV_RANK, MLA_HEADS, NOPE + V_DIM)
    wk = ukv[:, :, :NOPE].reshape(KV_RANK, MLA_HEADS * NOPE).astype(BF16)
    wvt = ukv[:, :, NOPE:].reshape(KV_RANK, MLA_HEADS * V_DIM).T.astype(BF16)
    gq = (q_norm_g * Q_SCALE).reshape(1, Q_RANK)
    gkv = kv_norm_g.reshape(1, KV_RANK)
    inv_freq = (ROPE_THETA ** (-jnp.arange(0, ROPE, 2, dtype=F32) / ROPE)).reshape(half, 1)
    posr = pos.reshape(1, T)

    tm = 512
    tk = 256
    nblk = seq // tm
    full = lambda shape: pl.BlockSpec(shape, lambda *_: (0,) * len(shape), pipeline_mode=pl.Buffered(1))
    qt, kn, kr, vt = pl.pallas_call(
        functools.partial(_latent_kernel, tm=tm, tk=tk),
        grid=(T // tm,),
        in_specs=[
            pl.BlockSpec((tm, D_MODEL), lambda i: (i, 0)),
            pl.BlockSpec((1, tm), lambda i: (0, i)),
            full((half, 1)),
            full((D_MODEL, Q_RANK + KV_RANK + 128)),
            full((1, Q_RANK)), full((1, KV_RANK)),
            full((MLA_HEADS * QK_DIM, Q_RANK)),
            full((KV_RANK, MLA_HEADS * NOPE)),
            full((MLA_HEADS * V_DIM, KV_RANK)),
        ],
        out_specs=[
            pl.BlockSpec((1, MLA_HEADS, QK_PAD, tm), lambda i: (i // nblk, 0, 0, i % nblk)),
            pl.BlockSpec((1, MLA_HEADS, tm, NOPE), lambda i: (i // nblk, 0, i % nblk, 0)),
            pl.BlockSpec((1, tm, 128), lambda i: (i // nblk, i % nblk, 0)),
            pl.BlockSpec((1, MLA_HEADS, tm // tk, V_DIM, tk), lambda i: (i // nblk, 0, i % nblk, 0, 0)),
        ],
        out_shape=[
            jax.ShapeDtypeStruct((batch, MLA_HEADS, QK_PAD, seq), BF16),
            jax.ShapeDtypeStruct((batch, MLA_HEADS, seq, NOPE), BF16),
            jax.ShapeDtypeStruct((batch, seq, 128), BF16),
            jax.ShapeDtypeStruct((batch, MLA_HEADS, seq // tk, V_DIM, tk), BF16),
        ],
        compiler_params=_params(("arbitrary",), 48),
        name="latent",
    )(x2, posr, inv_freq, wlat, gq, gkv, wqt, wk, wvt)

    tm2 = 512
    n2 = T // tm2
    wo_rows = D_MODEL // n2
    conv_out, w_out_b = pl.pallas_call(
        functools.partial(_convmix_kernel, tm=tm2, tc=256, tiles_per_seq=seq // tm2),
        grid=(n2,),
        in_specs=[
            pl.BlockSpec((tm2, D_MODEL), lambda i: (i, 0)),
            full((D_MODEL, 3 * CONV_WIDTH)),
            full((3, CONV_WIDTH)),
            pl.BlockSpec((wo_rows, D_MODEL), lambda i: (i, 0)),
        ],
        out_specs=[pl.BlockSpec((tm2, CONV_WIDTH), lambda i: (i, 0)),
                   pl.BlockSpec((wo_rows, D_MODEL), lambda i: (i, 0))],
        out_shape=[jax.ShapeDtypeStruct((T, CONV_WIDTH), BF16),
                   jax.ShapeDtypeStruct((D_MODEL, D_MODEL), BF16)],
        scratch_shapes=[pltpu.VMEM((tm2 + HALO, CONV_WIDTH), F32)],
        compiler_params=_params(("arbitrary",), 48),
        name="convmix",
    )(x2, wconv, conv_w, w_out)

    tq = 512
    attn = pl.pallas_call(
        functools.partial(_attn_kernel, tq=tq, tk=tk, heads=MLA_HEADS, lookahead=2),
        grid=(batch, seq // tq),
        in_specs=[
            pl.BlockSpec((1, MLA_HEADS, QK_PAD, tq), lambda b, i: (b, 0, 0, i)),
            pl.BlockSpec((1, MLA_HEADS, seq, NOPE), lambda b, i: (b, 0, 0, 0)),
            pl.BlockSpec((1, seq, 128), lambda b, i: (b, 0, 0)),
            pl.BlockSpec((1, MLA_HEADS, seq // tk, V_DIM, tk), lambda b, i: (b, 0, 0, 0, 0)),
        ],
        out_specs=pl.BlockSpec((1, tq, MLA_WIDTH), lambda b, i: (b, i, 0)),
        out_shape=jax.ShapeDtypeStruct((batch, seq, MLA_WIDTH), BF16),
        scratch_shapes=[pltpu.VMEM((MLA_HEADS, 1, tq), F32), pltpu.VMEM((MLA_HEADS, 1, tq), F32),
                        pltpu.VMEM((MLA_HEADS, V_DIM, tq), F32)],
        compiler_params=_params(("arbitrary", "arbitrary"), 56),
        name="attn",
    )(qt, kn, kr, vt)
    attn = attn.reshape(T, MLA_WIDTH)

    tm4 = 512
    x1, x1b = pl.pallas_call(
        _outproj_kernel,
        grid=(T // tm4,),
        in_specs=[
            pl.BlockSpec((tm4, MLA_WIDTH), lambda i: (i, 0)),
            pl.BlockSpec((tm4, CONV_WIDTH), lambda i: (i, 0)),
            pl.BlockSpec((tm4, D_MODEL), lambda i: (i, 0)),
            full((D_MODEL, D_MODEL)), full((1, D_MODEL)), full((1, D_MODEL)),
        ],
        out_specs=[pl.BlockSpec((tm4, D_MODEL), lambda i: (i, 0)), pl.BlockSpec((tm4, D_MODEL), lambda i: (i, 0))],
        out_shape=[jax.ShapeDtypeStruct((T, D_MODEL), F32), jax.ShapeDtypeStruct((T, D_MODEL), BF16)],
        compiler_params=_params(("arbitrary",), 56),
        name="outproj",
    )(attn, conv_out, x2, w_out_b, ln1_g.reshape(1, D_MODEL), ln1_b.reshape(1, D_MODEL))

    tm5, tn5 = 1024, 512
    nn5 = FFN_DIM // tn5
    nm5 = T // tm5
    wd_rows = FFN_DIM // (nn5 * nm5)
    h2, w_down_b = pl.pallas_call(
        functools.partial(_ffn_up_kernel, tm=tm5, tn=tn5, rc=tn5, tiles_per_seq=seq // tm5),
        grid=(nn5, nm5),
        in_specs=[
            pl.BlockSpec((tm5, D_MODEL), lambda n, m: (m, 0)),
            pl.BlockSpec((D_MODEL, tn5), lambda n, m: (0, n)),
            pl.BlockSpec((D_MODEL, tn5), lambda n, m: (0, n + nn5)),
            pl.BlockSpec((3, tn5), lambda n, m: (0, n)),
            pl.BlockSpec((3, tn5), lambda n, m: (0, n + nn5)),
            pl.BlockSpec((wd_rows, D_MODEL), lambda n, m: (n * nm5 + m, 0)),
        ],
        out_specs=[pl.BlockSpec((tm5, tn5), lambda n, m: (m, n)),
                   pl.BlockSpec((wd_rows, D_MODEL), lambda n, m: (n * nm5 + m, 0))],
        out_shape=[jax.ShapeDtypeStruct((T, FFN_DIM), BF16),
                   jax.ShapeDtypeStruct((FFN_DIM, D_MODEL), BF16)],
        scratch_shapes=[pltpu.VMEM((D_MODEL, tn5), BF16), pltpu.VMEM((D_MODEL, tn5), BF16),
                        pltpu.VMEM((tm5 + HALO, tn5), F32), pltpu.VMEM((tm5 + HALO, tn5), F32)],
        compiler_params=_params(("arbitrary", "arbitrary"), 56),
        name="ffn_up",
    )(x1b, w_gate_up, w_gate_up, ffn_conv_w, ffn_conv_w, w_down)

    tm6 = 256
    out = pl.pallas_call(
        _ffn_down_kernel,
        grid=(T // tm6,),
        in_specs=[
            pl.BlockSpec((tm6, FFN_DIM), lambda m: (m, 0)),
            full((FFN_DIM, D_MODEL)),
            pl.BlockSpec((tm6, D_MODEL), lambda m: (m, 0)),
            full((1, D_MODEL)), full((1, D_MODEL)),
        ],
        out_specs=pl.BlockSpec((tm6, D_MODEL), lambda m: (m, 0)),
        out_shape=jax.ShapeDtypeStruct((T, D_MODEL), F32),
        compiler_params=_params(("arbitrary",), 56),
        name="ffn_down",
    )(h2, w_down_b, x1, ln2_g.reshape(1, D_MODEL), ln2_b.reshape(1, D_MODEL))
    return out


def kernel(x, positions, w_in, q_norm_g, w_uq, kv_norm_g, w_ukv, conv_w, w_out, ln1_g, ln1_b,
           w_gate_up, ffn_conv_w, w_down, ln2_g, ln2_b):
    batch, seq, _ = x.shape
    x2 = x.reshape(batch * seq, D_MODEL)
    for l in range(DEPTH):
        x2 = _layer(x2, positions, w_in[l], q_norm_g[l], w_uq[l], kv_norm_g[l], w_ukv[l], conv_w[l],
                    w_out[l], ln1_g[l], ln1_b[l], w_gate_up[l], ffn_conv_w[l], w_down[l], ln2_g[l], ln2_b[l],
                    batch, seq)
    return x2.reshape(batch, seq, D_MODEL)
```

```python
import functools
import math

import jax
import jax.numpy as jnp
from jax import lax
from jax.experimental import pallas as pl
from jax.experimental.pallas import tpu as pltpu

D_MODEL = 2048
MLA_HEADS = 8
NOPE = 128
ROPE = 64
V_DIM = 128
MLA_WIDTH = MLA_HEADS * V_DIM
Q_RANK = 512
KV_RANK = 256
CONV_WIDTH = 1024
FFN_DIM = 5632
ROPE_THETA = 10000.0
LN_EPS = 1e-5
RMS_EPS = 1e-6
DEPTH = 1
ALPHA = (2.0 * DEPTH) ** 0.25
QK_DIM = NOPE + ROPE
Q_SCALE = (QK_DIM ** -0.5) * math.log2(math.e)
QK_PAD = 256
HALO = 8
NEG = -1e30

F32 = jnp.float32
BF16 = jnp.bfloat16
MIB = 1024 * 1024


def _dot(a, b):
    return jnp.dot(a, b, preferred_element_type=F32)


def _dot_nt(a, b):
    return lax.dot_general(a, b, (((1,), (1,)), ((), ())), preferred_element_type=F32)


def _rms(x, g):
    return x * lax.rsqrt(jnp.mean(x * x, axis=-1, keepdims=True) + RMS_EPS) * g


def _layer_norm(y, g, b):
    mu = jnp.mean(y, axis=-1, keepdims=True)
    d = y - mu
    var = jnp.mean(d * d, axis=-1, keepdims=True)
    return d * lax.rsqrt(var + LN_EPS) * g + b


def _causal_conv3(z, zs_ref, cw, cols, n, row0=0):
    zs_ref[HALO + row0:HALO + row0 + n, cols] = z
    z1 = zs_ref[HALO - 1 + row0:HALO - 1 + row0 + n, cols]
    z2 = zs_ref[HALO - 2 + row0:HALO - 2 + row0 + n, cols]
    return cw[2:3, :] * z + cw[1:2, :] * z1 + cw[0:1, :] * z2


def _latent_kernel(x_ref, posr_ref, invf_ref, win_ref, gq_ref, gkv_ref, wqt_ref, wk_ref, wvt_ref,
                   qt_ref, kn_ref, kr_ref, vt_ref, wlat_ref, *, tm, tk):
    half = ROPE // 2
    latent = Q_RANK + KV_RANK

    @pl.when(pl.program_id(0) == 0)
    def _():
        wlat_ref[...] = win_ref[...].astype(BF16)

    xb = x_ref[...].astype(BF16)
    h = _dot(xb, wlat_ref[...])
    cqn = _rms(h[:, :Q_RANK], gq_ref[...]).astype(BF16)
    ckvn = _rms(h[:, Q_RANK:latent], gkv_ref[...]).astype(BF16)

    ang_t = invf_ref[...] * posr_ref[...].astype(F32)
    cos_t, sin_t = jnp.cos(ang_t), jnp.sin(ang_t)
    table = jnp.concatenate([cos_t, cos_t, sin_t, sin_t], axis=0).T
    k_r = h[:, latent:latent + ROPE]
    kr_ref[0] = (jnp.concatenate([k_r, k_r], axis=1) * table).astype(BF16)

    kn = _dot(ckvn, wk_ref[...])
    vt = _dot_nt(wvt_ref[...], ckvn)
    qt = _dot_nt(wqt_ref[...], cqn)
    for hh in range(MLA_HEADS):
        kn_ref[0, hh] = kn[:, NOPE * hh:NOPE * hh + NOPE].astype(BF16)
        for c in range(tm // tk):
            vt_ref[0, hh, c] = vt[V_DIM * hh:V_DIM * hh + V_DIM, tk * c:tk * c + tk].astype(BF16)
        base = QK_DIM * hh
        r1 = qt[base + NOPE:base + NOPE + half]
        r2 = qt[base + NOPE + half:base + QK_DIM]
        qr1 = (r1 * cos_t - r2 * sin_t).astype(BF16)
        qr2 = (r2 * cos_t + r1 * sin_t).astype(BF16)
        qt_ref[0, hh, 0:NOPE, :] = qt[base:base + NOPE].astype(BF16)
        qt_ref[0, hh, NOPE:NOPE + half, :] = qr1
        qt_ref[0, hh, NOPE + half:NOPE + 2 * half, :] = qr2
        qt_ref[0, hh, NOPE + 2 * half:NOPE + 3 * half, :] = qr2
        qt_ref[0, hh, NOPE + 3 * half:QK_PAD, :] = -qr1


def _convmix_kernel(x_ref, w_ref, cw_ref, wo_ref, o_ref, wob_ref, zs_ref, *, tm, tc, tiles_per_seq):
    @pl.when(pl.program_id(0) % tiles_per_seq == 0)
    def _():
        zs_ref[0:HALO, :] = jnp.zeros((HALO, CONV_WIDTH), F32)

    wob_ref[...] = wo_ref[...].astype(BF16)

    xb = x_ref[...].astype(BF16)
    for c in range(CONV_WIDTH // tc):
        cols = slice(c * tc, (c + 1) * tc)
        gate_b = _dot(xb, w_ref[:, c * tc:(c + 1) * tc])
        gate_c = _dot(xb, w_ref[:, CONV_WIDTH + c * tc:CONV_WIDTH + (c + 1) * tc])
        x_c = _dot(xb, w_ref[:, 2 * CONV_WIDTH + c * tc:2 * CONV_WIDTH + (c + 1) * tc])
        y = _causal_conv3(gate_c * x_c, zs_ref, cw_ref[:, cols], cols, tm)
        o_ref[:, cols] = (gate_b * y).astype(BF16)
    zs_ref[0:HALO, :] = zs_ref[tm:tm + HALO, :]


def _attn_kernel(qt_ref, kn_ref, kr_ref, vt_ref, o_ref, m_ref, l_ref, acc_ref, *, tq, tk, heads, lookahead):
    qi = pl.program_id(1)
    per_q = tq // tk
    m_ref[...] = jnp.full(m_ref.shape, NEG, F32)
    l_ref[...] = jnp.zeros(l_ref.shape, F32)
    acc_ref[...] = jnp.zeros(acc_ref.shape, F32)

    def run(chains):
        def scores(chain):
            g, j, c0, _ = chain
            start = pl.multiple_of(j * tk, tk)
            k = jnp.concatenate([kn_ref[0, g, pl.ds(start, tk), :], kr_ref[0, pl.ds(start, tk), :]], axis=1)
            return _dot(k, qt_ref[0, g, :, c0:])

        ahead = [scores(ch) for ch in chains[:lookahead]]
        for i, (g, j, c0, mask) in enumerate(chains):
            s = ahead.pop(0)
            if i + lookahead < len(chains):
                ahead.append(scores(chains[i + lookahead]))
            if mask is not None:
                s = jnp.where(mask, s, NEG)
            m = m_ref[g, :, c0:]
            m_new = jnp.maximum(m, jnp.max(s, axis=0, keepdims=True))
            p = jnp.exp2(s - m_new)
            a = jnp.exp2(m - m_new)
            m_ref[g, :, c0:] = m_new
            l_ref[g, :, c0:] = a * l_ref[g, :, c0:] + jnp.sum(p, axis=0, keepdims=True)
            acc_ref[g, :, c0:] = a * acc_ref[g, :, c0:] + _dot(vt_ref[0, g, j], p.astype(BF16))

    def below_diagonal(jj, c):
        run([(g, per_q * jj + b, 0, None) for b in range(per_q) for g in range(heads)])
        return c

    lax.fori_loop(0, qi, below_diagonal, 0)

    diag = []
    for b in range(per_q):
        c0 = b * tk
        key = lax.broadcasted_iota(jnp.int32, (tk, tq - c0), 0)
        qry = lax.broadcasted_iota(jnp.int32, (tk, tq - c0), 1)
        diag += [(g, per_q * qi + b, c0, key <= qry) for g in range(heads)]
    run(diag)
    for g in range(heads):
        o_ref[0, :, V_DIM * g:V_DIM * g + V_DIM] = (acc_ref[g] / l_ref[g]).T.astype(BF16)


def _outproj_kernel(a_ref, c_ref, x_ref, w_ref, g_ref, b_ref, o32_ref, o16_ref, *, tm, rc):
    for r in range(tm // rc):
        rows = slice(r * rc, (r + 1) * rc)
        y = _dot(a_ref[rows, :], w_ref[0:MLA_WIDTH, :]) + _dot(c_ref[rows, :], w_ref[MLA_WIDTH:, :])
        y = y + ALPHA * x_ref[rows, :]
        out = _layer_norm(y, g_ref[...], b_ref[...])
        o32_ref[rows, :] = out
        o16_ref[rows, :] = out.astype(BF16)


def _ffn_up_kernel(x_ref, wg_ref, wu_ref, cwg_ref, cwu_ref, wd_ref, o_ref, wdb_ref,
                   wgb_ref, wub_ref, gs_ref, us_ref, *, tm, tn, tiles_per_seq):
    mi = pl.program_id(1)

    @pl.when(mi == 0)
    def _():
        wgb_ref[...] = wg_ref[...].astype(BF16)
        wub_ref[...] = wu_ref[...].astype(BF16)

    @pl.when(mi % tiles_per_seq == 0)
    def _():
        gs_ref[0:HALO, :] = jnp.zeros((HALO, tn), F32)
        us_ref[0:HALO, :] = jnp.zeros((HALO, tn), F32)

    wdb_ref[...] = wd_ref[...].astype(BF16)

    xb = x_ref[...]
    cols = slice(0, tn)
    g = _causal_conv3(_dot(xb, wgb_ref[...]), gs_ref, cwg_ref[...], cols, tm)
    u = _causal_conv3(_dot(xb, wub_ref[...]), us_ref, cwu_ref[...], cols, tm)
    o_ref[...] = (g / (1.0 + jnp.exp(-g)) * u).astype(BF16)
    gs_ref[0:HALO, :] = gs_ref[tm:tm + HALO, :]
    us_ref[0:HALO, :] = us_ref[tm:tm + HALO, :]


def _ffn_down_kernel(h_ref, w_ref, x_ref, g_ref, b_ref, o_ref):
    y = _dot(h_ref[...], w_ref[...]) + ALPHA * x_ref[...]
    o_ref[...] = _layer_norm(y, g_ref[...], b_ref[...])


def _params(sem, vmem_mib, flags=None):
    return pltpu.CompilerParams(dimension_semantics=sem, vmem_limit_bytes=vmem_mib * MIB, flags=flags)


def _layer(x2, pos, w_in, q_norm_g, w_uq, kv_norm_g, w_ukv, conv_w, w_out, ln1_g, ln1_b,
           w_gate_up, ffn_conv_w, w_down, ln2_g, ln2_b, batch, seq):
    T = batch * seq

    half = ROPE // 2
    wconv = w_in[:, 832:].astype(BF16)
    wqt = w_uq.T.astype(BF16)
    ukv = w_ukv.reshape(KV_RANK, MLA_HEADS, NOPE + V_DIM)
    wk = ukv[:, :, :NOPE].reshape(KV_RANK, MLA_HEADS * NOPE).astype(BF16)
    wvt = ukv[:, :, NOPE:].reshape(KV_RANK, MLA_HEADS * V_DIM).T.astype(BF16)
    gq = (q_norm_g * Q_SCALE).reshape(1, Q_RANK)
    gkv = kv_norm_g.reshape(1, KV_RANK)
    inv_freq = (ROPE_THETA ** (-jnp.arange(0, ROPE, 2, dtype=F32) / ROPE)).reshape(half, 1)
    posr = pos.reshape(1, T)

    tm = 512
    tk = 256
    nblk = seq // tm
    full = lambda shape: pl.BlockSpec(shape, lambda *_: (0,) * len(shape), pipeline_mode=pl.Buffered(1))
    qt, kn, kr, vt = pl.pallas_call(
        functools.partial(_latent_kernel, tm=tm, tk=tk),
        grid=(T // tm,),
        in_specs=[
            pl.BlockSpec((tm, D_MODEL), lambda i: (i, 0)),
            pl.BlockSpec((1, tm), lambda i: (0, i)),
            full((half, 1)),
            full((D_MODEL, Q_RANK + KV_RANK + 128)),
            full((1, Q_RANK)), full((1, KV_RANK)),
            full((MLA_HEADS * QK_DIM, Q_RANK)),
            full((KV_RANK, MLA_HEADS * NOPE)),
            full((MLA_HEADS * V_DIM, KV_RANK)),
        ],
        out_specs=[
            pl.BlockSpec((1, MLA_HEADS, QK_PAD, tm), lambda i: (i // nblk, 0, 0, i % nblk)),
            pl.BlockSpec((1, MLA_HEADS, tm, NOPE), lambda i: (i // nblk, 0, i % nblk, 0)),
            pl.BlockSpec((1, tm, 128), lambda i: (i // nblk, i % nblk, 0)),
            pl.BlockSpec((1, MLA_HEADS, tm // tk, V_DIM, tk), lambda i: (i // nblk, 0, i % nblk, 0, 0)),
        ],
        out_shape=[
            jax.ShapeDtypeStruct((batch, MLA_HEADS, QK_PAD, seq), BF16),
            jax.ShapeDtypeStruct((batch, MLA_HEADS, seq, NOPE), BF16),
            jax.ShapeDtypeStruct((batch, seq, 128), BF16),
            jax.ShapeDtypeStruct((batch, MLA_HEADS, seq // tk, V_DIM, tk), BF16),
        ],
        scratch_shapes=[pltpu.VMEM((D_MODEL, Q_RANK + KV_RANK + 128), BF16)],
        compiler_params=_params(("arbitrary",), 48),
        name="latent",
    )(x2, posr, inv_freq, w_in, gq, gkv, wqt, wk, wvt)

    tm2 = 512
    n2 = T // tm2
    wo_rows = D_MODEL // n2
    conv_out, w_out_b = pl.pallas_call(
        functools.partial(_convmix_kernel, tm=tm2, tc=256, tiles_per_seq=seq // tm2),
        grid=(n2,),
        in_specs=[
            pl.BlockSpec((tm2, D_MODEL), lambda i: (i, 0)),
            full((D_MODEL, 3 * CONV_WIDTH)),
            full((3, CONV_WIDTH)),
            pl.BlockSpec((wo_rows, D_MODEL), lambda i: (i, 0)),
        ],
        out_specs=[pl.BlockSpec((tm2, CONV_WIDTH), lambda i: (i, 0)),
                   pl.BlockSpec((wo_rows, D_MODEL), lambda i: (i, 0))],
        out_shape=[jax.ShapeDtypeStruct((T, CONV_WIDTH), BF16),
                   jax.ShapeDtypeStruct((D_MODEL, D_MODEL), BF16)],
        scratch_shapes=[pltpu.VMEM((tm2 + HALO, CONV_WIDTH), F32)],
        compiler_params=_params(("arbitrary",), 48),
        name="convmix",
    )(x2, wconv, conv_w, w_out)

    tq = 512
    attn = pl.pallas_call(
        functools.partial(_attn_kernel, tq=tq, tk=tk, heads=MLA_HEADS, lookahead=2),
        grid=(batch, seq // tq),
        in_specs=[
            pl.BlockSpec((1, MLA_HEADS, QK_PAD, tq), lambda b, i: (b, 0, 0, i)),
            pl.BlockSpec((1, MLA_HEADS, seq, NOPE), lambda b, i: (b, 0, 0, 0)),
            pl.BlockSpec((1, seq, 128), lambda b, i: (b, 0, 0)),
            pl.BlockSpec((1, MLA_HEADS, seq // tk, V_DIM, tk), lambda b, i: (b, 0, 0, 0, 0)),
        ],
        out_specs=pl.BlockSpec((1, tq, MLA_WIDTH), lambda b, i: (b, i, 0)),
        out_shape=jax.ShapeDtypeStruct((batch, seq, MLA_WIDTH), BF16),
        scratch_shapes=[pltpu.VMEM((MLA_HEADS, 1, tq), F32), pltpu.VMEM((MLA_HEADS, 1, tq), F32),
                        pltpu.VMEM((MLA_HEADS, V_DIM, tq), F32)],
        compiler_params=_params(("arbitrary", "arbitrary"), 56),
        name="attn",
    )(qt, kn, kr, vt)
    attn = attn.reshape(T, MLA_WIDTH)

    tm4 = 512
    x1, x1b = pl.pallas_call(
        functools.partial(_outproj_kernel, tm=tm4, rc=128),
        grid=(T // tm4,),
        in_specs=[
            pl.BlockSpec((tm4, MLA_WIDTH), lambda i: (i, 0)),
            pl.BlockSpec((tm4, CONV_WIDTH), lambda i: (i, 0)),
            pl.BlockSpec((tm4, D_MODEL), lambda i: (i, 0)),
            full((D_MODEL, D_MODEL)), full((1, D_MODEL)), full((1, D_MODEL)),
        ],
        out_specs=[pl.BlockSpec((tm4, D_MODEL), lambda i: (i, 0)), pl.BlockSpec((tm4, D_MODEL), lambda i: (i, 0))],
        out_shape=[jax.ShapeDtypeStruct((T, D_MODEL), F32), jax.ShapeDtypeStruct((T, D_MODEL), BF16)],
        compiler_params=_params(("arbitrary",), 56),
        name="outproj",
    )(attn, conv_out, x2, w_out_b, ln1_g.reshape(1, D_MODEL), ln1_b.reshape(1, D_MODEL))

    tm5, tn5 = 1024, 512
    nn5 = FFN_DIM // tn5
    nm5 = T // tm5
    wd_rows = FFN_DIM // (nn5 * nm5)
    h2, w_down_b = pl.pallas_call(
        functools.partial(_ffn_up_kernel, tm=tm5, tn=tn5, tiles_per_seq=seq // tm5),
        grid=(nn5, nm5),
        in_specs=[
            pl.BlockSpec((tm5, D_MODEL), lambda n, m: (m, 0)),
            pl.BlockSpec((D_MODEL, tn5), lambda n, m: (0, n)),
            pl.BlockSpec((D_MODEL, tn5), lambda n, m: (0, n + nn5)),
            pl.BlockSpec((3, tn5), lambda n, m: (0, n)),
            pl.BlockSpec((3, tn5), lambda n, m: (0, n + nn5)),
            pl.BlockSpec((wd_rows, D_MODEL), lambda n, m: (n * nm5 + m, 0)),
        ],
        out_specs=[pl.BlockSpec((tm5, tn5), lambda n, m: (m, n)),
                   pl.BlockSpec((wd_rows, D_MODEL), lambda n, m: (n * nm5 + m, 0))],
        out_shape=[jax.ShapeDtypeStruct((T, FFN_DIM), BF16),
                   jax.ShapeDtypeStruct((FFN_DIM, D_MODEL), BF16)],
        scratch_shapes=[pltpu.VMEM((D_MODEL, tn5), BF16), pltpu.VMEM((D_MODEL, tn5), BF16),
                        pltpu.VMEM((tm5 + HALO, tn5), F32), pltpu.VMEM((tm5 + HALO, tn5), F32)],
        compiler_params=_params(("arbitrary", "arbitrary"), 56),
        name="ffn_up",
    )(x1b, w_gate_up, w_gate_up, ffn_conv_w, ffn_conv_w, w_down)

    tm6 = 256
    out = pl.pallas_call(
        _ffn_down_kernel,
        grid=(T // tm6,),
        in_specs=[
            pl.BlockSpec((tm6, FFN_DIM), lambda m: (m, 0)),
            full((FFN_DIM, D_MODEL)),
            pl.BlockSpec((tm6, D_MODEL), lambda m: (m, 0)),
            full((1, D_MODEL)), full((1, D_MODEL)),
        ],
        out_specs=pl.BlockSpec((tm6, D_MODEL), lambda m: (m, 0)),
        out_shape=jax.ShapeDtypeStruct((T, D_MODEL), F32),
        compiler_params=_params(("arbitrary",), 56),
        name="ffn_down",
    )(h2, w_down_b, x1, ln2_g.reshape(1, D_MODEL), ln2_b.reshape(1, D_MODEL))
    return out


def kernel(x, positions, w_in, q_norm_g, w_uq, kv_norm_g, w_ukv, conv_w, w_out, ln1_g, ln1_b,
           w_gate_up, ffn_conv_w, w_down, ln2_g, ln2_b):
    batch, seq, _ = x.shape
    x2 = x.reshape(batch * seq, D_MODEL)
    for l in range(DEPTH):
        x2 = _layer(x2, positions, w_in[l], q_norm_g[l], w_uq[l], kv_norm_g[l], w_ukv[l], conv_w[l],
                    w_out[l], ln1_g[l], ln1_b[l], w_gate_up[l], ffn_conv_w[l], w_down[l], ln2_g[l], ln2_b[l],
                    batch, seq)
    return x2.reshape(batch, seq, D_MODEL)
```

```python
import functools
import math

import jax
import jax.numpy as jnp
from jax import lax
from jax.experimental import pallas as pl
from jax.experimental.pallas import tpu as pltpu

D_MODEL = 2048
MLA_HEADS = 8
NOPE = 128
ROPE = 64
V_DIM = 128
V_ROWS = V_DIM + 16
MLA_WIDTH = MLA_HEADS * V_DIM
Q_RANK = 512
KV_RANK = 256
CONV_WIDTH = 1024
FFN_DIM = 5632
ROPE_THETA = 10000.0
LN_EPS = 1e-5
RMS_EPS = 1e-6
DEPTH = 1
ALPHA = (2.0 * DEPTH) ** 0.25
QK_DIM = NOPE + ROPE
Q_SCALE = (QK_DIM ** -0.5) * math.log2(math.e)
QK_PAD = 256
HALO = 8
NEG = -1e30

F32 = jnp.float32
BF16 = jnp.bfloat16
MIB = 1024 * 1024


def _dot(a, b):
    return jnp.dot(a, b, preferred_element_type=F32)


def _dot_nt(a, b):
    return lax.dot_general(a, b, (((1,), (1,)), ((), ())), preferred_element_type=F32)


def _rms(x, g):
    return x * lax.rsqrt(jnp.mean(x * x, axis=-1, keepdims=True) + RMS_EPS) * g


def _layer_norm(y, g, b):
    mu = jnp.mean(y, axis=-1, keepdims=True)
    d = y - mu
    var = jnp.mean(d * d, axis=-1, keepdims=True)
    return d * lax.rsqrt(var + LN_EPS) * g + b


def _causal_conv3(z, zs_ref, cw, cols, n, row0=0):
    zs_ref[HALO + row0:HALO + row0 + n, cols] = z
    z1 = zs_ref[HALO - 1 + row0:HALO - 1 + row0 + n, cols]
    z2 = zs_ref[HALO - 2 + row0:HALO - 2 + row0 + n, cols]
    return cw[2:3, :] * z + cw[1:2, :] * z1 + cw[0:1, :] * z2


def _latent_kernel(x_ref, posr_ref, invf_ref, win_ref, wrows_ref, gq_ref, gkv_ref, wqt_ref, wk_ref, wvt_ref,
                   qt_ref, kn_ref, kr_ref, vt_ref, wconv_ref, wlat_ref, *, tm, tk):
    half = ROPE // 2
    latent = Q_RANK + KV_RANK

    @pl.when(pl.program_id(0) == 0)
    def _():
        wlat_ref[...] = win_ref[...].astype(BF16)

    wconv_ref[...] = wrows_ref[:, latent + ROPE:].astype(BF16)

    xb = x_ref[...].astype(BF16)
    h = _dot(xb, wlat_ref[...])
    cqn = _rms(h[:, :Q_RANK], gq_ref[...]).astype(BF16)
    ckvn = _rms(h[:, Q_RANK:latent], gkv_ref[...]).astype(BF16)

    ang_t = invf_ref[...] * posr_ref[...].astype(F32)
    cos_t, sin_t = jnp.cos(ang_t), jnp.sin(ang_t)
    table = jnp.concatenate([cos_t, cos_t, sin_t, sin_t], axis=0).T
    k_r = h[:, latent:latent + ROPE]
    kr_ref[0] = (jnp.concatenate([k_r, k_r], axis=1) * table).astype(BF16)

    kn = _dot(ckvn, wk_ref[...])
    vt = _dot_nt(wvt_ref[...], ckvn)
    qt = _dot_nt(wqt_ref[...], cqn)
    ones_row = (lax.broadcasted_iota(jnp.int32, (V_ROWS - V_DIM, tk), 0) == 0).astype(BF16)
    for hh in range(MLA_HEADS):
        kn_ref[0, hh] = kn[:, NOPE * hh:NOPE * hh + NOPE].astype(BF16)
        for c in range(tm // tk):
            vt_ref[0, hh, c, 0:V_DIM, :] = vt[V_DIM * hh:V_DIM * hh + V_DIM, tk * c:tk * c + tk].astype(BF16)
            vt_ref[0, hh, c, V_DIM:V_ROWS, :] = ones_row
        base = QK_DIM * hh
        r1 = qt[base + NOPE:base + NOPE + half]
        r2 = qt[base + NOPE + half:base + QK_DIM]
        qr1 = (r1 * cos_t - r2 * sin_t).astype(BF16)
        qr2 = (r2 * cos_t + r1 * sin_t).astype(BF16)
        qt_ref[0, hh, 0:NOPE, :] = qt[base:base + NOPE].astype(BF16)
        qt_ref[0, hh, NOPE:NOPE + half, :] = qr1
        qt_ref[0, hh, NOPE + half:NOPE + 2 * half, :] = qr2
        qt_ref[0, hh, NOPE + 2 * half:NOPE + 3 * half, :] = qr2
        qt_ref[0, hh, NOPE + 3 * half:QK_PAD, :] = -qr1


def _convmix_kernel(x_ref, w_ref, cw_ref, wo_ref, o_ref, wob_ref, zs_ref, *, tm, tc, tiles_per_seq):
    @pl.when(pl.program_id(0) % tiles_per_seq == 0)
    def _():
        zs_ref[0:HALO, :] = jnp.zeros((HALO, CONV_WIDTH), F32)

    wob_ref[...] = wo_ref[...].astype(BF16)

    xb = x_ref[...].astype(BF16)
    for c in range(CONV_WIDTH // tc):
        cols = slice(c * tc, (c + 1) * tc)
        gate_b = _dot(xb, w_ref[:, c * tc:(c + 1) * tc])
        gate_c = _dot(xb, w_ref[:, CONV_WIDTH + c * tc:CONV_WIDTH + (c + 1) * tc])
        x_c = _dot(xb, w_ref[:, 2 * CONV_WIDTH + c * tc:2 * CONV_WIDTH + (c + 1) * tc])
        y = _causal_conv3(gate_c * x_c, zs_ref, cw_ref[:, cols], cols, tm)
        o_ref[:, cols] = (gate_b * y).astype(BF16)
    zs_ref[0:HALO, :] = zs_ref[tm:tm + HALO, :]


def _attn_kernel(qt_ref, kn_ref, kr_ref, vt_ref, o_ref, m_ref, acc_ref, *, tq, tk, heads, lookahead):
    qi = pl.program_id(1)
    per_q = tq // tk
    trip_blocks = 2 * per_q
    m_ref[...] = jnp.full(m_ref.shape, NEG, F32)
    acc_ref[...] = jnp.zeros(acc_ref.shape, F32)

    def run(chains):
        def scores(chain):
            g, j, c0, _ = chain
            start = pl.multiple_of(j * tk, tk)
            k = jnp.concatenate([kn_ref[0, g, pl.ds(start, tk), :], kr_ref[0, pl.ds(start, tk), :]], axis=1)
            return _dot(k, qt_ref[0, g, :, c0:])

        ahead = [scores(ch) for ch in chains[:lookahead]]
        for i, (g, j, c0, mask) in enumerate(chains):
            s = ahead.pop(0)
            if i + lookahead < len(chains):
                ahead.append(scores(chains[i + lookahead]))
            if mask is not None:
                s = jnp.where(mask, s, NEG)
            m = m_ref[g, :, c0:]
            m_new = jnp.maximum(m, jnp.max(s, axis=0, keepdims=True))
            p = jnp.exp2(s - m_new)
            a = jnp.exp2(m - m_new)
            m_ref[g, :, c0:] = m_new
            acc_ref[g, :, c0:] = a * acc_ref[g, :, c0:] + _dot(vt_ref[0, g, j], p.astype(BF16))

    def full_blocks(first, count):
        return [(g, first + b, 0, None) for b in range(count) for g in range(heads)]

    def below_diagonal(t, c):
        run(full_blocks(trip_blocks * t, trip_blocks))
        return c

    n_below = qi * per_q
    lax.fori_loop(0, n_below // trip_blocks, below_diagonal, 0)

    @pl.when(n_below % trip_blocks != 0)
    def _():
        run(full_blocks(n_below - per_q, per_q))

    diag = []
    for b in range(per_q):
        c0 = b * tk
        key = lax.broadcasted_iota(jnp.int32, (tk, tq - c0), 0)
        qry = lax.broadcasted_iota(jnp.int32, (tk, tq - c0), 1)
        diag += [(g, per_q * qi + b, c0, key <= qry) for g in range(heads)]
    run(diag)
    for g in range(heads):
        out_t = acc_ref[g, 0:V_DIM, :] / acc_ref[g, V_DIM:V_DIM + 1, :]
        o_ref[0, :, V_DIM * g:V_DIM * g + V_DIM] = out_t.T.astype(BF16)


def _outproj_kernel(a_ref, c_ref, x_ref, w_ref, g_ref, b_ref, o32_ref, o16_ref, *, tm, rc):
    for r in range(tm // rc):
        rows = slice(r * rc, (r + 1) * rc)
        y = _dot(a_ref[rows, :], w_ref[0:MLA_WIDTH, :]) + _dot(c_ref[rows, :], w_ref[MLA_WIDTH:, :])
        y = y + ALPHA * x_ref[rows, :]
        out = _layer_norm(y, g_ref[...], b_ref[...])
        o32_ref[rows, :] = out
        o16_ref[rows, :] = out.astype(BF16)


def _ffn_up_kernel(x_ref, wg_ref, wu_ref, cwg_ref, cwu_ref, wd_ref, o_ref, wdb_ref,
                   wgb_ref, wub_ref, gs_ref, us_ref, *, tm, tn, tiles_per_seq):
    mi = pl.program_id(1)

    @pl.when(mi == 0)
    def _():
        wgb_ref[...] = wg_ref[...].astype(BF16)
        wub_ref[...] = wu_ref[...].astype(BF16)

    @pl.when(mi % tiles_per_seq == 0)
    def _():
        gs_ref[0:HALO, :] = jnp.zeros((HALO, tn), F32)
        us_ref[0:HALO, :] = jnp.zeros((HALO, tn), F32)

    wdb_ref[...] = wd_ref[...].astype(BF16)

    xb = x_ref[...]
    cols = slice(0, tn)
    g = _causal_conv3(_dot(xb, wgb_ref[...]), gs_ref, cwg_ref[...], cols, tm)
    u = _causal_conv3(_dot(xb, wub_ref[...]), us_ref, cwu_ref[...], cols, tm)
    o_ref[...] = (g / (1.0 + jnp.exp(-g)) * u).astype(BF16)
    gs_ref[0:HALO, :] = gs_ref[tm:tm + HALO, :]
    us_ref[0:HALO, :] = us_ref[tm:tm + HALO, :]


def _ffn_down_kernel(h_ref, w_ref, x_ref, g_ref, b_ref, o_ref):
    y = _dot(h_ref[...], w_ref[...]) + ALPHA * x_ref[...]
    o_ref[...] = _layer_norm(y, g_ref[...], b_ref[...])


def _params(sem, vmem_mib, flags=None):
    return pltpu.CompilerParams(dimension_semantics=sem, vmem_limit_bytes=vmem_mib * MIB, flags=flags)


def _layer(x2, pos, w_in, q_norm_g, w_uq, kv_norm_g, w_ukv, conv_w, w_out, ln1_g, ln1_b,
           w_gate_up, ffn_conv_w, w_down, ln2_g, ln2_b, batch, seq):
    T = batch * seq

    half = ROPE // 2
    wqt = w_uq.T.astype(BF16)
    ukv = w_ukv.reshape(KV_RANK, MLA_HEADS, NOPE + V_DIM)
    wk = ukv[:, :, :NOPE].reshape(KV_RANK, MLA_HEADS * NOPE).astype(BF16)
    wvt = ukv[:, :, NOPE:].reshape(KV_RANK, MLA_HEADS * V_DIM).T.astype(BF16)
    gq = (q_norm_g * Q_SCALE).reshape(1, Q_RANK)
    gkv = kv_norm_g.reshape(1, KV_RANK)
    inv_freq = (ROPE_THETA ** (-jnp.arange(0, ROPE, 2, dtype=F32) / ROPE)).reshape(half, 1)
    posr = pos.reshape(1, T)

    tm = 512
    tk = 256
    nblk = seq // tm
    full = lambda shape: pl.BlockSpec(shape, lambda *_: (0,) * len(shape), pipeline_mode=pl.Buffered(1))
    n1 = T // tm
    in_dim = w_in.shape[1]
    wrows = D_MODEL // n1
    qt, kn, kr, vt, wconv = pl.pallas_call(
        functools.partial(_latent_kernel, tm=tm, tk=tk),
        grid=(n1,),
        in_specs=[
            pl.BlockSpec((tm, D_MODEL), lambda i: (i, 0)),
            pl.BlockSpec((1, tm), lambda i: (0, i)),
            full((half, 1)),
            full((D_MODEL, Q_RANK + KV_RANK + 128)),
            pl.BlockSpec((wrows, in_dim), lambda i: (i, 0)),
            full((1, Q_RANK)), full((1, KV_RANK)),
            full((MLA_HEADS * QK_DIM, Q_RANK)),
            full((KV_RANK, MLA_HEADS * NOPE)),
            full((MLA_HEADS * V_DIM, KV_RANK)),
        ],
        out_specs=[
            pl.BlockSpec((1, MLA_HEADS, QK_PAD, tm), lambda i: (i // nblk, 0, 0, i % nblk)),
            pl.BlockSpec((1, MLA_HEADS, tm, NOPE), lambda i: (i // nblk, 0, i % nblk, 0)),
            pl.BlockSpec((1, tm, 128), lambda i: (i // nblk, i % nblk, 0)),
            pl.BlockSpec((1, MLA_HEADS, tm // tk, V_ROWS, tk), lambda i: (i // nblk, 0, i % nblk, 0, 0)),
            pl.BlockSpec((wrows, 3 * CONV_WIDTH), lambda i: (i, 0)),
        ],
        out_shape=[
            jax.ShapeDtypeStruct((batch, MLA_HEADS, QK_PAD, seq), BF16),
            jax.ShapeDtypeStruct((batch, MLA_HEADS, seq, NOPE), BF16),
            jax.ShapeDtypeStruct((batch, seq, 128), BF16),
            jax.ShapeDtypeStruct((batch, MLA_HEADS, seq // tk, V_ROWS, tk), BF16),
            jax.ShapeDtypeStruct((D_MODEL, 3 * CONV_WIDTH), BF16),
        ],
        scratch_shapes=[pltpu.VMEM((D_MODEL, Q_RANK + KV_RANK + 128), BF16)],
        compiler_params=_params(("arbitrary",), 48),
        name="latent",
    )(x2, posr, inv_freq, w_in, w_in, gq, gkv, wqt, wk, wvt)

    tm2 = 512
    n2 = T // tm2
    wo_rows = D_MODEL // n2
    conv_out, w_out_b = pl.pallas_call(
        functools.partial(_convmix_kernel, tm=tm2, tc=256, tiles_per_seq=seq // tm2),
        grid=(n2,),
        in_specs=[
            pl.BlockSpec((tm2, D_MODEL), lambda i: (i, 0)),
            full((D_MODEL, 3 * CONV_WIDTH)),
            full((3, CONV_WIDTH)),
            pl.BlockSpec((wo_rows, D_MODEL), lambda i: (i, 0)),
        ],
        out_specs=[pl.BlockSpec((tm2, CONV_WIDTH), lambda i: (i, 0)),
                   pl.BlockSpec((wo_rows, D_MODEL), lambda i: (i, 0))],
        out_shape=[jax.ShapeDtypeStruct((T, CONV_WIDTH), BF16),
                   jax.ShapeDtypeStruct((D_MODEL, D_MODEL), BF16)],
        scratch_shapes=[pltpu.VMEM((tm2 + HALO, CONV_WIDTH), F32)],
        compiler_params=_params(("arbitrary",), 48),
        name="convmix",
    )(x2, wconv, conv_w, w_out)

    tq = 512
    attn = pl.pallas_call(
        functools.partial(_attn_kernel, tq=tq, tk=tk, heads=MLA_HEADS, lookahead=2),
        grid=(batch, seq // tq),
        in_specs=[
            pl.BlockSpec((1, MLA_HEADS, QK_PAD, tq), lambda b, i: (b, 0, 0, i)),
            pl.BlockSpec((1, MLA_HEADS, seq, NOPE), lambda b, i: (b, 0, 0, 0)),
            pl.BlockSpec((1, seq, 128), lambda b, i: (b, 0, 0)),
            pl.BlockSpec((1, MLA_HEADS, seq // tk, V_ROWS, tk), lambda b, i: (b, 0, 0, 0, 0)),
        ],
        out_specs=pl.BlockSpec((1, tq, MLA_WIDTH), lambda b, i: (b, i, 0)),
        out_shape=jax.ShapeDtypeStruct((batch, seq, MLA_WIDTH), BF16),
        scratch_shapes=[pltpu.VMEM((MLA_HEADS, 1, tq), F32), pltpu.VMEM((MLA_HEADS, V_ROWS, tq), F32)],
        compiler_params=_params(("arbitrary", "arbitrary"), 56),
        name="attn",
    )(qt, kn, kr, vt)
    attn = attn.reshape(T, MLA_WIDTH)

    tm4 = 512
    x1, x1b = pl.pallas_call(
        functools.partial(_outproj_kernel, tm=tm4, rc=128),
        grid=(T // tm4,),
        in_specs=[
            pl.BlockSpec((tm4, MLA_WIDTH), lambda i: (i, 0)),
            pl.BlockSpec((tm4, CONV_WIDTH), lambda i: (i, 0)),
            pl.BlockSpec((tm4, D_MODEL), lambda i: (i, 0)),
            full((D_MODEL, D_MODEL)), full((1, D_MODEL)), full((1, D_MODEL)),
        ],
        out_specs=[pl.BlockSpec((tm4, D_MODEL), lambda i: (i, 0)), pl.BlockSpec((tm4, D_MODEL), lambda i: (i, 0))],
        out_shape=[jax.ShapeDtypeStruct((T, D_MODEL), F32), jax.ShapeDtypeStruct((T, D_MODEL), BF16)],
        compiler_params=_params(("arbitrary",), 56),
        name="outproj",
    )(attn, conv_out, x2, w_out_b, ln1_g.reshape(1, D_MODEL), ln1_b.reshape(1, D_MODEL))

    tm5, tn5 = 1024, 512
    nn5 = FFN_DIM // tn5
    nm5 = T // tm5
    wd_rows = FFN_DIM // (nn5 * nm5)
    h2, w_down_b = pl.pallas_call(
        functools.partial(_ffn_up_kernel, tm=tm5, tn=tn5, tiles_per_seq=seq // tm5),
        grid=(nn5, nm5),
        in_specs=[
            pl.BlockSpec((tm5, D_MODEL), lambda n, m: (m, 0)),
            pl.BlockSpec((D_MODEL, tn5), lambda n, m: (0, n)),
            pl.BlockSpec((D_MODEL, tn5), lambda n, m: (0, n + nn5)),
            pl.BlockSpec((3, tn5), lambda n, m: (0, n)),
            pl.BlockSpec((3, tn5), lambda n, m: (0, n + nn5)),
            pl.BlockSpec((wd_rows, D_MODEL), lambda n, m: (n * nm5 + m, 0)),
        ],
        out_specs=[pl.BlockSpec((tm5, tn5), lambda n, m: (m, n)),
                   pl.BlockSpec((wd_rows, D_MODEL), lambda n, m: (n * nm5 + m, 0))],
        out_shape=[jax.ShapeDtypeStruct((T, FFN_DIM), BF16),
                   jax.ShapeDtypeStruct((FFN_DIM, D_MODEL), BF16)],
        scratch_shapes=[pltpu.VMEM((D_MODEL, tn5), BF16), pltpu.VMEM((D_MODEL, tn5), BF16),
                        pltpu.VMEM((tm5 + HALO, tn5), F32), pltpu.VMEM((tm5 + HALO, tn5), F32)],
        compiler_params=_params(("arbitrary", "arbitrary"), 56),
        name="ffn_up",
    )(x1b, w_gate_up, w_gate_up, ffn_conv_w, ffn_conv_w, w_down)

    tm6 = 256
    out = pl.pallas_call(
        _ffn_down_kernel,
        grid=(T // tm6,),
        in_specs=[
            pl.BlockSpec((tm6, FFN_DIM), lambda m: (m, 0)),
            full((FFN_DIM, D_MODEL)),
            pl.BlockSpec((tm6, D_MODEL), lambda m: (m, 0)),
            full((1, D_MODEL)), full((1, D_MODEL)),
        ],
        out_specs=pl.BlockSpec((tm6, D_MODEL), lambda m: (m, 0)),
        out_shape=jax.ShapeDtypeStruct((T, D_MODEL), F32),
        compiler_params=_params(("arbitrary",), 56),
        name="ffn_down",
    )(h2, w_down_b, x1, ln2_g.reshape(1, D_MODEL), ln2_b.reshape(1, D_MODEL))
    return out


def kernel(x, positions, w_in, q_norm_g, w_uq, kv_norm_g, w_ukv, conv_w, w_out, ln1_g, ln1_b,
           w_gate_up, ffn_conv_w, w_down, ln2_g, ln2_b):
    batch, seq, _ = x.shape
    x2 = x.reshape(batch * seq, D_MODEL)
    for l in range(DEPTH):
        x2 = _layer(x2, positions, w_in[l], q_norm_g[l], w_uq[l], kv_norm_g[l], w_ukv[l], conv_w[l],
                    w_out[l], ln1_g[l], ln1_b[l], w_gate_up[l], ffn_conv_w[l], w_down[l], ln2_g[l], ln2_b[l],
                    batch, seq)
    return x2.reshape(batch, seq, D_MODEL)
```

```python
import functools
import math

import jax
import jax.numpy as jnp
from jax import lax
from jax.experimental import pallas as pl
from jax.experimental.pallas import tpu as pltpu

D_MODEL = 2048
MLA_HEADS = 8
NOPE = 128
ROPE = 64
V_DIM = 128
V_ROWS = V_DIM + 16
MLA_WIDTH = MLA_HEADS * V_DIM
Q_RANK = 512
KV_RANK = 256
CONV_WIDTH = 1024
FFN_DIM = 5632
ROPE_THETA = 10000.0
LN_EPS = 1e-5
RMS_EPS = 1e-6
DEPTH = 1
ALPHA = (2.0 * DEPTH) ** 0.25
QK_DIM = NOPE + ROPE
Q_SCALE = (QK_DIM ** -0.5) * math.log2(math.e)
QK_PAD = 256
HALO = 8
WCHUNK = 64
NEG = -1e30

F32 = jnp.float32
BF16 = jnp.bfloat16
MIB = 1024 * 1024


def _dot(a, b):
    return jnp.dot(a, b, preferred_element_type=F32)


def _dot_nt(a, b):
    return lax.dot_general(a, b, (((1,), (1,)), ((), ())), preferred_element_type=F32)


def _rms(x, g):
    return x * lax.rsqrt(jnp.mean(x * x, axis=-1, keepdims=True) + RMS_EPS) * g


def _layer_norm(y, g, b):
    mu = jnp.mean(y, axis=-1, keepdims=True)
    d = y - mu
    var = jnp.mean(d * d, axis=-1, keepdims=True)
    return d * lax.rsqrt(var + LN_EPS) * g + b


def _causal_conv3(z, zs_ref, cw, cols, n, row0=0):
    zs_ref[HALO + row0:HALO + row0 + n, cols] = z
    z1 = zs_ref[HALO - 1 + row0:HALO - 1 + row0 + n, cols]
    z2 = zs_ref[HALO - 2 + row0:HALO - 2 + row0 + n, cols]
    return cw[2:3, :] * z + cw[1:2, :] * z1 + cw[0:1, :] * z2


def _latent_kernel(x_ref, posr_ref, invf_ref, win_ref, wc0_ref, wc1_ref, wc2_ref, gq_ref, gkv_ref,
                   wqt_ref, wk_ref, wvt_ref,
                   qt_ref, kn_ref, kr_ref, vt_ref, wconv_ref, wlat_ref, *, tm, tk):
    half = ROPE // 2
    latent = Q_RANK + KV_RANK

    @pl.when(pl.program_id(0) == 0)
    def _():
        wlat_ref[...] = win_ref[...].astype(BF16)

    for n, wc_ref in enumerate((wc0_ref, wc1_ref, wc2_ref)):
        wconv_ref[n * WCHUNK:(n + 1) * WCHUNK, :] = wc_ref[...].astype(BF16)

    xb = x_ref[...].astype(BF16)
    h = _dot_nt(xb, wlat_ref[...])
    cqn = _rms(h[:, :Q_RANK], gq_ref[...]).astype(BF16)
    ckvn = _rms(h[:, Q_RANK:latent], gkv_ref[...]).astype(BF16)

    ang_t = invf_ref[...] * posr_ref[...].astype(F32)
    cos_t, sin_t = jnp.cos(ang_t), jnp.sin(ang_t)
    table = jnp.concatenate([cos_t, cos_t, sin_t, sin_t], axis=0).T
    k_r = h[:, latent:latent + ROPE]
    kr_ref[0] = (jnp.concatenate([k_r, k_r], axis=1) * table).astype(BF16)

    kn = _dot(ckvn, wk_ref[...])
    vt = _dot_nt(wvt_ref[...], ckvn)
    qt = _dot_nt(wqt_ref[...], cqn)
    ones_row = (lax.broadcasted_iota(jnp.int32, (V_ROWS - V_DIM, tk), 0) == 0).astype(BF16)
    for hh in range(MLA_HEADS):
        kn_ref[0, hh] = kn[:, NOPE * hh:NOPE * hh + NOPE].astype(BF16)
        for c in range(tm // tk):
            vt_ref[0, hh, c, 0:V_DIM, :] = vt[V_DIM * hh:V_DIM * hh + V_DIM, tk * c:tk * c + tk].astype(BF16)
            vt_ref[0, hh, c, V_DIM:V_ROWS, :] = ones_row
        base = QK_DIM * hh
        r1 = qt[base + NOPE:base + NOPE + half]
        r2 = qt[base + NOPE + half:base + QK_DIM]
        qr1 = (r1 * cos_t - r2 * sin_t).astype(BF16)
        qr2 = (r2 * cos_t + r1 * sin_t).astype(BF16)
        qt_ref[0, hh, 0:NOPE, :] = qt[base:base + NOPE].astype(BF16)
        qt_ref[0, hh, NOPE:NOPE + half, :] = qr1
        qt_ref[0, hh, NOPE + half:NOPE + 2 * half, :] = qr2
        qt_ref[0, hh, NOPE + 2 * half:NOPE + 3 * half, :] = qr2
        qt_ref[0, hh, NOPE + 3 * half:QK_PAD, :] = -qr1


def _convmix_kernel(x_ref, w_ref, cw_ref, wo_ref, o_ref, wob_ref, zs_ref, *, tm, tc, tiles_per_seq):
    @pl.when(pl.program_id(0) % tiles_per_seq == 0)
    def _():
        zs_ref[0:HALO, :] = jnp.zeros((HALO, CONV_WIDTH), F32)

    wob_ref[...] = wo_ref[...].astype(BF16)

    xb = x_ref[...].astype(BF16)
    for c in range(CONV_WIDTH // tc):
        cols = slice(c * tc, (c + 1) * tc)
        gate_b = _dot_nt(xb, w_ref[c * tc:(c + 1) * tc, :])
        gate_c = _dot_nt(xb, w_ref[CONV_WIDTH + c * tc:CONV_WIDTH + (c + 1) * tc, :])
        x_c = _dot_nt(xb, w_ref[2 * CONV_WIDTH + c * tc:2 * CONV_WIDTH + (c + 1) * tc, :])
        y = _causal_conv3(gate_c * x_c, zs_ref, cw_ref[:, cols], cols, tm)
        o_ref[:, cols] = (gate_b * y).astype(BF16)
    zs_ref[0:HALO, :] = zs_ref[tm:tm + HALO, :]


def _attn_kernel(qt_ref, kn_ref, kr_ref, vt_ref, o_ref, m_ref, acc_ref, *, tq, tk, heads, lookahead):
    qi = pl.program_id(1)
    per_q = tq // tk
    trip_blocks = 2 * per_q
    m_ref[...] = jnp.full(m_ref.shape, NEG, F32)
    acc_ref[...] = jnp.zeros(acc_ref.shape, F32)

    def run(chains):
        def scores(chain):
            g, j, c0, _ = chain
            start = pl.multiple_of(j * tk, tk)
            k = jnp.concatenate([kn_ref[0, g, pl.ds(start, tk), :], kr_ref[0, pl.ds(start, tk), :]], axis=1)
            return _dot(k, qt_ref[0, g, :, c0:])

        ahead = [scores(ch) for ch in chains[:lookahead]]
        for i, (g, j, c0, mask) in enumerate(chains):
            s = ahead.pop(0)
            if i + lookahead < len(chains):
                ahead.append(scores(chains[i + lookahead]))
            if mask is not None:
                s = jnp.where(mask, s, NEG)
            m = m_ref[g, :, c0:]
            m_new = jnp.maximum(m, jnp.max(s, axis=0, keepdims=True))
            p = jnp.exp2(s - m_new)
            a = jnp.exp2(m - m_new)
            m_ref[g, :, c0:] = m_new
            acc_ref[g, :, c0:] = a * acc_ref[g, :, c0:] + _dot(vt_ref[0, g, j], p.astype(BF16))

    def full_blocks(first, count):
        return [(g, first + b, 0, None) for b in range(count) for g in range(heads)]

    def below_diagonal(t, c):
        run(full_blocks(trip_blocks * t, trip_blocks))
        return c

    n_below = qi * per_q
    lax.fori_loop(0, n_below // trip_blocks, below_diagonal, 0)

    @pl.when(n_below % trip_blocks != 0)
    def _():
        run(full_blocks(n_below - per_q, per_q))

    diag = []
    for b in range(per_q):
        c0 = b * tk
        key = lax.broadcasted_iota(jnp.int32, (tk, tq - c0), 0)
        qry = lax.broadcasted_iota(jnp.int32, (tk, tq - c0), 1)
        diag += [(g, per_q * qi + b, c0, key <= qry) for g in range(heads)]
    run(diag)
    for g in range(heads):
        out_t = acc_ref[g, 0:V_DIM, :] / acc_ref[g, V_DIM:V_DIM + 1, :]
        o_ref[0, :, V_DIM * g:V_DIM * g + V_DIM] = out_t.T.astype(BF16)


def _outproj_kernel(a_ref, c_ref, x_ref, w_ref, g_ref, b_ref, o32_ref, o16_ref, *, tm, rc):
    for r in range(tm // rc):
        rows = slice(r * rc, (r + 1) * rc)
        y = _dot(a_ref[rows, :], w_ref[0:MLA_WIDTH, :]) + _dot(c_ref[rows, :], w_ref[MLA_WIDTH:, :])
        y = y + ALPHA * x_ref[rows, :]
        out = _layer_norm(y, g_ref[...], b_ref[...])
        o32_ref[rows, :] = out
        o16_ref[rows, :] = out.astype(BF16)


def _ffn_up_kernel(x_ref, wg_ref, wu_ref, cwg_ref, cwu_ref, wd_ref, o_ref, wdb_ref,
                   wgb_ref, wub_ref, gs_ref, us_ref, *, tm, tn, tiles_per_seq):
    mi = pl.program_id(1)

    @pl.when(mi == 0)
    def _():
        wgb_ref[...] = wg_ref[...].astype(BF16)
        wub_ref[...] = wu_ref[...].astype(BF16)

    @pl.when(mi % tiles_per_seq == 0)
    def _():
        gs_ref[0:HALO, :] = jnp.zeros((HALO, tn), F32)
        us_ref[0:HALO, :] = jnp.zeros((HALO, tn), F32)

    wdb_ref[...] = wd_ref[...].astype(BF16)

    xb = x_ref[...]
    cols = slice(0, tn)
    g = _causal_conv3(_dot(xb, wgb_ref[...]), gs_ref, cwg_ref[...], cols, tm)
    u = _causal_conv3(_dot(xb, wub_ref[...]), us_ref, cwu_ref[...], cols, tm)
    o_ref[...] = (g / (1.0 + jnp.exp(-g)) * u).astype(BF16)
    gs_ref[0:HALO, :] = gs_ref[tm:tm + HALO, :]
    us_ref[0:HALO, :] = us_ref[tm:tm + HALO, :]


def _ffn_down_kernel(h_ref, w_ref, x_ref, g_ref, b_ref, o_ref):
    y = _dot(h_ref[...], w_ref[...]) + ALPHA * x_ref[...]
    o_ref[...] = _layer_norm(y, g_ref[...], b_ref[...])


def _params(sem, vmem_mib, flags=None):
    return pltpu.CompilerParams(dimension_semantics=sem, vmem_limit_bytes=vmem_mib * MIB, flags=flags)


def _layer(x2, pos, w_in, q_norm_g, w_uq, kv_norm_g, w_ukv, conv_w, w_out, ln1_g, ln1_b,
           w_gate_up, ffn_conv_w, w_down, ln2_g, ln2_b, batch, seq):
    T = batch * seq

    half = ROPE // 2
    wqt = w_uq.T.astype(BF16)
    ukv = w_ukv.reshape(KV_RANK, MLA_HEADS, NOPE + V_DIM)
    wk = ukv[:, :, :NOPE].reshape(KV_RANK, MLA_HEADS * NOPE).astype(BF16)
    wvt = ukv[:, :, NOPE:].reshape(KV_RANK, MLA_HEADS * V_DIM).T.astype(BF16)
    gq = (q_norm_g * Q_SCALE).reshape(1, Q_RANK)
    gkv = kv_norm_g.reshape(1, KV_RANK)
    inv_freq = (ROPE_THETA ** (-jnp.arange(0, ROPE, 2, dtype=F32) / ROPE)).reshape(half, 1)
    posr = pos.reshape(1, T)

    tm = 512
    tk = 256
    nblk = seq // tm
    full = lambda shape: pl.BlockSpec(shape, lambda *_: (0,) * len(shape), pipeline_mode=pl.Buffered(1))
    n1 = T // tm
    w_in_t = w_in.T
    lat_rows = Q_RANK + KV_RANK + 128
    conv_row0 = Q_RANK + KV_RANK + ROPE
    assert conv_row0 % WCHUNK == 0 and 3 * CONV_WIDTH == 3 * WCHUNK * n1
    wchunk = lambda n: pl.BlockSpec((WCHUNK, D_MODEL), lambda i: (conv_row0 // WCHUNK + 3 * i + n, 0))
    qt, kn, kr, vt, wconv = pl.pallas_call(
        functools.partial(_latent_kernel, tm=tm, tk=tk),
        grid=(n1,),
        in_specs=[
            pl.BlockSpec((tm, D_MODEL), lambda i: (i, 0)),
            pl.BlockSpec((1, tm), lambda i: (0, i)),
            full((half, 1)),
            full((lat_rows, D_MODEL)),
            wchunk(0), wchunk(1), wchunk(2),
            full((1, Q_RANK)), full((1, KV_RANK)),
            full((MLA_HEADS * QK_DIM, Q_RANK)),
            full((KV_RANK, MLA_HEADS * NOPE)),
            full((MLA_HEADS * V_DIM, KV_RANK)),
        ],
        out_specs=[
            pl.BlockSpec((1, MLA_HEADS, QK_PAD, tm), lambda i: (i // nblk, 0, 0, i % nblk)),
            pl.BlockSpec((1, MLA_HEADS, tm, NOPE), lambda i: (i // nblk, 0, i % nblk, 0)),
            pl.BlockSpec((1, tm, 128), lambda i: (i // nblk, i % nblk, 0)),
            pl.BlockSpec((1, MLA_HEADS, tm // tk, V_ROWS, tk), lambda i: (i // nblk, 0, i % nblk, 0, 0)),
            pl.BlockSpec((3 * WCHUNK, D_MODEL), lambda i: (i, 0)),
        ],
        out_shape=[
            jax.ShapeDtypeStruct((batch, MLA_HEADS, QK_PAD, seq), BF16),
            jax.ShapeDtypeStruct((batch, MLA_HEADS, seq, NOPE), BF16),
            jax.ShapeDtypeStruct((batch, seq, 128), BF16),
            jax.ShapeDtypeStruct((batch, MLA_HEADS, seq // tk, V_ROWS, tk), BF16),
            jax.ShapeDtypeStruct((3 * CONV_WIDTH, D_MODEL), BF16),
        ],
        scratch_shapes=[pltpu.VMEM((lat_rows, D_MODEL), BF16)],
        compiler_params=_params(("arbitrary",), 48),
        name="latent",
    )(x2, posr, inv_freq, w_in_t, w_in_t, w_in_t, w_in_t, gq, gkv, wqt, wk, wvt)

    tm2 = 512
    n2 = T // tm2
    wo_rows = D_MODEL // n2
    conv_out, w_out_b = pl.pallas_call(
        functools.partial(_convmix_kernel, tm=tm2, tc=256, tiles_per_seq=seq // tm2),
        grid=(n2,),
        in_specs=[
            pl.BlockSpec((tm2, D_MODEL), lambda i: (i, 0)),
            full((3 * CONV_WIDTH, D_MODEL)),
            full((3, CONV_WIDTH)),
            pl.BlockSpec((wo_rows, D_MODEL), lambda i: (i, 0)),
        ],
        out_specs=[pl.BlockSpec((tm2, CONV_WIDTH), lambda i: (i, 0)),
                   pl.BlockSpec((wo_rows, D_MODEL), lambda i: (i, 0))],
        out_shape=[jax.ShapeDtypeStruct((T, CONV_WIDTH), BF16),
                   jax.ShapeDtypeStruct((D_MODEL, D_MODEL), BF16)],
        scratch_shapes=[pltpu.VMEM((tm2 + HALO, CONV_WIDTH), F32)],
        compiler_params=_params(("arbitrary",), 48),
        name="convmix",
    )(x2, wconv, conv_w, w_out)

    tq = 512
    attn = pl.pallas_call(
        functools.partial(_attn_kernel, tq=tq, tk=tk, heads=MLA_HEADS, lookahead=2),
        grid=(batch, seq // tq),
        in_specs=[
            pl.BlockSpec((1, MLA_HEADS, QK_PAD, tq), lambda b, i: (b, 0, 0, i)),
            pl.BlockSpec((1, MLA_HEADS, seq, NOPE), lambda b, i: (b, 0, 0, 0)),
            pl.BlockSpec((1, seq, 128), lambda b, i: (b, 0, 0)),
            pl.BlockSpec((1, MLA_HEADS, seq // tk, V_ROWS, tk), lambda b, i: (b, 0, 0, 0, 0)),
        ],
        out_specs=pl.BlockSpec((1, tq, MLA_WIDTH), lambda b, i: (b, i, 0)),
        out_shape=jax.ShapeDtypeStruct((batch, seq, MLA_WIDTH), BF16),
        scratch_shapes=[pltpu.VMEM((MLA_HEADS, 1, tq), F32), pltpu.VMEM((MLA_HEADS, V_ROWS, tq), F32)],
        compiler_params=_params(("arbitrary", "arbitrary"), 56),
        name="attn",
    )(qt, kn, kr, vt)
    attn = attn.reshape(T, MLA_WIDTH)

    tm4 = 512
    x1, x1b = pl.pallas_call(
        functools.partial(_outproj_kernel, tm=tm4, rc=128),
        grid=(T // tm4,),
        in_specs=[
            pl.BlockSpec((tm4, MLA_WIDTH), lambda i: (i, 0)),
            pl.BlockSpec((tm4, CONV_WIDTH), lambda i: (i, 0)),
            pl.BlockSpec((tm4, D_MODEL), lambda i: (i, 0)),
            full((D_MODEL, D_MODEL)), full((1, D_MODEL)), full((1, D_MODEL)),
        ],
        out_specs=[pl.BlockSpec((tm4, D_MODEL), lambda i: (i, 0)), pl.BlockSpec((tm4, D_MODEL), lambda i: (i, 0))],
        out_shape=[jax.ShapeDtypeStruct((T, D_MODEL), F32), jax.ShapeDtypeStruct((T, D_MODEL), BF16)],
        compiler_params=_params(("arbitrary",), 56),
        name="outproj",
    )(attn, conv_out, x2, w_out_b, ln1_g.reshape(1, D_MODEL), ln1_b.reshape(1, D_MODEL))

    tm5, tn5 = 1024, 512
    nn5 = FFN_DIM // tn5
    nm5 = T // tm5
    wd_rows = FFN_DIM // (nn5 * nm5)
    h2, w_down_b = pl.pallas_call(
        functools.partial(_ffn_up_kernel, tm=tm5, tn=tn5, tiles_per_seq=seq // tm5),
        grid=(nn5, nm5),
        in_specs=[
            pl.BlockSpec((tm5, D_MODEL), lambda n, m: (m, 0)),
            pl.BlockSpec((D_MODEL, tn5), lambda n, m: (0, n)),
            pl.BlockSpec((D_MODEL, tn5), lambda n, m: (0, n + nn5)),
            pl.BlockSpec((3, tn5), lambda n, m: (0, n)),
            pl.BlockSpec((3, tn5), lambda n, m: (0, n + nn5)),
            pl.BlockSpec((wd_rows, D_MODEL), lambda n, m: (n * nm5 + m, 0)),
        ],
        out_specs=[pl.BlockSpec((tm5, tn5), lambda n, m: (m, n)),
                   pl.BlockSpec((wd_rows, D_MODEL), lambda n, m: (n * nm5 + m, 0))],
        out_shape=[jax.ShapeDtypeStruct((T, FFN_DIM), BF16),
                   jax.ShapeDtypeStruct((FFN_DIM, D_MODEL), BF16)],
        scratch_shapes=[pltpu.VMEM((D_MODEL, tn5), BF16), pltpu.VMEM((D_MODEL, tn5), BF16),
                        pltpu.VMEM((tm5 + HALO, tn5), F32), pltpu.VMEM((tm5 + HALO, tn5), F32)],
        compiler_params=_params(("arbitrary", "arbitrary"), 56),
        name="ffn_up",
    )(x1b, w_gate_up, w_gate_up, ffn_conv_w, ffn_conv_w, w_down)

    tm6 = 256
    out = pl.pallas_call(
        _ffn_down_kernel,
        grid=(T // tm6,),
        in_specs=[
            pl.BlockSpec((tm6, FFN_DIM), lambda m: (m, 0)),
            full((FFN_DIM, D_MODEL)),
            pl.BlockSpec((tm6, D_MODEL), lambda m: (m, 0)),
            full((1, D_MODEL)), full((1, D_MODEL)),
        ],
        out_specs=pl.BlockSpec((tm6, D_MODEL), lambda m: (m, 0)),
        out_shape=jax.ShapeDtypeStruct((T, D_MODEL), F32),
        compiler_params=_params(("arbitrary",), 56),
        name="ffn_down",
    )(h2, w_down_b, x1, ln2_g.reshape(1, D_MODEL), ln2_b.reshape(1, D_MODEL))
    return out


def kernel(x, positions, w_in, q_norm_g, w_uq, kv_norm_g, w_ukv, conv_w, w_out, ln1_g, ln1_b,
           w_gate_up, ffn_conv_w, w_down, ln2_g, ln2_b):
    batch, seq, _ = x.shape
    x2 = x.reshape(batch * seq, D_MODEL)
    for l in range(DEPTH):
        x2 = _layer(x2, positions, w_in[l], q_norm_g[l], w_uq[l], kv_norm_g[l], w_ukv[l], conv_w[l],
                    w_out[l], ln1_g[l], ln1_b[l], w_gate_up[l], ffn_conv_w[l], w_down[l], ln2_g[l], ln2_b[l],
                    batch, seq)
    return x2.reshape(batch, seq, D_MODEL)
```

```python
import functools
import math

import jax
import jax.numpy as jnp
from jax import lax
from jax.experimental import pallas as pl
from jax.experimental.pallas import tpu as pltpu

D_MODEL = 2048
MLA_HEADS = 8
NOPE = 128
ROPE = 64
V_DIM = 128
V_ROWS = V_DIM + 16
MLA_WIDTH = MLA_HEADS * V_DIM
Q_RANK = 512
KV_RANK = 256
CONV_WIDTH = 1024
FFN_DIM = 5632
ROPE_THETA = 10000.0
LN_EPS = 1e-5
RMS_EPS = 1e-6
DEPTH = 1
ALPHA = (2.0 * DEPTH) ** 0.25
QK_DIM = NOPE + ROPE
Q_SCALE = (QK_DIM ** -0.5) * math.log2(math.e)
QK_PAD = 256
HALO = 8
WCHUNK = 64
NEG = -1e30

F32 = jnp.float32
BF16 = jnp.bfloat16
MIB = 1024 * 1024


def _dot(a, b):
    return jnp.dot(a, b, preferred_element_type=F32)


def _dot_nt(a, b):
    return lax.dot_general(a, b, (((1,), (1,)), ((), ())), preferred_element_type=F32)


def _rms(x, g):
    return x * lax.rsqrt(jnp.mean(x * x, axis=-1, keepdims=True) + RMS_EPS) * g


def _layer_norm(y, g, b):
    mu = jnp.mean(y, axis=-1, keepdims=True)
    d = y - mu
    var = jnp.mean(d * d, axis=-1, keepdims=True)
    return d * lax.rsqrt(var + LN_EPS) * g + b


def _causal_conv3(z, hs_ref, cw, cols):
    n = z.shape[0]
    w0, w1, w2 = cw[0:1, :], cw[1:2, :], cw[2:3, :]
    y = w2 * z + w1 * pltpu.roll(z, 1, axis=0) + w0 * pltpu.roll(z, 2, axis=0)
    head = z[0:HALO]
    hs_ref[HALO:2 * HALO, cols] = head
    y_head = w2 * head + w1 * hs_ref[HALO - 1:2 * HALO - 1, cols] + w0 * hs_ref[HALO - 2:2 * HALO - 2, cols]
    hs_ref[0:HALO, cols] = z[n - HALO:n]
    return jnp.concatenate([y_head, y[HALO:]], axis=0)


def _latent_kernel(x_ref, posr_ref, invf_ref, win_ref, wc0_ref, wc1_ref, wc2_ref, gq_ref, gkv_ref,
                   wqt_ref, wk_ref, wvt_ref,
                   qt_ref, kn_ref, kr_ref, vt_ref, wconv_ref, wlat_ref, *, tm, tk):
    half = ROPE // 2
    latent = Q_RANK + KV_RANK

    @pl.when(pl.program_id(0) == 0)
    def _():
        wlat_ref[...] = win_ref[...].astype(BF16)

    for n, wc_ref in enumerate((wc0_ref, wc1_ref, wc2_ref)):
        wconv_ref[n * WCHUNK:(n + 1) * WCHUNK, :] = wc_ref[...].astype(BF16)

    xb = x_ref[...].astype(BF16)
    h = _dot_nt(xb, wlat_ref[...])
    cqn = _rms(h[:, :Q_RANK], gq_ref[...]).astype(BF16)
    ckvn = _rms(h[:, Q_RANK:latent], gkv_ref[...]).astype(BF16)

    ang_t = invf_ref[...] * posr_ref[...].astype(F32)
    cos_t, sin_t = jnp.cos(ang_t), jnp.sin(ang_t)
    table = jnp.concatenate([cos_t, cos_t, sin_t, sin_t], axis=0).T
    k_r = h[:, latent:latent + ROPE]
    kr_ref[0] = (jnp.concatenate([k_r, k_r], axis=1) * table).astype(BF16)

    kn = _dot(ckvn, wk_ref[...])
    vt = _dot_nt(wvt_ref[...], ckvn)
    qt = _dot_nt(wqt_ref[...], cqn)
    ones_row = (lax.broadcasted_iota(jnp.int32, (V_ROWS - V_DIM, tk), 0) == 0).astype(BF16)
    for hh in range(MLA_HEADS):
        kn_ref[0, hh] = kn[:, NOPE * hh:NOPE * hh + NOPE].astype(BF16)
        for c in range(tm // tk):
            vt_ref[0, hh, c, 0:V_DIM, :] = vt[V_DIM * hh:V_DIM * hh + V_DIM, tk * c:tk * c + tk].astype(BF16)
            vt_ref[0, hh, c, V_DIM:V_ROWS, :] = ones_row
        base = QK_DIM * hh
        r1 = qt[base + NOPE:base + NOPE + half]
        r2 = qt[base + NOPE + half:base + QK_DIM]
        qr1 = (r1 * cos_t - r2 * sin_t).astype(BF16)
        qr2 = (r2 * cos_t + r1 * sin_t).astype(BF16)
        qt_ref[0, hh, 0:NOPE, :] = qt[base:base + NOPE].astype(BF16)
        qt_ref[0, hh, NOPE:NOPE + half, :] = qr1
        qt_ref[0, hh, NOPE + half:NOPE + 2 * half, :] = qr2
        qt_ref[0, hh, NOPE + 2 * half:NOPE + 3 * half, :] = qr2
        qt_ref[0, hh, NOPE + 3 * half:QK_PAD, :] = -qr1


def _convmix_kernel(x_ref, w_ref, cw_ref, wo_ref, o_ref, wob_ref, zs_ref, *, tc, tiles_per_seq):
    @pl.when(pl.program_id(0) % tiles_per_seq == 0)
    def _():
        zs_ref[0:HALO, :] = jnp.zeros((HALO, CONV_WIDTH), F32)

    wob_ref[...] = wo_ref[...].astype(BF16)

    xb = x_ref[...].astype(BF16)
    for c in range(CONV_WIDTH // tc):
        cols = slice(c * tc, (c + 1) * tc)
        gate_b = _dot_nt(xb, w_ref[c * tc:(c + 1) * tc, :])
        gate_c = _dot_nt(xb, w_ref[CONV_WIDTH + c * tc:CONV_WIDTH + (c + 1) * tc, :])
        x_c = _dot_nt(xb, w_ref[2 * CONV_WIDTH + c * tc:2 * CONV_WIDTH + (c + 1) * tc, :])
        y = _causal_conv3(gate_c * x_c, zs_ref, cw_ref[:, cols], cols)
        o_ref[:, cols] = (gate_b * y).astype(BF16)


def _attn_kernel(qt_ref, kn_ref, kr_ref, vt_ref, o_ref, m_ref, acc_ref, *, tq, tk, heads, lookahead):
    qi = pl.program_id(1)
    per_q = tq // tk
    trip_blocks = 2 * per_q
    m_ref[...] = jnp.full(m_ref.shape, NEG, F32)
    acc_ref[...] = jnp.zeros(acc_ref.shape, F32)

    def run(chains):
        def scores(chain):
            g, j, c0, _ = chain
            start = pl.multiple_of(j * tk, tk)
            k = jnp.concatenate([kn_ref[0, g, pl.ds(start, tk), :], kr_ref[0, pl.ds(start, tk), :]], axis=1)
            return _dot(k, qt_ref[0, g, :, c0:])

        ahead = [scores(ch) for ch in chains[:lookahead]]
        for i, (g, j, c0, mask) in enumerate(chains):
            s = ahead.pop(0)
            if i + lookahead < len(chains):
                ahead.append(scores(chains[i + lookahead]))
            if mask is not None:
                s = jnp.where(mask, s, NEG)
            m = m_ref[g, :, c0:]
            m_new = jnp.maximum(m, jnp.max(s, axis=0, keepdims=True))
            p = jnp.exp2(s - m_new)
            a = jnp.exp2(m - m_new)
            m_ref[g, :, c0:] = m_new
            acc_ref[g, :, c0:] = a * acc_ref[g, :, c0:] + _dot(vt_ref[0, g, j], p.astype(BF16))

    def full_blocks(first, count):
        return [(g, first + b, 0, None) for b in range(count) for g in range(heads)]

    def below_diagonal(t, c):
        run(full_blocks(trip_blocks * t, trip_blocks))
        return c

    n_below = qi * per_q
    lax.fori_loop(0, n_below // trip_blocks, below_diagonal, 0)

    @pl.when(n_below % trip_blocks != 0)
    def _():
        run(full_blocks(n_below - per_q, per_q))

    diag = []
    for b in range(per_q):
        c0 = b * tk
        key = lax.broadcasted_iota(jnp.int32, (tk, tq - c0), 0)
        qry = lax.broadcasted_iota(jnp.int32, (tk, tq - c0), 1)
        diag += [(g, per_q * qi + b, c0, key <= qry) for g in range(heads)]
    run(diag)
    for g in range(heads):
        out_t = acc_ref[g, 0:V_DIM, :] / acc_ref[g, V_DIM:V_DIM + 1, :]
        o_ref[0, :, V_DIM * g:V_DIM * g + V_DIM] = out_t.T.astype(BF16)


def _outproj_kernel(a_ref, c_ref, x_ref, w_ref, g_ref, b_ref, o32_ref, o16_ref, *, tm, rc):
    for r in range(tm // rc):
        rows = slice(r * rc, (r + 1) * rc)
        mixed = jnp.concatenate([a_ref[rows, :], c_ref[rows, :]], axis=1)
        y = _dot(mixed, w_ref[...]) + ALPHA * x_ref[rows, :]
        out = _layer_norm(y, g_ref[...], b_ref[...])
        o32_ref[rows, :] = out
        o16_ref[rows, :] = out.astype(BF16)


def _ffn_up_kernel(x_ref, wg_ref, wu_ref, cwg_ref, cwu_ref, wd_ref, o_ref, wdb_ref,
                   wgb_ref, wub_ref, gs_ref, us_ref, *, tn, tiles_per_seq):
    mi = pl.program_id(1)

    @pl.when(mi == 0)
    def _():
        wgb_ref[...] = wg_ref[...].astype(BF16)
        wub_ref[...] = wu_ref[...].astype(BF16)

    @pl.when(mi % tiles_per_seq == 0)
    def _():
        gs_ref[0:HALO, :] = jnp.zeros((HALO, tn), F32)
        us_ref[0:HALO, :] = jnp.zeros((HALO, tn), F32)

    wdb_ref[...] = wd_ref[...].astype(BF16)

    xb = x_ref[...]
    cols = slice(0, tn)
    g = _causal_conv3(_dot(xb, wgb_ref[...]), gs_ref, cwg_ref[...], cols)
    u = _causal_conv3(_dot(xb, wub_ref[...]), us_ref, cwu_ref[...], cols)
    o_ref[...] = (g / (1.0 + jnp.exp(-g)) * u).astype(BF16)


def _ffn_down_kernel(h_ref, w_ref, x_ref, g_ref, b_ref, o_ref):
    y = _dot(h_ref[...], w_ref[...]) + ALPHA * x_ref[...]
    o_ref[...] = _layer_norm(y, g_ref[...], b_ref[...])


def _params(sem, vmem_mib):
    return pltpu.CompilerParams(dimension_semantics=sem, vmem_limit_bytes=vmem_mib * MIB)


def _layer(x2, pos, w_in, q_norm_g, w_uq, kv_norm_g, w_ukv, conv_w, w_out, ln1_g, ln1_b,
           w_gate_up, ffn_conv_w, w_down, ln2_g, ln2_b, batch, seq):
    T = batch * seq

    half = ROPE // 2
    wqt = w_uq.T.astype(BF16)
    ukv = w_ukv.reshape(KV_RANK, MLA_HEADS, NOPE + V_DIM)
    wk = ukv[:, :, :NOPE].reshape(KV_RANK, MLA_HEADS * NOPE).astype(BF16)
    wvt = ukv[:, :, NOPE:].reshape(KV_RANK, MLA_HEADS * V_DIM).T.astype(BF16)
    gq = (q_norm_g * Q_SCALE).reshape(1, Q_RANK)
    gkv = kv_norm_g.reshape(1, KV_RANK)
    inv_freq = (ROPE_THETA ** (-jnp.arange(0, ROPE, 2, dtype=F32) / ROPE)).reshape(half, 1)
    posr = pos.reshape(1, T)

    tm = 512
    tk = 256
    nblk = seq // tm
    full = lambda shape: pl.BlockSpec(shape, lambda *_: (0,) * len(shape), pipeline_mode=pl.Buffered(1))
    n1 = T // tm
    w_in_t = w_in.T
    lat_rows = Q_RANK + KV_RANK + 128
    conv_row0 = Q_RANK + KV_RANK + ROPE
    assert conv_row0 % WCHUNK == 0 and 3 * CONV_WIDTH == 3 * WCHUNK * n1
    wchunk = lambda n: pl.BlockSpec((WCHUNK, D_MODEL), lambda i: (conv_row0 // WCHUNK + 3 * i + n, 0))
    qt, kn, kr, vt, wconv = pl.pallas_call(
        functools.partial(_latent_kernel, tm=tm, tk=tk),
        grid=(n1,),
        in_specs=[
            pl.BlockSpec((tm, D_MODEL), lambda i: (i, 0)),
            pl.BlockSpec((1, tm), lambda i: (0, i)),
            full((half, 1)),
            full((lat_rows, D_MODEL)),
            wchunk(0), wchunk(1), wchunk(2),
            full((1, Q_RANK)), full((1, KV_RANK)),
            full((MLA_HEADS * QK_DIM, Q_RANK)),
            full((KV_RANK, MLA_HEADS * NOPE)),
            full((MLA_HEADS * V_DIM, KV_RANK)),
        ],
        out_specs=[
            pl.BlockSpec((1, MLA_HEADS, QK_PAD, tm), lambda i: (i // nblk, 0, 0, i % nblk)),
            pl.BlockSpec((1, MLA_HEADS, tm, NOPE), lambda i: (i // nblk, 0, i % nblk, 0)),
            pl.BlockSpec((1, tm, 128), lambda i: (i // nblk, i % nblk, 0)),
            pl.BlockSpec((1, MLA_HEADS, tm // tk, V_ROWS, tk), lambda i: (i // nblk, 0, i % nblk, 0, 0)),
            pl.BlockSpec((3 * WCHUNK, D_MODEL), lambda i: (i, 0)),
        ],
        out_shape=[
            jax.ShapeDtypeStruct((batch, MLA_HEADS, QK_PAD, seq), BF16),
            jax.ShapeDtypeStruct((batch, MLA_HEADS, seq, NOPE), BF16),
            jax.ShapeDtypeStruct((batch, seq, 128), BF16),
            jax.ShapeDtypeStruct((batch, MLA_HEADS, seq // tk, V_ROWS, tk), BF16),
            jax.ShapeDtypeStruct((3 * CONV_WIDTH, D_MODEL), BF16),
        ],
        scratch_shapes=[pltpu.VMEM((lat_rows, D_MODEL), BF16)],
        compiler_params=_params(("arbitrary",), 48),
        name="latent",
    )(x2, posr, inv_freq, w_in_t, w_in_t, w_in_t, w_in_t, gq, gkv, wqt, wk, wvt)

    tm2 = 512
    n2 = T // tm2
    wo_rows = D_MODEL // n2
    conv_out, w_out_b = pl.pallas_call(
        functools.partial(_convmix_kernel, tc=256, tiles_per_seq=seq // tm2),
        grid=(n2,),
        in_specs=[
            pl.BlockSpec((tm2, D_MODEL), lambda i: (i, 0)),
            full((3 * CONV_WIDTH, D_MODEL)),
            full((3, CONV_WIDTH)),
            pl.BlockSpec((wo_rows, D_MODEL), lambda i: (i, 0)),
        ],
        out_specs=[pl.BlockSpec((tm2, CONV_WIDTH), lambda i: (i, 0)),
                   pl.BlockSpec((wo_rows, D_MODEL), lambda i: (i, 0))],
        out_shape=[jax.ShapeDtypeStruct((T, CONV_WIDTH), BF16),
                   jax.ShapeDtypeStruct((D_MODEL, D_MODEL), BF16)],
        scratch_shapes=[pltpu.VMEM((2 * HALO, CONV_WIDTH), F32)],
        compiler_params=_params(("arbitrary",), 48),
        name="convmix",
    )(x2, wconv, conv_w, w_out)

    tq = 512
    attn = pl.pallas_call(
        functools.partial(_attn_kernel, tq=tq, tk=tk, heads=MLA_HEADS, lookahead=2),
        grid=(batch, seq // tq),
        in_specs=[
            pl.BlockSpec((1, MLA_HEADS, QK_PAD, tq), lambda b, i: (b, 0, 0, i)),
            pl.BlockSpec((1, MLA_HEADS, seq, NOPE), lambda b, i: (b, 0, 0, 0)),
            pl.BlockSpec((1, seq, 128), lambda b, i: (b, 0, 0)),
            pl.BlockSpec((1, MLA_HEADS, seq // tk, V_ROWS, tk), lambda b, i: (b, 0, 0, 0, 0)),
        ],
        out_specs=pl.BlockSpec((1, tq, MLA_WIDTH), lambda b, i: (b, i, 0)),
        out_shape=jax.ShapeDtypeStruct((batch, seq, MLA_WIDTH), BF16),
        scratch_shapes=[pltpu.VMEM((MLA_HEADS, 1, tq), F32), pltpu.VMEM((MLA_HEADS, V_ROWS, tq), F32)],
        compiler_params=_params(("arbitrary", "arbitrary"), 56),
        name="attn",
    )(qt, kn, kr, vt)
    attn = attn.reshape(T, MLA_WIDTH)

    tm4 = 512
    x1, x1b = pl.pallas_call(
        functools.partial(_outproj_kernel, tm=tm4, rc=128),
        grid=(T // tm4,),
        in_specs=[
            pl.BlockSpec((tm4, MLA_WIDTH), lambda i: (i, 0)),
            pl.BlockSpec((tm4, CONV_WIDTH), lambda i: (i, 0)),
            pl.BlockSpec((tm4, D_MODEL), lambda i: (i, 0)),
            full((D_MODEL, D_MODEL)), full((1, D_MODEL)), full((1, D_MODEL)),
        ],
        out_specs=[pl.BlockSpec((tm4, D_MODEL), lambda i: (i, 0)), pl.BlockSpec((tm4, D_MODEL), lambda i: (i, 0))],
        out_shape=[jax.ShapeDtypeStruct((T, D_MODEL), F32), jax.ShapeDtypeStruct((T, D_MODEL), BF16)],
        compiler_params=_params(("arbitrary",), 56),
        name="outproj",
    )(attn, conv_out, x2, w_out_b, ln1_g.reshape(1, D_MODEL), ln1_b.reshape(1, D_MODEL))

    tm5, tn5 = 1024, 512
    nn5 = FFN_DIM // tn5
    nm5 = T // tm5
    wd_rows = FFN_DIM // (nn5 * nm5)
    h2, w_down_b = pl.pallas_call(
        functools.partial(_ffn_up_kernel, tn=tn5, tiles_per_seq=seq // tm5),
        grid=(nn5, nm5),
        in_specs=[
            pl.BlockSpec((tm5, D_MODEL), lambda n, m: (m, 0)),
            pl.BlockSpec((D_MODEL, tn5), lambda n, m: (0, n)),
            pl.BlockSpec((D_MODEL, tn5), lambda n, m: (0, n + nn5)),
            pl.BlockSpec((3, tn5), lambda n, m: (0, n)),
            pl.BlockSpec((3, tn5), lambda n, m: (0, n + nn5)),
            pl.BlockSpec((wd_rows, D_MODEL), lambda n, m: (n * nm5 + m, 0)),
        ],
        out_specs=[pl.BlockSpec((tm5, tn5), lambda n, m: (m, n)),
                   pl.BlockSpec((wd_rows, D_MODEL), lambda n, m: (n * nm5 + m, 0))],
        out_shape=[jax.ShapeDtypeStruct((T, FFN_DIM), BF16),
                   jax.ShapeDtypeStruct((FFN_DIM, D_MODEL), BF16)],
        scratch_shapes=[pltpu.VMEM((D_MODEL, tn5), BF16), pltpu.VMEM((D_MODEL, tn5), BF16),
                        pltpu.VMEM((2 * HALO, tn5), F32), pltpu.VMEM((2 * HALO, tn5), F32)],
        compiler_params=_params(("arbitrary", "arbitrary"), 56),
        name="ffn_up",
    )(x1b, w_gate_up, w_gate_up, ffn_conv_w, ffn_conv_w, w_down)

    tm6 = 256
    out = pl.pallas_call(
        _ffn_down_kernel,
        grid=(T // tm6,),
        in_specs=[
            pl.BlockSpec((tm6, FFN_DIM), lambda m: (m, 0)),
            full((FFN_DIM, D_MODEL)),
            pl.BlockSpec((tm6, D_MODEL), lambda m: (m, 0)),
            full((1, D_MODEL)), full((1, D_MODEL)),
        ],
        out_specs=pl.BlockSpec((tm6, D_MODEL), lambda m: (m, 0)),
        out_shape=jax.ShapeDtypeStruct((T, D_MODEL), F32),
        compiler_params=_params(("arbitrary",), 56),
        name="ffn_down",
    )(h2, w_down_b, x1, ln2_g.reshape(1, D_MODEL), ln2_b.reshape(1, D_MODEL))
    return out


def kernel(x, positions, w_in, q_norm_g, w_uq, kv_norm_g, w_ukv, conv_w, w_out, ln1_g, ln1_b,
           w_gate_up, ffn_conv_w, w_down, ln2_g, ln2_b):
    batch, seq, _ = x.shape
    x2 = x.reshape(batch * seq, D_MODEL)
    for l in range(DEPTH):
        x2 = _layer(x2, positions, w_in[l], q_norm_g[l], w_uq[l], kv_norm_g[l], w_ukv[l], conv_w[l],
                    w_out[l], ln1_g[l], ln1_b[l], w_gate_up[l], ffn_conv_w[l], w_down[l], ln2_g[l], ln2_b[l],
                    batch, seq)
    return x2.reshape(batch, seq, D_MODEL)
```

```python
import functools
import math

import jax
import jax.numpy as jnp
from jax import lax
from jax.experimental import pallas as pl
from jax.experimental.pallas import tpu as pltpu

D_MODEL = 2048
MLA_HEADS = 8
NOPE = 128
ROPE = 64
V_DIM = 128
V_ROWS = V_DIM + 16
MLA_WIDTH = MLA_HEADS * V_DIM
Q_RANK = 512
KV_RANK = 256
CONV_WIDTH = 1024
FFN_DIM = 5632
ROPE_THETA = 10000.0
LN_EPS = 1e-5
RMS_EPS = 1e-6
DEPTH = 1
ALPHA = (2.0 * DEPTH) ** 0.25
QK_DIM = NOPE + ROPE
Q_SCALE = (QK_DIM ** -0.5) * math.log2(math.e)
QK_PAD = 256
HALO = 8
WCHUNK = 64
NEG = -1e30

F32 = jnp.float32
BF16 = jnp.bfloat16
MIB = 1024 * 1024


def _dot(a, b):
    return jnp.dot(a, b, preferred_element_type=F32)


def _dot_nt(a, b):
    return lax.dot_general(a, b, (((1,), (1,)), ((), ())), preferred_element_type=F32)


def _rms(x, g):
    return x * lax.rsqrt(jnp.mean(x * x, axis=-1, keepdims=True) + RMS_EPS) * g


def _layer_norm(y, g, b):
    mu = jnp.mean(y, axis=-1, keepdims=True)
    d = y - mu
    var = jnp.mean(d * d, axis=-1, keepdims=True)
    return d * lax.rsqrt(var + LN_EPS) * g + b


def _causal_conv3(z, hs_ref, cw, cols):
    n = z.shape[0]
    w0, w1, w2 = cw[0:1, :], cw[1:2, :], cw[2:3, :]
    y = w2 * z + w1 * pltpu.roll(z, 1, axis=0) + w0 * pltpu.roll(z, 2, axis=0)
    head = z[0:HALO]
    hs_ref[HALO:2 * HALO, cols] = head
    y_head = w2 * head + w1 * hs_ref[HALO - 1:2 * HALO - 1, cols] + w0 * hs_ref[HALO - 2:2 * HALO - 2, cols]
    hs_ref[0:HALO, cols] = z[n - HALO:n]
    return jnp.concatenate([y_head, y[HALO:]], axis=0)


def _latent_kernel(x_ref, posr_ref, invf_ref, win_ref, wc0_ref, wc1_ref, wc2_ref, gq_ref, gkv_ref,
                   wqt_ref, wk_ref, wvt_ref,
                   qt_ref, kn_ref, kr_ref, vt_ref, wconv_ref, wlat_ref, *, tm, tk):
    half = ROPE // 2
    latent = Q_RANK + KV_RANK

    @pl.when(pl.program_id(0) == 0)
    def _():
        wlat_ref[...] = win_ref[...].astype(BF16)

    for n, wc_ref in enumerate((wc0_ref, wc1_ref, wc2_ref)):
        wconv_ref[n * WCHUNK:(n + 1) * WCHUNK, :] = wc_ref[...].astype(BF16)

    xb = x_ref[...].astype(BF16)
    h = _dot_nt(xb, wlat_ref[...])
    cqn = _rms(h[:, :Q_RANK], gq_ref[...]).astype(BF16)
    ckvn = _rms(h[:, Q_RANK:latent], gkv_ref[...]).astype(BF16)

    ang_t = invf_ref[...] * posr_ref[...].astype(F32)
    cos_t, sin_t = jnp.cos(ang_t), jnp.sin(ang_t)
    table = jnp.concatenate([cos_t, cos_t, sin_t, sin_t], axis=0).T
    k_r = h[:, latent:latent + ROPE]
    kr_ref[0] = (jnp.concatenate([k_r, k_r], axis=1) * table).astype(BF16)

    kn = _dot(ckvn, wk_ref[...])
    vt = _dot_nt(wvt_ref[...], ckvn)
    qt = _dot_nt(wqt_ref[...], cqn)
    ones_row = (lax.broadcasted_iota(jnp.int32, (V_ROWS - V_DIM, tk), 0) == 0).astype(BF16)
    for hh in range(MLA_HEADS):
        kn_ref[0, hh] = kn[:, NOPE * hh:NOPE * hh + NOPE].astype(BF16)
        for c in range(tm // tk):
            vt_ref[0, hh, c, 0:V_DIM, :] = vt[V_DIM * hh:V_DIM * hh + V_DIM, tk * c:tk * c + tk].astype(BF16)
            vt_ref[0, hh, c, V_DIM:V_ROWS, :] = ones_row
        base = QK_DIM * hh
        r1 = qt[base + NOPE:base + NOPE + half]
        r2 = qt[base + NOPE + half:base + QK_DIM]
        qr1 = (r1 * cos_t - r2 * sin_t).astype(BF16)
        qr2 = (r2 * cos_t + r1 * sin_t).astype(BF16)
        qt_ref[0, hh, 0:NOPE, :] = qt[base:base + NOPE].astype(BF16)
        qt_ref[0, hh, NOPE:NOPE + half, :] = qr1
        qt_ref[0, hh, NOPE + half:NOPE + 2 * half, :] = qr2
        qt_ref[0, hh, NOPE + 2 * half:NOPE + 3 * half, :] = qr2
        qt_ref[0, hh, NOPE + 3 * half:QK_PAD, :] = -qr1


def _convmix_kernel(x_ref, w_ref, cw_ref, wo_ref, o_ref, wob_ref, zs_ref, *, tc, tiles_per_seq):
    @pl.when(pl.program_id(0) % tiles_per_seq == 0)
    def _():
        zs_ref[0:HALO, :] = jnp.zeros((HALO, CONV_WIDTH), F32)

    wob_ref[...] = wo_ref[...].astype(BF16)

    xb = x_ref[...].astype(BF16)
    for c in range(CONV_WIDTH // tc):
        cols = slice(c * tc, (c + 1) * tc)
        gate_b = _dot_nt(xb, w_ref[c * tc:(c + 1) * tc, :])
        gate_c = _dot_nt(xb, w_ref[CONV_WIDTH + c * tc:CONV_WIDTH + (c + 1) * tc, :])
        x_c = _dot_nt(xb, w_ref[2 * CONV_WIDTH + c * tc:2 * CONV_WIDTH + (c + 1) * tc, :])
        y = _causal_conv3(gate_c * x_c, zs_ref, cw_ref[:, cols], cols)
        o_ref[:, cols] = (gate_b * y).astype(BF16)


def _attn_kernel(qt_ref, kn_ref, kr_ref, vt_ref, o_ref, m_ref, acc_ref, *, tq, tk, heads, lookahead):
    qi = pl.program_id(1)
    per_q = tq // tk
    trip_blocks = 2 * per_q
    m_ref[...] = jnp.full(m_ref.shape, NEG, F32)
    acc_ref[...] = jnp.zeros(acc_ref.shape, F32)

    def run(chains):
        def scores(chain):
            g, j, c0, _ = chain
            start = pl.multiple_of(j * tk, tk)
            k = jnp.concatenate([kn_ref[0, g, pl.ds(start, tk), :], kr_ref[0, pl.ds(start, tk), :]], axis=1)
            return _dot(k, qt_ref[0, g, :, c0:])

        ahead = [scores(ch) for ch in chains[:lookahead]]
        for i, (g, j, c0, mask) in enumerate(chains):
            s = ahead.pop(0)
            if i + lookahead < len(chains):
                ahead.append(scores(chains[i + lookahead]))
            if mask is not None:
                s = jnp.where(mask, s, NEG)
            m = m_ref[g, :, c0:]
            m_new = jnp.maximum(m, jnp.max(s, axis=0, keepdims=True))
            p = jnp.exp2(s - m_new)
            a = jnp.exp2(m - m_new)
            m_ref[g, :, c0:] = m_new
            acc_ref[g, :, c0:] = a * acc_ref[g, :, c0:] + _dot(vt_ref[0, g, j], p.astype(BF16))

    def full_blocks(first, count):
        return [(g, first + b, 0, None) for b in range(count) for g in range(heads)]

    def below_diagonal(t, c):
        run(full_blocks(trip_blocks * t, trip_blocks))
        return c

    n_below = qi * per_q
    lax.fori_loop(0, n_below // trip_blocks, below_diagonal, 0)

    @pl.when(n_below % trip_blocks != 0)
    def _():
        run(full_blocks(n_below - per_q, per_q))

    diag = []
    for b in range(per_q):
        c0 = b * tk
        key = lax.broadcasted_iota(jnp.int32, (tk, tq - c0), 0)
        qry = lax.broadcasted_iota(jnp.int32, (tk, tq - c0), 1)
        diag += [(g, per_q * qi + b, c0, key <= qry) for g in range(heads)]
    run(diag)
    for g in range(heads):
        out_t = acc_ref[g, 0:V_DIM, :] / acc_ref[g, V_DIM:V_DIM + 1, :]
        o_ref[0, :, V_DIM * g:V_DIM * g + V_DIM] = out_t.T.astype(BF16)


def _outproj_kernel(a_ref, c_ref, x_ref, w_ref, g_ref, b_ref, o32_ref, o16_ref, *, tm, rc):
    for r in range(tm // rc):
        rows = slice(r * rc, (r + 1) * rc)
        mixed = jnp.concatenate([a_ref[rows, :], c_ref[rows, :]], axis=1)
        y = _dot(mixed, w_ref[...]) + ALPHA * x_ref[rows, :]
        out = _layer_norm(y, g_ref[...], b_ref[...])
        o32_ref[rows, :] = out
        o16_ref[rows, :] = out.astype(BF16)


def _ffn_up_kernel(x_ref, wg_ref, wu_ref, cwg_ref, cwu_ref, wd_ref, o_ref, wdb_ref,
                   wgb_ref, wub_ref, gs_ref, us_ref, *, tn, tiles_per_seq):
    mi = pl.program_id(1)

    @pl.when(mi == 0)
    def _():
        wgb_ref[...] = wg_ref[...].astype(BF16)
        wub_ref[...] = wu_ref[...].astype(BF16)

    @pl.when(mi % tiles_per_seq == 0)
    def _():
        gs_ref[0:HALO, :] = jnp.zeros((HALO, tn), F32)
        us_ref[0:HALO, :] = jnp.zeros((HALO, tn), F32)

    wdb_ref[...] = wd_ref[...].astype(BF16)

    xb = x_ref[...]
    cols = slice(0, tn)
    g = _causal_conv3(_dot(xb, wgb_ref[...]), gs_ref, cwg_ref[...], cols)
    u = _causal_conv3(_dot(xb, wub_ref[...]), us_ref, cwu_ref[...], cols)
    o_ref[...] = (g / (1.0 + jnp.exp(-g)) * u).astype(BF16)


def _ffn_down_kernel(h_ref, w_ref, x_ref, g_ref, b_ref, o_ref, *, tm, rc):
    for r in range(tm // rc):
        rows = slice(r * rc, (r + 1) * rc)
        y = _dot(h_ref[rows, :], w_ref[...]) + ALPHA * x_ref[rows, :]
        o_ref[rows, :] = _layer_norm(y, g_ref[...], b_ref[...])


def _params(sem, vmem_mib):
    return pltpu.CompilerParams(dimension_semantics=sem, vmem_limit_bytes=vmem_mib * MIB)


def _layer(x2, pos, w_in, q_norm_g, w_uq, kv_norm_g, w_ukv, conv_w, w_out, ln1_g, ln1_b,
           w_gate_up, ffn_conv_w, w_down, ln2_g, ln2_b, batch, seq):
    T = batch * seq

    half = ROPE // 2
    wqt = w_uq.T.astype(BF16)
    ukv = w_ukv.reshape(KV_RANK, MLA_HEADS, NOPE + V_DIM)
    wk = ukv[:, :, :NOPE].reshape(KV_RANK, MLA_HEADS * NOPE).astype(BF16)
    wvt = ukv[:, :, NOPE:].reshape(KV_RANK, MLA_HEADS * V_DIM).T.astype(BF16)
    gq = (q_norm_g * Q_SCALE).reshape(1, Q_RANK)
    gkv = kv_norm_g.reshape(1, KV_RANK)
    inv_freq = (ROPE_THETA ** (-jnp.arange(0, ROPE, 2, dtype=F32) / ROPE)).reshape(half, 1)
    posr = pos.reshape(1, T)

    tm = 512
    tk = 256
    nblk = seq // tm
    full = lambda shape: pl.BlockSpec(shape, lambda *_: (0,) * len(shape), pipeline_mode=pl.Buffered(1))
    n1 = T // tm
    w_in_t = w_in.T
    lat_rows = Q_RANK + KV_RANK + 128
    conv_row0 = Q_RANK + KV_RANK + ROPE
    assert conv_row0 % WCHUNK == 0 and 3 * CONV_WIDTH == 3 * WCHUNK * n1
    wchunk = lambda n: pl.BlockSpec((WCHUNK, D_MODEL), lambda i: (conv_row0 // WCHUNK + 3 * i + n, 0))
    qt, kn, kr, vt, wconv = pl.pallas_call(
        functools.partial(_latent_kernel, tm=tm, tk=tk),
        grid=(n1,),
        in_specs=[
            pl.BlockSpec((tm, D_MODEL), lambda i: (i, 0)),
            pl.BlockSpec((1, tm), lambda i: (0, i)),
            full((half, 1)),
            full((lat_rows, D_MODEL)),
            wchunk(0), wchunk(1), wchunk(2),
            full((1, Q_RANK)), full((1, KV_RANK)),
            full((MLA_HEADS * QK_DIM, Q_RANK)),
            full((KV_RANK, MLA_HEADS * NOPE)),
            full((MLA_HEADS * V_DIM, KV_RANK)),
        ],
        out_specs=[
            pl.BlockSpec((1, MLA_HEADS, QK_PAD, tm), lambda i: (i // nblk, 0, 0, i % nblk)),
            pl.BlockSpec((1, MLA_HEADS, tm, NOPE), lambda i: (i // nblk, 0, i % nblk, 0)),
            pl.BlockSpec((1, tm, 128), lambda i: (i // nblk, i % nblk, 0)),
            pl.BlockSpec((1, MLA_HEADS, tm // tk, V_ROWS, tk), lambda i: (i // nblk, 0, i % nblk, 0, 0)),
            pl.BlockSpec((3 * WCHUNK, D_MODEL), lambda i: (i, 0)),
        ],
        out_shape=[
            jax.ShapeDtypeStruct((batch, MLA_HEADS, QK_PAD, seq), BF16),
            jax.ShapeDtypeStruct((batch, MLA_HEADS, seq, NOPE), BF16),
            jax.ShapeDtypeStruct((batch, seq, 128), BF16),
            jax.ShapeDtypeStruct((batch, MLA_HEADS, seq // tk, V_ROWS, tk), BF16),
            jax.ShapeDtypeStruct((3 * CONV_WIDTH, D_MODEL), BF16),
        ],
        scratch_shapes=[pltpu.VMEM((lat_rows, D_MODEL), BF16)],
        compiler_params=_params(("arbitrary",), 48),
        name="latent",
    )(x2, posr, inv_freq, w_in_t, w_in_t, w_in_t, w_in_t, gq, gkv, wqt, wk, wvt)

    tm2 = 512
    n2 = T // tm2
    wo_rows = D_MODEL // n2
    conv_out, w_out_b = pl.pallas_call(
        functools.partial(_convmix_kernel, tc=256, tiles_per_seq=seq // tm2),
        grid=(n2,),
        in_specs=[
            pl.BlockSpec((tm2, D_MODEL), lambda i: (i, 0)),
            full((3 * CONV_WIDTH, D_MODEL)),
            full((3, CONV_WIDTH)),
            pl.BlockSpec((wo_rows, D_MODEL), lambda i: (i, 0)),
        ],
        out_specs=[pl.BlockSpec((tm2, CONV_WIDTH), lambda i: (i, 0)),
                   pl.BlockSpec((wo_rows, D_MODEL), lambda i: (i, 0))],
        out_shape=[jax.ShapeDtypeStruct((T, CONV_WIDTH), BF16),
                   jax.ShapeDtypeStruct((D_MODEL, D_MODEL), BF16)],
        scratch_shapes=[pltpu.VMEM((2 * HALO, CONV_WIDTH), F32)],
        compiler_params=_params(("arbitrary",), 48),
        name="convmix",
    )(x2, wconv, conv_w, w_out)

    tq = 512
    attn = pl.pallas_call(
        functools.partial(_attn_kernel, tq=tq, tk=tk, heads=MLA_HEADS, lookahead=2),
        grid=(batch, seq // tq),
        in_specs=[
            pl.BlockSpec((1, MLA_HEADS, QK_PAD, tq), lambda b, i: (b, 0, 0, i)),
            pl.BlockSpec((1, MLA_HEADS, seq, NOPE), lambda b, i: (b, 0, 0, 0)),
            pl.BlockSpec((1, seq, 128), lambda b, i: (b, 0, 0)),
            pl.BlockSpec((1, MLA_HEADS, seq // tk, V_ROWS, tk), lambda b, i: (b, 0, 0, 0, 0)),
        ],
        out_specs=pl.BlockSpec((1, tq, MLA_WIDTH), lambda b, i: (b, i, 0)),
        out_shape=jax.ShapeDtypeStruct((batch, seq, MLA_WIDTH), BF16),
        scratch_shapes=[pltpu.VMEM((MLA_HEADS, 1, tq), F32), pltpu.VMEM((MLA_HEADS, V_ROWS, tq), F32)],
        compiler_params=_params(("arbitrary", "arbitrary"), 56),
        name="attn",
    )(qt, kn, kr, vt)
    attn = attn.reshape(T, MLA_WIDTH)

    tm4 = 512
    x1, x1b = pl.pallas_call(
        functools.partial(_outproj_kernel, tm=tm4, rc=128),
        grid=(T // tm4,),
        in_specs=[
            pl.BlockSpec((tm4, MLA_WIDTH), lambda i: (i, 0)),
            pl.BlockSpec((tm4, CONV_WIDTH), lambda i: (i, 0)),
            pl.BlockSpec((tm4, D_MODEL), lambda i: (i, 0)),
            full((D_MODEL, D_MODEL)), full((1, D_MODEL)), full((1, D_MODEL)),
        ],
        out_specs=[pl.BlockSpec((tm4, D_MODEL), lambda i: (i, 0)), pl.BlockSpec((tm4, D_MODEL), lambda i: (i, 0))],
        out_shape=[jax.ShapeDtypeStruct((T, D_MODEL), F32), jax.ShapeDtypeStruct((T, D_MODEL), BF16)],
        compiler_params=_params(("arbitrary",), 56),
        name="outproj",
    )(attn, conv_out, x2, w_out_b, ln1_g.reshape(1, D_MODEL), ln1_b.reshape(1, D_MODEL))

    tm5, tn5 = 1024, 512
    nn5 = FFN_DIM // tn5
    nm5 = T // tm5
    wd_rows = FFN_DIM // (nn5 * nm5)
    h2, w_down_b = pl.pallas_call(
        functools.partial(_ffn_up_kernel, tn=tn5, tiles_per_seq=seq // tm5),
        grid=(nn5, nm5),
        in_specs=[
            pl.BlockSpec((tm5, D_MODEL), lambda n, m: (m, 0)),
            pl.BlockSpec((D_MODEL, tn5), lambda n, m: (0, n)),
            pl.BlockSpec((D_MODEL, tn5), lambda n, m: (0, n + nn5)),
            pl.BlockSpec((3, tn5), lambda n, m: (0, n)),
            pl.BlockSpec((3, tn5), lambda n, m: (0, n + nn5)),
            pl.BlockSpec((wd_rows, D_MODEL), lambda n, m: (n * nm5 + m, 0)),
        ],
        out_specs=[pl.BlockSpec((tm5, tn5), lambda n, m: (m, n)),
                   pl.BlockSpec((wd_rows, D_MODEL), lambda n, m: (n * nm5 + m, 0))],
        out_shape=[jax.ShapeDtypeStruct((T, FFN_DIM), BF16),
                   jax.ShapeDtypeStruct((FFN_DIM, D_MODEL), BF16)],
        scratch_shapes=[pltpu.VMEM((D_MODEL, tn5), BF16), pltpu.VMEM((D_MODEL, tn5), BF16),
                        pltpu.VMEM((2 * HALO, tn5), F32), pltpu.VMEM((2 * HALO, tn5), F32)],
        compiler_params=_params(("arbitrary", "arbitrary"), 56),
        name="ffn_up",
    )(x1b, w_gate_up, w_gate_up, ffn_conv_w, ffn_conv_w, w_down)

    tm6 = 512
    out = pl.pallas_call(
        functools.partial(_ffn_down_kernel, tm=tm6, rc=128),
        grid=(T // tm6,),
        in_specs=[
            pl.BlockSpec((tm6, FFN_DIM), lambda m: (m, 0)),
            full((FFN_DIM, D_MODEL)),
            pl.BlockSpec((tm6, D_MODEL), lambda m: (m, 0)),
            full((1, D_MODEL)), full((1, D_MODEL)),
        ],
        out_specs=pl.BlockSpec((tm6, D_MODEL), lambda m: (m, 0)),
        out_shape=jax.ShapeDtypeStruct((T, D_MODEL), F32),
        compiler_params=_params(("arbitrary",), 56),
        name="ffn_down",
    )(h2, w_down_b, x1, ln2_g.reshape(1, D_MODEL), ln2_b.reshape(1, D_MODEL))
    return out


def kernel(x, positions, w_in, q_norm_g, w_uq, kv_norm_g, w_ukv, conv_w, w_out, ln1_g, ln1_b,
           w_gate_up, ffn_conv_w, w_down, ln2_g, ln2_b):
    batch, seq, _ = x.shape
    x2 = x.reshape(batch * seq, D_MODEL)
    for l in range(DEPTH):
        x2 = _layer(x2, positions, w_in[l], q_norm_g[l], w_uq[l], kv_norm_g[l], w_ukv[l], conv_w[l],
                    w_out[l], ln1_g[l], ln1_b[l], w_gate_up[l], ffn_conv_w[l], w_down[l], ln2_g[l], ln2_b[l],
                    batch, seq)
    return x2.reshape(batch, seq, D_MODEL)
```

```python
import functools
import math

import jax
import jax.numpy as jnp
from jax import lax
from jax.experimental import pallas as pl
from jax.experimental.pallas import tpu as pltpu

LANES = 128
F32_SUBLANES = 8
BF16_SUBLANES = 2 * F32_SUBLANES
V7X_VMEM_BYTES = 64 * 1024 * 1024
VMEM_LIMIT = V7X_VMEM_BYTES - 8 * 1024 * 1024

D_MODEL = 2048
MLA_HEADS = 8
NOPE = 128
ROPE = 64
V_DIM = 128
V_ROWS = V_DIM + BF16_SUBLANES
MLA_WIDTH = MLA_HEADS * V_DIM
Q_RANK = 512
KV_RANK = 256
CONV_WIDTH = 1024
FFN_DIM = 5632
ROPE_THETA = 10000.0
LN_EPS = 1e-5
RMS_EPS = 1e-6
DEPTH = 1
ALPHA = (2.0 * DEPTH) ** 0.25
QK_DIM = NOPE + ROPE
Q_SCALE = (QK_DIM ** -0.5) * math.log2(math.e)
ROPE_LANES = 2 * ROPE
QK_PAD = NOPE + ROPE_LANES
LAT_ROWS = Q_RANK + KV_RANK + LANES
HALO = F32_SUBLANES
WCHUNK = 64
NEG = -1e30

TM_LATENT = 512
TK_ATTN = 256
TQ_ATTN = 512
ATTN_LOOKAHEAD = 2
TM_CONVMIX, TC_CONVMIX = 512, 256
TM_OUTPROJ, RC_OUTPROJ = 512, 128
TM_FFN_UP, TN_FFN_UP = 1024, 512
TM_FFN_DOWN, RC_FFN_DOWN = 512, 128

F32 = jnp.float32
BF16 = jnp.bfloat16


def _dot(a, b):
    return jnp.dot(a, b, preferred_element_type=F32)


def _dot_nt(a, b):
    return lax.dot_general(a, b, (((1,), (1,)), ((), ())), preferred_element_type=F32)


def _rms(x, g):
    return x * lax.rsqrt(jnp.mean(x * x, axis=-1, keepdims=True) + RMS_EPS) * g


def _layer_norm(y, g, b):
    mu = jnp.mean(y, axis=-1, keepdims=True)
    d = y - mu
    var = jnp.mean(d * d, axis=-1, keepdims=True)
    return d * lax.rsqrt(var + LN_EPS) * g + b


def _causal_conv3(z, hs_ref, cw, cols):
    n = z.shape[0]
    w0, w1, w2 = cw[0:1, :], cw[1:2, :], cw[2:3, :]
    y = w2 * z + w1 * pltpu.roll(z, 1, axis=0) + w0 * pltpu.roll(z, 2, axis=0)
    head = z[0:HALO]
    hs_ref[HALO:2 * HALO, cols] = head
    y_head = w2 * head + w1 * hs_ref[HALO - 1:2 * HALO - 1, cols] + w0 * hs_ref[HALO - 2:2 * HALO - 2, cols]
    hs_ref[0:HALO, cols] = z[n - HALO:n]
    return jnp.concatenate([y_head, y[HALO:]], axis=0)


def _latent_kernel(x_ref, posr_ref, invf_ref, win_ref, wc0_ref, wc1_ref, wc2_ref, gq_ref, gkv_ref,
                   wqt_ref, wk_ref, wvt_ref,
                   qt_ref, kn_ref, kr_ref, vt_ref, wconv_ref, wlat_ref, *, tm, tk):
    half = ROPE // 2
    latent = Q_RANK + KV_RANK

    @pl.when(pl.program_id(0) == 0)
    def _():
        wlat_ref[...] = win_ref[...].astype(BF16)

    for n, wc_ref in enumerate((wc0_ref, wc1_ref, wc2_ref)):
        wconv_ref[n * WCHUNK:(n + 1) * WCHUNK, :] = wc_ref[...].astype(BF16)

    xb = x_ref[...].astype(BF16)
    h = _dot_nt(xb, wlat_ref[...])
    cqn = _rms(h[:, :Q_RANK], gq_ref[...]).astype(BF16)
    ckvn = _rms(h[:, Q_RANK:latent], gkv_ref[...]).astype(BF16)

    ang_t = invf_ref[...] * posr_ref[...].astype(F32)
    cos_t, sin_t = jnp.cos(ang_t), jnp.sin(ang_t)
    table = jnp.concatenate([cos_t, cos_t, sin_t, sin_t], axis=0).T
    k_r = h[:, latent:latent + ROPE]
    kr_ref[0] = (jnp.concatenate([k_r, k_r], axis=1) * table).astype(BF16)

    kn = _dot(ckvn, wk_ref[...])
    vt = _dot_nt(wvt_ref[...], ckvn)
    qt = _dot_nt(wqt_ref[...], cqn)
    ones_row = (lax.broadcasted_iota(jnp.int32, (V_ROWS - V_DIM, tk), 0) == 0).astype(BF16)
    for hh in range(MLA_HEADS):
        kn_ref[0, hh] = kn[:, NOPE * hh:NOPE * hh + NOPE].astype(BF16)
        for c in range(tm // tk):
            vt_ref[0, hh, c, 0:V_DIM, :] = vt[V_DIM * hh:V_DIM * hh + V_DIM, tk * c:tk * c + tk].astype(BF16)
            vt_ref[0, hh, c, V_DIM:V_ROWS, :] = ones_row
        base = QK_DIM * hh
        r1 = qt[base + NOPE:base + NOPE + half]
        r2 = qt[base + NOPE + half:base + QK_DIM]
        qr1 = (r1 * cos_t - r2 * sin_t).astype(BF16)
        qr2 = (r2 * cos_t + r1 * sin_t).astype(BF16)
        qt_ref[0, hh, 0:NOPE, :] = qt[base:base + NOPE].astype(BF16)
        qt_ref[0, hh, NOPE:NOPE + half, :] = qr1
        qt_ref[0, hh, NOPE + half:NOPE + 2 * half, :] = qr2
        qt_ref[0, hh, NOPE + 2 * half:NOPE + 3 * half, :] = qr2
        qt_ref[0, hh, NOPE + 3 * half:QK_PAD, :] = -qr1


def _convmix_kernel(x_ref, w_ref, cw_ref, wo_ref, o_ref, wob_ref, zs_ref, *, tc, tiles_per_seq):
    @pl.when(pl.program_id(0) % tiles_per_seq == 0)
    def _():
        zs_ref[0:HALO, :] = jnp.zeros((HALO, CONV_WIDTH), F32)

    wob_ref[...] = wo_ref[...].astype(BF16)

    xb = x_ref[...].astype(BF16)
    for c in range(CONV_WIDTH // tc):
        cols = slice(c * tc, (c + 1) * tc)
        gate_b = _dot_nt(xb, w_ref[c * tc:(c + 1) * tc, :])
        gate_c = _dot_nt(xb, w_ref[CONV_WIDTH + c * tc:CONV_WIDTH + (c + 1) * tc, :])
        x_c = _dot_nt(xb, w_ref[2 * CONV_WIDTH + c * tc:2 * CONV_WIDTH + (c + 1) * tc, :])
        y = _causal_conv3(gate_c * x_c, zs_ref, cw_ref[:, cols], cols)
        o_ref[:, cols] = (gate_b * y).astype(BF16)


def _attn_kernel(qt_ref, kn_ref, kr_ref, vt_ref, o_ref, m_ref, acc_ref, *, tq, tk, heads, lookahead):
    qi = pl.program_id(1)
    per_q = tq // tk
    trip_blocks = 2 * per_q
    m_ref[...] = jnp.full(m_ref.shape, NEG, F32)
    acc_ref[...] = jnp.zeros(acc_ref.shape, F32)

    def run(chains):
        def scores(chain):
            g, j, c0, _ = chain
            start = pl.multiple_of(j * tk, tk)
            k = jnp.concatenate([kn_ref[0, g, pl.ds(start, tk), :], kr_ref[0, pl.ds(start, tk), :]], axis=1)
            return _dot(k, qt_ref[0, g, :, c0:])

        ahead = [scores(ch) for ch in chains[:lookahead]]
        for i, (g, j, c0, mask) in enumerate(chains):
            s = ahead.pop(0)
            if i + lookahead < len(chains):
                ahead.append(scores(chains[i + lookahead]))
            if mask is not None:
                s = jnp.where(mask, s, NEG)
            m = m_ref[g, :, c0:]
            m_new = jnp.maximum(m, jnp.max(s, axis=0, keepdims=True))
            p = jnp.exp2(s - m_new)
            a = jnp.exp2(m - m_new)
            m_ref[g, :, c0:] = m_new
            acc_ref[g, :, c0:] = a * acc_ref[g, :, c0:] + _dot(vt_ref[0, g, j], p.astype(BF16))

    def full_blocks(first, count):
        return [(g, first + b, 0, None) for b in range(count) for g in range(heads)]

    def below_diagonal(t, c):
        run(full_blocks(trip_blocks * t, trip_blocks))
        return c

    n_below = qi * per_q
    lax.fori_loop(0, n_below // trip_blocks, below_diagonal, 0)

    @pl.when(n_below % trip_blocks != 0)
    def _():
        run(full_blocks(n_below - per_q, per_q))

    diag = []
    for b in range(per_q):
        c0 = b * tk
        key = lax.broadcasted_iota(jnp.int32, (tk, tq - c0), 0)
        qry = lax.broadcasted_iota(jnp.int32, (tk, tq - c0), 1)
        diag += [(g, per_q * qi + b, c0, key <= qry) for g in range(heads)]
    run(diag)
    for g in range(heads):
        out_t = acc_ref[g, 0:V_DIM, :] / acc_ref[g, V_DIM:V_DIM + 1, :]
        o_ref[0, :, V_DIM * g:V_DIM * g + V_DIM] = out_t.T.astype(BF16)


def _outproj_kernel(a_ref, c_ref, x_ref, w_ref, g_ref, b_ref, o32_ref, o16_ref, *, tm, rc):
    for r in range(tm // rc):
        rows = slice(r * rc, (r + 1) * rc)
        mixed = jnp.concatenate([a_ref[rows, :], c_ref[rows, :]], axis=1)
        y = _dot(mixed, w_ref[...]) + ALPHA * x_ref[rows, :]
        out = _layer_norm(y, g_ref[...], b_ref[...])
        o32_ref[rows, :] = out
        o16_ref[rows, :] = out.astype(BF16)


def _ffn_up_kernel(x_ref, wg_ref, wu_ref, cwg_ref, cwu_ref, wd_ref, o_ref, wdb_ref,
                   wgb_ref, wub_ref, gs_ref, us_ref, *, tn, tiles_per_seq):
    mi = pl.program_id(1)

    @pl.when(mi == 0)
    def _():
        wgb_ref[...] = wg_ref[...].astype(BF16)
        wub_ref[...] = wu_ref[...].astype(BF16)

    @pl.when(mi % tiles_per_seq == 0)
    def _():
        gs_ref[0:HALO, :] = jnp.zeros((HALO, tn), F32)
        us_ref[0:HALO, :] = jnp.zeros((HALO, tn), F32)

    wdb_ref[...] = wd_ref[...].astype(BF16)

    xb = x_ref[...]
    cols = slice(0, tn)
    g = _causal_conv3(_dot(xb, wgb_ref[...]), gs_ref, cwg_ref[...], cols)
    u = _causal_conv3(_dot(xb, wub_ref[...]), us_ref, cwu_ref[...], cols)
    o_ref[...] = (g / (1.0 + jnp.exp(-g)) * u).astype(BF16)


def _ffn_down_kernel(h_ref, w_ref, x_ref, g_ref, b_ref, o_ref, *, tm, rc):
    for r in range(tm // rc):
        rows = slice(r * rc, (r + 1) * rc)
        y = _dot(h_ref[rows, :], w_ref[...]) + ALPHA * x_ref[rows, :]
        o_ref[rows, :] = _layer_norm(y, g_ref[...], b_ref[...])


def _params(n_grid_axes):
    return pltpu.CompilerParams(dimension_semantics=("arbitrary",) * n_grid_axes, vmem_limit_bytes=VMEM_LIMIT)


def _layer(x2, pos, w_in, q_norm_g, w_uq, kv_norm_g, w_ukv, conv_w, w_out, ln1_g, ln1_b,
           w_gate_up, ffn_conv_w, w_down, ln2_g, ln2_b, batch, seq):
    T = batch * seq
    for tile in (TM_LATENT, TQ_ATTN, TM_CONVMIX, TM_OUTPROJ, TM_FFN_UP, TM_FFN_DOWN):
        assert seq % tile == 0, (seq, tile)
    assert TM_LATENT % TK_ATTN == 0 and TQ_ATTN % TK_ATTN == 0
    assert x2.shape == (T, D_MODEL) and w_in.shape == (D_MODEL, LAT_ROWS - LANES + ROPE + 3 * CONV_WIDTH)

    half = ROPE // 2
    wqt = w_uq.T.astype(BF16)
    ukv = w_ukv.reshape(KV_RANK, MLA_HEADS, NOPE + V_DIM)
    wk = ukv[:, :, :NOPE].reshape(KV_RANK, MLA_HEADS * NOPE).astype(BF16)
    wvt = ukv[:, :, NOPE:].reshape(KV_RANK, MLA_HEADS * V_DIM).T.astype(BF16)
    gq = (q_norm_g * Q_SCALE).reshape(1, Q_RANK)
    gkv = kv_norm_g.reshape(1, KV_RANK)
    inv_freq = (ROPE_THETA ** (-jnp.arange(0, ROPE, 2, dtype=F32) / ROPE)).reshape(half, 1)
    posr = pos.reshape(1, T)

    tm, tk = TM_LATENT, TK_ATTN
    nblk = seq // tm
    full = lambda shape: pl.BlockSpec(shape, lambda *_: (0,) * len(shape), pipeline_mode=pl.Buffered(1))
    n1 = T // tm
    w_in_t = w_in.T
    conv_row0 = Q_RANK + KV_RANK + ROPE
    assert conv_row0 % WCHUNK == 0 and 3 * CONV_WIDTH == 3 * WCHUNK * n1
    wchunk = lambda n: pl.BlockSpec((WCHUNK, D_MODEL), lambda i: (conv_row0 // WCHUNK + 3 * i + n, 0))
    qt, kn, kr, vt, wconv = pl.pallas_call(
        functools.partial(_latent_kernel, tm=tm, tk=tk),
        grid=(n1,),
        in_specs=[
            pl.BlockSpec((tm, D_MODEL), lambda i: (i, 0)),
            pl.BlockSpec((1, tm), lambda i: (0, i)),
            full((half, 1)),
            full((LAT_ROWS, D_MODEL)),
            wchunk(0), wchunk(1), wchunk(2),
            full((1, Q_RANK)), full((1, KV_RANK)),
            full((MLA_HEADS * QK_DIM, Q_RANK)),
            full((KV_RANK, MLA_HEADS * NOPE)),
            full((MLA_HEADS * V_DIM, KV_RANK)),
        ],
        out_specs=[
            pl.BlockSpec((1, MLA_HEADS, QK_PAD, tm), lambda i: (i // nblk, 0, 0, i % nblk)),
            pl.BlockSpec((1, MLA_HEADS, tm, NOPE), lambda i: (i // nblk, 0, i % nblk, 0)),
            pl.BlockSpec((1, tm, ROPE_LANES), lambda i: (i // nblk, i % nblk, 0)),
            pl.BlockSpec((1, MLA_HEADS, tm // tk, V_ROWS, tk), lambda i: (i // nblk, 0, i % nblk, 0, 0)),
            pl.BlockSpec((3 * WCHUNK, D_MODEL), lambda i: (i, 0)),
        ],
        out_shape=[
            jax.ShapeDtypeStruct((batch, MLA_HEADS, QK_PAD, seq), BF16),
            jax.ShapeDtypeStruct((batch, MLA_HEADS, seq, NOPE), BF16),
            jax.ShapeDtypeStruct((batch, seq, ROPE_LANES), BF16),
            jax.ShapeDtypeStruct((batch, MLA_HEADS, seq // tk, V_ROWS, tk), BF16),
            jax.ShapeDtypeStruct((3 * CONV_WIDTH, D_MODEL), BF16),
        ],
        scratch_shapes=[pltpu.VMEM((LAT_ROWS, D_MODEL), BF16)],
        compiler_params=_params(1),
        name="latent",
    )(x2, posr, inv_freq, w_in_t, w_in_t, w_in_t, w_in_t, gq, gkv, wqt, wk, wvt)

    tm2 = TM_CONVMIX
    n2 = T // tm2
    wo_rows = D_MODEL // n2
    conv_out, w_out_b = pl.pallas_call(
        functools.partial(_convmix_kernel, tc=TC_CONVMIX, tiles_per_seq=seq // tm2),
        grid=(n2,),
        in_specs=[
            pl.BlockSpec((tm2, D_MODEL), lambda i: (i, 0)),
            full((3 * CONV_WIDTH, D_MODEL)),
            full((3, CONV_WIDTH)),
            pl.BlockSpec((wo_rows, D_MODEL), lambda i: (i, 0)),
        ],
        out_specs=[pl.BlockSpec((tm2, CONV_WIDTH), lambda i: (i, 0)),
                   pl.BlockSpec((wo_rows, D_MODEL), lambda i: (i, 0))],
        out_shape=[jax.ShapeDtypeStruct((T, CONV_WIDTH), BF16),
                   jax.ShapeDtypeStruct((D_MODEL, D_MODEL), BF16)],
        scratch_shapes=[pltpu.VMEM((2 * HALO, CONV_WIDTH), F32)],
        compiler_params=_params(1),
        name="convmix",
    )(x2, wconv, conv_w, w_out)

    tq = TQ_ATTN
    attn = pl.pallas_call(
        functools.partial(_attn_kernel, tq=tq, tk=tk, heads=MLA_HEADS, lookahead=ATTN_LOOKAHEAD),
        grid=(batch, seq // tq),
        in_specs=[
            pl.BlockSpec((1, MLA_HEADS, QK_PAD, tq), lambda b, i: (b, 0, 0, i)),
            pl.BlockSpec((1, MLA_HEADS, seq, NOPE), lambda b, i: (b, 0, 0, 0)),
            pl.BlockSpec((1, seq, ROPE_LANES), lambda b, i: (b, 0, 0)),
            pl.BlockSpec((1, MLA_HEADS, seq // tk, V_ROWS, tk), lambda b, i: (b, 0, 0, 0, 0)),
        ],
        out_specs=pl.BlockSpec((1, tq, MLA_WIDTH), lambda b, i: (b, i, 0)),
        out_shape=jax.ShapeDtypeStruct((batch, seq, MLA_WIDTH), BF16),
        scratch_shapes=[pltpu.VMEM((MLA_HEADS, 1, tq), F32), pltpu.VMEM((MLA_HEADS, V_ROWS, tq), F32)],
        compiler_params=_params(2),
        name="attn",
    )(qt, kn, kr, vt)
    attn = attn.reshape(T, MLA_WIDTH)

    tm4 = TM_OUTPROJ
    x1, x1b = pl.pallas_call(
        functools.partial(_outproj_kernel, tm=tm4, rc=RC_OUTPROJ),
        grid=(T // tm4,),
        in_specs=[
            pl.BlockSpec((tm4, MLA_WIDTH), lambda i: (i, 0)),
            pl.BlockSpec((tm4, CONV_WIDTH), lambda i: (i, 0)),
            pl.BlockSpec((tm4, D_MODEL), lambda i: (i, 0)),
            full((D_MODEL, D_MODEL)), full((1, D_MODEL)), full((1, D_MODEL)),
        ],
        out_specs=[pl.BlockSpec((tm4, D_MODEL), lambda i: (i, 0)), pl.BlockSpec((tm4, D_MODEL), lambda i: (i, 0))],
        out_shape=[jax.ShapeDtypeStruct((T, D_MODEL), F32), jax.ShapeDtypeStruct((T, D_MODEL), BF16)],
        compiler_params=_params(1),
        name="outproj",
    )(attn, conv_out, x2, w_out_b, ln1_g.reshape(1, D_MODEL), ln1_b.reshape(1, D_MODEL))

    tm5, tn5 = TM_FFN_UP, TN_FFN_UP
    nn5 = FFN_DIM // tn5
    nm5 = T // tm5
    wd_rows = FFN_DIM // (nn5 * nm5)
    h2, w_down_b = pl.pallas_call(
        functools.partial(_ffn_up_kernel, tn=tn5, tiles_per_seq=seq // tm5),
        grid=(nn5, nm5),
        in_specs=[
            pl.BlockSpec((tm5, D_MODEL), lambda n, m: (m, 0)),
            pl.BlockSpec((D_MODEL, tn5), lambda n, m: (0, n)),
            pl.BlockSpec((D_MODEL, tn5), lambda n, m: (0, n + nn5)),
            pl.BlockSpec((3, tn5), lambda n, m: (0, n)),
            pl.BlockSpec((3, tn5), lambda n, m: (0, n + nn5)),
            pl.BlockSpec((wd_rows, D_MODEL), lambda n, m: (n * nm5 + m, 0)),
        ],
        out_specs=[pl.BlockSpec((tm5, tn5), lambda n, m: (m, n)),
                   pl.BlockSpec((wd_rows, D_MODEL), lambda n, m: (n * nm5 + m, 0))],
        out_shape=[jax.ShapeDtypeStruct((T, FFN_DIM), BF16),
                   jax.ShapeDtypeStruct((FFN_DIM, D_MODEL), BF16)],
        scratch_shapes=[pltpu.VMEM((D_MODEL, tn5), BF16), pltpu.VMEM((D_MODEL, tn5), BF16),
                        pltpu.VMEM((2 * HALO, tn5), F32), pltpu.VMEM((2 * HALO, tn5), F32)],
        compiler_params=_params(2),
        name="ffn_up",
    )(x1b, w_gate_up, w_gate_up, ffn_conv_w, ffn_conv_w, w_down)

    tm6 = TM_FFN_DOWN
    out = pl.pallas_call(
        functools.partial(_ffn_down_kernel, tm=tm6, rc=RC_FFN_DOWN),
        grid=(T // tm6,),
        in_specs=[
            pl.BlockSpec((tm6, FFN_DIM), lambda m: (m, 0)),
            full((FFN_DIM, D_MODEL)),
            pl.BlockSpec((tm6, D_MODEL), lambda m: (m, 0)),
            full((1, D_MODEL)), full((1, D_MODEL)),
        ],
        out_specs=pl.BlockSpec((tm6, D_MODEL), lambda m: (m, 0)),
        out_shape=jax.ShapeDtypeStruct((T, D_MODEL), F32),
        compiler_params=_params(1),
        name="ffn_down",
    )(h2, w_down_b, x1, ln2_g.reshape(1, D_MODEL), ln2_b.reshape(1, D_MODEL))
    return out


def kernel(x, positions, w_in, q_norm_g, w_uq, kv_norm_g, w_ukv, conv_w, w_out, ln1_g, ln1_b,
           w_gate_up, ffn_conv_w, w_down, ln2_g, ln2_b):
    batch, seq, _ = x.shape
    x2 = x.reshape(batch * seq, D_MODEL)
    for l in range(DEPTH):
        x2 = _layer(x2, positions, w_in[l], q_norm_g[l], w_uq[l], kv_norm_g[l], w_ukv[l], conv_w[l],
                    w_out[l], ln1_g[l], ln1_b[l], w_gate_up[l], ffn_conv_w[l], w_down[l], ln2_g[l], ln2_b[l],
                    batch, seq)
    return x2.reshape(batch, seq, D_MODEL)
```

```python
import functools
import math

import jax
import jax.numpy as jnp
from jax import lax
from jax.experimental import pallas as pl
from jax.experimental.pallas import tpu as pltpu

LANES = 128
F32_SUBLANES = 8
BF16_SUBLANES = 2 * F32_SUBLANES
V7X_VMEM_BYTES = 64 * 1024 * 1024
VMEM_LIMIT = V7X_VMEM_BYTES - 8 * 1024 * 1024

D_MODEL = 2048
MLA_HEADS = 8
NOPE = 128
ROPE = 64
V_DIM = 128
V_ROWS = V_DIM + BF16_SUBLANES
MLA_WIDTH = MLA_HEADS * V_DIM
Q_RANK = 512
KV_RANK = 256
CONV_WIDTH = 1024
FFN_DIM = 5632
ROPE_THETA = 10000.0
LN_EPS = 1e-5
RMS_EPS = 1e-6
DEPTH = 1
ALPHA = (2.0 * DEPTH) ** 0.25
QK_DIM = NOPE + ROPE
Q_SCALE = (QK_DIM ** -0.5) * math.log2(math.e)
ROPE_LANES = 2 * ROPE
QK_PAD = NOPE + ROPE_LANES
LAT_ROWS = Q_RANK + KV_RANK + LANES
HALO = F32_SUBLANES
WCHUNK = 64
NEG = -1e30

TM_LATENT = 512
TK_ATTN = 256
TQ_ATTN = 512
ATTN_LOOKAHEAD = 2
TM_CONVMIX, TC_CONVMIX = 1024, 256
TM_OUTPROJ, RC_OUTPROJ = 512, 128
TM_FFN_UP, TN_FFN_UP = 1024, 512
TM_FFN_DOWN, RC_FFN_DOWN = 512, 128

F32 = jnp.float32
BF16 = jnp.bfloat16


def _dot(a, b):
    return jnp.dot(a, b, preferred_element_type=F32)


def _dot_nt(a, b):
    return lax.dot_general(a, b, (((1,), (1,)), ((), ())), preferred_element_type=F32)


def _rms(x, g):
    return x * lax.rsqrt(jnp.mean(x * x, axis=-1, keepdims=True) + RMS_EPS) * g


def _layer_norm(y, g, b):
    mu = jnp.mean(y, axis=-1, keepdims=True)
    d = y - mu
    var = jnp.mean(d * d, axis=-1, keepdims=True)
    return d * lax.rsqrt(var + LN_EPS) * g + b


def _causal_conv3(z, hs_ref, cw, cols):
    n = z.shape[0]
    w0, w1, w2 = cw[0:1, :], cw[1:2, :], cw[2:3, :]
    y = w2 * z + w1 * pltpu.roll(z, 1, axis=0) + w0 * pltpu.roll(z, 2, axis=0)
    head = z[0:HALO]
    hs_ref[HALO:2 * HALO, cols] = head
    y_head = w2 * head + w1 * hs_ref[HALO - 1:2 * HALO - 1, cols] + w0 * hs_ref[HALO - 2:2 * HALO - 2, cols]
    hs_ref[0:HALO, cols] = z[n - HALO:n]
    return jnp.concatenate([y_head, y[HALO:]], axis=0)


def _latent_kernel(x_ref, posr_ref, invf_ref, win_ref, gq_ref, gkv_ref, wqt_ref, wk_ref, wvt_ref, *refs,
                   tm, tk, n_wc):
    wc_refs = refs[:n_wc]
    qt_ref, kn_ref, kr_ref, vt_ref, wconv_ref, wlat_ref = refs[n_wc:]
    half = ROPE // 2
    latent = Q_RANK + KV_RANK

    @pl.when(pl.program_id(0) == 0)
    def _():
        wlat_ref[...] = win_ref[...].astype(BF16)

    for n, wc_ref in enumerate(wc_refs):
        wconv_ref[n * WCHUNK:(n + 1) * WCHUNK, :] = wc_ref[...].astype(BF16)

    xb = x_ref[...].astype(BF16)
    h = _dot_nt(xb, wlat_ref[...])
    cqn = _rms(h[:, :Q_RANK], gq_ref[...]).astype(BF16)
    ckvn = _rms(h[:, Q_RANK:latent], gkv_ref[...]).astype(BF16)

    ang_t = invf_ref[...] * posr_ref[...].astype(F32)
    cos_t, sin_t = jnp.cos(ang_t), jnp.sin(ang_t)
    table = jnp.concatenate([cos_t, cos_t, sin_t, sin_t], axis=0).T
    k_r = h[:, latent:latent + ROPE]
    kr_ref[0] = (jnp.concatenate([k_r, k_r], axis=1) * table).astype(BF16)

    kn = _dot(ckvn, wk_ref[...])
    vt = _dot_nt(wvt_ref[...], ckvn)
    qt = _dot_nt(wqt_ref[...], cqn)
    ones_row = (lax.broadcasted_iota(jnp.int32, (V_ROWS - V_DIM, tk), 0) == 0).astype(BF16)
    for hh in range(MLA_HEADS):
        kn_ref[0, hh] = kn[:, NOPE * hh:NOPE * hh + NOPE].astype(BF16)
        for c in range(tm // tk):
            vt_ref[0, hh, c, 0:V_DIM, :] = vt[V_DIM * hh:V_DIM * hh + V_DIM, tk * c:tk * c + tk].astype(BF16)
            vt_ref[0, hh, c, V_DIM:V_ROWS, :] = ones_row
        base = QK_DIM * hh
        r1 = qt[base + NOPE:base + NOPE + half]
        r2 = qt[base + NOPE + half:base + QK_DIM]
        qr1 = (r1 * cos_t - r2 * sin_t).astype(BF16)
        qr2 = (r2 * cos_t + r1 * sin_t).astype(BF16)
        qt_ref[0, hh, 0:NOPE, :] = qt[base:base + NOPE].astype(BF16)
        qt_ref[0, hh, NOPE:NOPE + half, :] = qr1
        qt_ref[0, hh, NOPE + half:NOPE + 2 * half, :] = qr2
        qt_ref[0, hh, NOPE + 2 * half:NOPE + 3 * half, :] = qr2
        qt_ref[0, hh, NOPE + 3 * half:QK_PAD, :] = -qr1


def _convmix_kernel(x_ref, w_ref, cw_ref, wo_ref, o_ref, wob_ref, zs_ref, *, tc, tiles_per_seq):
    @pl.when(pl.program_id(0) % tiles_per_seq == 0)
    def _():
        zs_ref[0:HALO, :] = jnp.zeros((HALO, CONV_WIDTH), F32)

    wob_ref[...] = wo_ref[...].astype(BF16)

    xb = x_ref[...].astype(BF16)
    for c in range(CONV_WIDTH // tc):
        cols = slice(c * tc, (c + 1) * tc)
        gate_b = _dot_nt(xb, w_ref[c * tc:(c + 1) * tc, :])
        gate_c = _dot_nt(xb, w_ref[CONV_WIDTH + c * tc:CONV_WIDTH + (c + 1) * tc, :])
        x_c = _dot_nt(xb, w_ref[2 * CONV_WIDTH + c * tc:2 * CONV_WIDTH + (c + 1) * tc, :])
        y = _causal_conv3(gate_c * x_c, zs_ref, cw_ref[:, cols], cols)
        o_ref[:, cols] = (gate_b * y).astype(BF16)


def _attn_kernel(qt_ref, kn_ref, kr_ref, vt_ref, o_ref, m_ref, acc_ref, *, tq, tk, heads, lookahead):
    qi = pl.program_id(1)
    per_q = tq // tk
    trip_blocks = 2 * per_q
    m_ref[...] = jnp.full(m_ref.shape, NEG, F32)
    acc_ref[...] = jnp.zeros(acc_ref.shape, F32)

    def run(chains):
        def scores(chain):
            g, j, c0, _ = chain
            start = pl.multiple_of(j * tk, tk)
            k = jnp.concatenate([kn_ref[0, g, pl.ds(start, tk), :], kr_ref[0, pl.ds(start, tk), :]], axis=1)
            return _dot(k, qt_ref[0, g, :, c0:])

        ahead = [scores(ch) for ch in chains[:lookahead]]
        for i, (g, j, c0, mask) in enumerate(chains):
            s = ahead.pop(0)
            if i + lookahead < len(chains):
                ahead.append(scores(chains[i + lookahead]))
            if mask is not None:
                s = jnp.where(mask, s, NEG)
            m = m_ref[g, :, c0:]
            m_new = jnp.maximum(m, jnp.max(s, axis=0, keepdims=True))
            p = jnp.exp2(s - m_new)
            a = jnp.exp2(m - m_new)
            m_ref[g, :, c0:] = m_new
            acc_ref[g, :, c0:] = a * acc_ref[g, :, c0:] + _dot(vt_ref[0, g, j], p.astype(BF16))

    def full_blocks(first, count):
        return [(g, first + b, 0, None) for b in range(count) for g in range(heads)]

    def below_diagonal(t, c):
        run(full_blocks(trip_blocks * t, trip_blocks))
        return c

    n_below = qi * per_q
    lax.fori_loop(0, n_below // trip_blocks, below_diagonal, 0)

    @pl.when(n_below % trip_blocks != 0)
    def _():
        run(full_blocks(n_below - per_q, per_q))

    diag = []
    for b in range(per_q):
        c0 = b * tk
        key = lax.broadcasted_iota(jnp.int32, (tk, tq - c0), 0)
        qry = lax.broadcasted_iota(jnp.int32, (tk, tq - c0), 1)
        diag += [(g, per_q * qi + b, c0, key <= qry) for g in range(heads)]
    run(diag)
    for g in range(heads):
        out_t = acc_ref[g, 0:V_DIM, :] / acc_ref[g, V_DIM:V_DIM + 1, :]
        o_ref[0, :, V_DIM * g:V_DIM * g + V_DIM] = out_t.T.astype(BF16)


def _outproj_kernel(a_ref, c_ref, x_ref, w_ref, g_ref, b_ref, o32_ref, o16_ref, *, tm, rc):
    for r in range(tm // rc):
        rows = slice(r * rc, (r + 1) * rc)
        mixed = jnp.concatenate([a_ref[rows, :], c_ref[rows, :]], axis=1)
        y = _dot(mixed, w_ref[...]) + ALPHA * x_ref[rows, :]
        out = _layer_norm(y, g_ref[...], b_ref[...])
        o32_ref[rows, :] = out
        o16_ref[rows, :] = out.astype(BF16)


def _ffn_up_kernel(x_ref, wg_ref, wu_ref, cwg_ref, cwu_ref, wd_ref, o_ref, wdb_ref,
                   wgb_ref, wub_ref, gs_ref, us_ref, *, tn, tiles_per_seq):
    mi = pl.program_id(1)

    @pl.when(mi == 0)
    def _():
        wgb_ref[...] = wg_ref[...].astype(BF16)
        wub_ref[...] = wu_ref[...].astype(BF16)

    @pl.when(mi % tiles_per_seq == 0)
    def _():
        gs_ref[0:HALO, :] = jnp.zeros((HALO, tn), F32)
        us_ref[0:HALO, :] = jnp.zeros((HALO, tn), F32)

    wdb_ref[...] = wd_ref[...].astype(BF16)

    xb = x_ref[...]
    cols = slice(0, tn)
    g = _causal_conv3(_dot(xb, wgb_ref[...]), gs_ref, cwg_ref[...], cols)
    u = _causal_conv3(_dot(xb, wub_ref[...]), us_ref, cwu_ref[...], cols)
    o_ref[...] = (g / (1.0 + jnp.exp(-g)) * u).astype(BF16)


def _ffn_down_kernel(h_ref, w_ref, x_ref, g_ref, b_ref, o_ref, *, tm, rc):
    for r in range(tm // rc):
        rows = slice(r * rc, (r + 1) * rc)
        y = _dot(h_ref[rows, :], w_ref[...]) + ALPHA * x_ref[rows, :]
        o_ref[rows, :] = _layer_norm(y, g_ref[...], b_ref[...])


def _params(n_grid_axes):
    return pltpu.CompilerParams(dimension_semantics=("arbitrary",) * n_grid_axes, vmem_limit_bytes=VMEM_LIMIT)


def _layer(x2, pos, w_in, q_norm_g, w_uq, kv_norm_g, w_ukv, conv_w, w_out, ln1_g, ln1_b,
           w_gate_up, ffn_conv_w, w_down, ln2_g, ln2_b, batch, seq):
    T = batch * seq
    for tile in (TM_LATENT, TQ_ATTN, TM_CONVMIX, TM_OUTPROJ, TM_FFN_UP, TM_FFN_DOWN):
        assert seq % tile == 0, (seq, tile)
    assert TM_LATENT % TK_ATTN == 0 and TQ_ATTN % TK_ATTN == 0
    assert x2.shape == (T, D_MODEL) and w_in.shape == (D_MODEL, LAT_ROWS - LANES + ROPE + 3 * CONV_WIDTH)

    half = ROPE // 2
    wqt = w_uq.T.astype(BF16)
    ukv = w_ukv.reshape(KV_RANK, MLA_HEADS, NOPE + V_DIM)
    wk = ukv[:, :, :NOPE].reshape(KV_RANK, MLA_HEADS * NOPE).astype(BF16)
    wvt = ukv[:, :, NOPE:].reshape(KV_RANK, MLA_HEADS * V_DIM).T.astype(BF16)
    gq = (q_norm_g * Q_SCALE).reshape(1, Q_RANK)
    gkv = kv_norm_g.reshape(1, KV_RANK)
    inv_freq = (ROPE_THETA ** (-jnp.arange(0, ROPE, 2, dtype=F32) / ROPE)).reshape(half, 1)
    posr = pos.reshape(1, T)

    tm, tk = TM_LATENT, TK_ATTN
    nblk = seq // tm
    full = lambda shape: pl.BlockSpec(shape, lambda *_: (0,) * len(shape), pipeline_mode=pl.Buffered(1))
    n1 = T // tm
    w_in_t = w_in.T
    conv_row0 = Q_RANK + KV_RANK + ROPE
    n_wc = 3 * CONV_WIDTH // (WCHUNK * n1)
    assert conv_row0 % WCHUNK == 0 and 3 * CONV_WIDTH == n_wc * WCHUNK * n1
    wchunk = lambda n: pl.BlockSpec((WCHUNK, D_MODEL), lambda i: (conv_row0 // WCHUNK + n_wc * i + n, 0))
    qt, kn, kr, vt, wconv = pl.pallas_call(
        functools.partial(_latent_kernel, tm=tm, tk=tk, n_wc=n_wc),
        grid=(n1,),
        in_specs=[
            pl.BlockSpec((tm, D_MODEL), lambda i: (i, 0)),
            pl.BlockSpec((1, tm), lambda i: (0, i)),
            full((half, 1)),
            full((LAT_ROWS, D_MODEL)),
            full((1, Q_RANK)), full((1, KV_RANK)),
            full((MLA_HEADS * QK_DIM, Q_RANK)),
            full((KV_RANK, MLA_HEADS * NOPE)),
            full((MLA_HEADS * V_DIM, KV_RANK)),
            *[wchunk(n) for n in range(n_wc)],
        ],
        out_specs=[
            pl.BlockSpec((1, MLA_HEADS, QK_PAD, tm), lambda i: (i // nblk, 0, 0, i % nblk)),
            pl.BlockSpec((1, MLA_HEADS, tm, NOPE), lambda i: (i // nblk, 0, i % nblk, 0)),
            pl.BlockSpec((1, tm, ROPE_LANES), lambda i: (i // nblk, i % nblk, 0)),
            pl.BlockSpec((1, MLA_HEADS, tm // tk, V_ROWS, tk), lambda i: (i // nblk, 0, i % nblk, 0, 0)),
            pl.BlockSpec((n_wc * WCHUNK, D_MODEL), lambda i: (i, 0)),
        ],
        out_shape=[
            jax.ShapeDtypeStruct((batch, MLA_HEADS, QK_PAD, seq), BF16),
            jax.ShapeDtypeStruct((batch, MLA_HEADS, seq, NOPE), BF16),
            jax.ShapeDtypeStruct((batch, seq, ROPE_LANES), BF16),
            jax.ShapeDtypeStruct((batch, MLA_HEADS, seq // tk, V_ROWS, tk), BF16),
            jax.ShapeDtypeStruct((3 * CONV_WIDTH, D_MODEL), BF16),
        ],
        scratch_shapes=[pltpu.VMEM((LAT_ROWS, D_MODEL), BF16)],
        compiler_params=_params(1),
        name="latent",
    )(x2, posr, inv_freq, w_in_t, gq, gkv, wqt, wk, wvt, *([w_in_t] * n_wc))

    tm2 = TM_CONVMIX
    n2 = T // tm2
    wo_rows = D_MODEL // n2
    conv_out, w_out_b = pl.pallas_call(
        functools.partial(_convmix_kernel, tc=TC_CONVMIX, tiles_per_seq=seq // tm2),
        grid=(n2,),
        in_specs=[
            pl.BlockSpec((tm2, D_MODEL), lambda i: (i, 0)),
            full((3 * CONV_WIDTH, D_MODEL)),
            full((3, CONV_WIDTH)),
            pl.BlockSpec((wo_rows, D_MODEL), lambda i: (i, 0)),
        ],
        out_specs=[pl.BlockSpec((tm2, CONV_WIDTH), lambda i: (i, 0)),
                   pl.BlockSpec((wo_rows, D_MODEL), lambda i: (i, 0))],
        out_shape=[jax.ShapeDtypeStruct((T, CONV_WIDTH), BF16),
                   jax.ShapeDtypeStruct((D_MODEL, D_MODEL), BF16)],
        scratch_shapes=[pltpu.VMEM((2 * HALO, CONV_WIDTH), F32)],
        compiler_params=_params(1),
        name="convmix",
    )(x2, wconv, conv_w, w_out)

    tq = TQ_ATTN
    attn = pl.pallas_call(
        functools.partial(_attn_kernel, tq=tq, tk=tk, heads=MLA_HEADS, lookahead=ATTN_LOOKAHEAD),
        grid=(batch, seq // tq),
        in_specs=[
            pl.BlockSpec((1, MLA_HEADS, QK_PAD, tq), lambda b, i: (b, 0, 0, i)),
            pl.BlockSpec((1, MLA_HEADS, seq, NOPE), lambda b, i: (b, 0, 0, 0)),
            pl.BlockSpec((1, seq, ROPE_LANES), lambda b, i: (b, 0, 0)),
            pl.BlockSpec((1, MLA_HEADS, seq // tk, V_ROWS, tk), lambda b, i: (b, 0, 0, 0, 0)),
        ],
        out_specs=pl.BlockSpec((1, tq, MLA_WIDTH), lambda b, i: (b, i, 0)),
        out_shape=jax.ShapeDtypeStruct((batch, seq, MLA_WIDTH), BF16),
        scratch_shapes=[pltpu.VMEM((MLA_HEADS, 1, tq), F32), pltpu.VMEM((MLA_HEADS, V_ROWS, tq), F32)],
        compiler_params=_params(2),
        name="attn",
    )(qt, kn, kr, vt)
    attn = attn.reshape(T, MLA_WIDTH)

    tm4 = TM_OUTPROJ
    x1, x1b = pl.pallas_call(
        functools.partial(_outproj_kernel, tm=tm4, rc=RC_OUTPROJ),
        grid=(T // tm4,),
        in_specs=[
            pl.BlockSpec((tm4, MLA_WIDTH), lambda i: (i, 0)),
            pl.BlockSpec((tm4, CONV_WIDTH), lambda i: (i, 0)),
            pl.BlockSpec((tm4, D_MODEL), lambda i: (i, 0)),
            full((D_MODEL, D_MODEL)), full((1, D_MODEL)), full((1, D_MODEL)),
        ],
        out_specs=[pl.BlockSpec((tm4, D_MODEL), lambda i: (i, 0)), pl.BlockSpec((tm4, D_MODEL), lambda i: (i, 0))],
        out_shape=[jax.ShapeDtypeStruct((T, D_MODEL), F32), jax.ShapeDtypeStruct((T, D_MODEL), BF16)],
        compiler_params=_params(1),
        name="outproj",
    )(attn, conv_out, x2, w_out_b, ln1_g.reshape(1, D_MODEL), ln1_b.reshape(1, D_MODEL))

    tm5, tn5 = TM_FFN_UP, TN_FFN_UP
    nn5 = FFN_DIM // tn5
    nm5 = T // tm5
    wd_rows = FFN_DIM // (nn5 * nm5)
    h2, w_down_b = pl.pallas_call(
        functools.partial(_ffn_up_kernel, tn=tn5, tiles_per_seq=seq // tm5),
        grid=(nn5, nm5),
        in_specs=[
            pl.BlockSpec((tm5, D_MODEL), lambda n, m: (m, 0)),
            pl.BlockSpec((D_MODEL, tn5), lambda n, m: (0, n)),
            pl.BlockSpec((D_MODEL, tn5), lambda n, m: (0, n + nn5)),
            pl.BlockSpec((3, tn5), lambda n, m: (0, n)),
            pl.BlockSpec((3, tn5), lambda n, m: (0, n + nn5)),
            pl.BlockSpec((wd_rows, D_MODEL), lambda n, m: (n * nm5 + m, 0)),
        ],
        out_specs=[pl.BlockSpec((tm5, tn5), lambda n, m: (m, n)),
                   pl.BlockSpec((wd_rows, D_MODEL), lambda n, m: (n * nm5 + m, 0))],
        out_shape=[jax.ShapeDtypeStruct((T, FFN_DIM), BF16),
                   jax.ShapeDtypeStruct((FFN_DIM, D_MODEL), BF16)],
        scratch_shapes=[pltpu.VMEM((D_MODEL, tn5), BF16), pltpu.VMEM((D_MODEL, tn5), BF16),
                        pltpu.VMEM((2 * HALO, tn5), F32), pltpu.VMEM((2 * HALO, tn5), F32)],
        compiler_params=_params(2),
        name="ffn_up",
    )(x1b, w_gate_up, w_gate_up, ffn_conv_w, ffn_conv_w, w_down)

    tm6 = TM_FFN_DOWN
    out = pl.pallas_call(
        functools.partial(_ffn_down_kernel, tm=tm6, rc=RC_FFN_DOWN),
        grid=(T // tm6,),
        in_specs=[
            pl.BlockSpec((tm6, FFN_DIM), lambda m: (m, 0)),
            full((FFN_DIM, D_MODEL)),
            pl.BlockSpec((tm6, D_MODEL), lambda m: (m, 0)),
            full((1, D_MODEL)), full((1, D_MODEL)),
        ],
        out_specs=pl.BlockSpec((tm6, D_MODEL), lambda m: (m, 0)),
        out_shape=jax.ShapeDtypeStruct((T, D_MODEL), F32),
        compiler_params=_params(1),
        name="ffn_down",
    )(h2, w_down_b, x1, ln2_g.reshape(1, D_MODEL), ln2_b.reshape(1, D_MODEL))
    return out


def kernel(x, positions, w_in, q_norm_g, w_uq, kv_norm_g, w_ukv, conv_w, w_out, ln1_g, ln1_b,
           w_gate_up, ffn_conv_w, w_down, ln2_g, ln2_b):
    batch, seq, _ = x.shape
    x2 = x.reshape(batch * seq, D_MODEL)
    for l in range(DEPTH):
        x2 = _layer(x2, positions, w_in[l], q_norm_g[l], w_uq[l], kv_norm_g[l], w_ukv[l], conv_w[l],
                    w_out[l], ln1_g[l], ln1_b[l], w_gate_up[l], ffn_conv_w[l], w_down[l], ln2_g[l], ln2_b[l],
                    batch, seq)
    return x2.reshape(batch, seq, D_MODEL)
```

```python
import functools
import math

import jax
import jax.numpy as jnp
from jax import lax
from jax.experimental import pallas as pl
from jax.experimental.pallas import tpu as pltpu

LANES = 128
F32_SUBLANES = 8
BF16_SUBLANES = 2 * F32_SUBLANES
V7X_VMEM_BYTES = 64 * 1024 * 1024
VMEM_LIMIT = V7X_VMEM_BYTES - 8 * 1024 * 1024

D_MODEL = 2048
MLA_HEADS = 8
NOPE = 128
ROPE = 64
V_DIM = 128
V_ROWS = V_DIM + BF16_SUBLANES
MLA_WIDTH = MLA_HEADS * V_DIM
Q_RANK = 512
KV_RANK = 256
CONV_WIDTH = 1024
FFN_DIM = 5632
ROPE_THETA = 10000.0
LN_EPS = 1e-5
RMS_EPS = 1e-6
DEPTH = 1
ALPHA = (2.0 * DEPTH) ** 0.25
QK_DIM = NOPE + ROPE
Q_SCALE = (QK_DIM ** -0.5) * math.log2(math.e)
ROPE_LANES = 2 * ROPE
QK_PAD = NOPE + ROPE_LANES
LAT_ROWS = Q_RANK + KV_RANK + LANES
HALO = F32_SUBLANES
WCHUNK = 64
NEG = -1e30

TM_LATENT = 512
TK_ATTN = 256
TQ_ATTN = 512
ATTN_LOOKAHEAD = 2
TM_CONVMIX, TC_CONVMIX = 1024, 256
TM_OUTPROJ, RC_OUTPROJ = 512, 128
TM_FFN_UP, TN_FFN_UP = 1024, 512
TM_FFN_DOWN, RC_FFN_DOWN = 512, 128

F32 = jnp.float32
BF16 = jnp.bfloat16


def _dot(a, b):
    return jnp.dot(a, b, preferred_element_type=F32)


def _dot_nt(a, b):
    return lax.dot_general(a, b, (((1,), (1,)), ((), ())), preferred_element_type=F32)


def _rms(x, g):
    return x * lax.rsqrt(jnp.mean(x * x, axis=-1, keepdims=True) + RMS_EPS) * g


def _layer_norm(y, g, b):
    mu = jnp.mean(y, axis=-1, keepdims=True)
    d = y - mu
    var = jnp.mean(d * d, axis=-1, keepdims=True)
    return d * lax.rsqrt(var + LN_EPS) * g + b


def _causal_conv3(z, hs_ref, cw, cols):
    n = z.shape[0]
    w0, w1, w2 = cw[0:1, :], cw[1:2, :], cw[2:3, :]
    y = w2 * z + w1 * pltpu.roll(z, 1, axis=0) + w0 * pltpu.roll(z, 2, axis=0)
    head = z[0:HALO]
    hs_ref[HALO:2 * HALO, cols] = head
    y_head = w2 * head + w1 * hs_ref[HALO - 1:2 * HALO - 1, cols] + w0 * hs_ref[HALO - 2:2 * HALO - 2, cols]
    hs_ref[0:HALO, cols] = z[n - HALO:n]
    return jnp.concatenate([y_head, y[HALO:]], axis=0)


def _latent_kernel(x_ref, posr_ref, invf_ref, win_ref, gq_ref, gkv_ref, wuq_ref, wukv_ref, *refs,
                   tm, tk, n_wc):
    wc_refs = refs[:n_wc]
    qt_ref, kn_ref, kr_ref, vt_ref, wconv_ref, wlat_ref, wqt_ref, wk_ref, wvt_ref = refs[n_wc:]
    half = ROPE // 2
    latent = Q_RANK + KV_RANK

    @pl.when(pl.program_id(0) == 0)
    def _():
        wlat_ref[...] = win_ref[...].astype(BF16)
        wqt_ref[...] = wuq_ref[...].T.astype(BF16)
        for hh in range(MLA_HEADS):
            k_cols = slice((NOPE + V_DIM) * hh, (NOPE + V_DIM) * hh + NOPE)
            v_cols = slice((NOPE + V_DIM) * hh + NOPE, (NOPE + V_DIM) * (hh + 1))
            wk_ref[:, NOPE * hh:NOPE * (hh + 1)] = wukv_ref[:, k_cols].astype(BF16)
            wvt_ref[V_DIM * hh:V_DIM * (hh + 1), :] = wukv_ref[:, v_cols].T.astype(BF16)

    for n, wc_ref in enumerate(wc_refs):
        wconv_ref[n * WCHUNK:(n + 1) * WCHUNK, :] = wc_ref[...].astype(BF16)

    xb = x_ref[...].astype(BF16)
    h = _dot_nt(xb, wlat_ref[...])
    cqn = _rms(h[:, :Q_RANK], gq_ref[...]).astype(BF16)
    ckvn = _rms(h[:, Q_RANK:latent], gkv_ref[...]).astype(BF16)

    ang_t = invf_ref[...] * posr_ref[...].astype(F32)
    cos_t, sin_t = jnp.cos(ang_t), jnp.sin(ang_t)
    table = jnp.concatenate([cos_t, cos_t, sin_t, sin_t], axis=0).T
    k_r = h[:, latent:latent + ROPE]
    kr_ref[0] = (jnp.concatenate([k_r, k_r], axis=1) * table).astype(BF16)

    kn = _dot(ckvn, wk_ref[...])
    vt = _dot_nt(wvt_ref[...], ckvn)
    qt = _dot_nt(wqt_ref[...], cqn)
    ones_row = (lax.broadcasted_iota(jnp.int32, (V_ROWS - V_DIM, tk), 0) == 0).astype(BF16)
    for hh in range(MLA_HEADS):
        kn_ref[0, hh] = kn[:, NOPE * hh:NOPE * hh + NOPE].astype(BF16)
        for c in range(tm // tk):
            vt_ref[0, hh, c, 0:V_DIM, :] = vt[V_DIM * hh:V_DIM * hh + V_DIM, tk * c:tk * c + tk].astype(BF16)
            vt_ref[0, hh, c, V_DIM:V_ROWS, :] = ones_row
        base = QK_DIM * hh
        r1 = qt[base + NOPE:base + NOPE + half]
        r2 = qt[base + NOPE + half:base + QK_DIM]
        qr1 = (r1 * cos_t - r2 * sin_t).astype(BF16)
        qr2 = (r2 * cos_t + r1 * sin_t).astype(BF16)
        qt_ref[0, hh, 0:NOPE, :] = qt[base:base + NOPE].astype(BF16)
        qt_ref[0, hh, NOPE:NOPE + half, :] = qr1
        qt_ref[0, hh, NOPE + half:NOPE + 2 * half, :] = qr2
        qt_ref[0, hh, NOPE + 2 * half:NOPE + 3 * half, :] = qr2
        qt_ref[0, hh, NOPE + 3 * half:QK_PAD, :] = -qr1


def _convmix_kernel(x_ref, w_ref, cw_ref, wo_ref, o_ref, wob_ref, zs_ref, *, tc, tiles_per_seq):
    @pl.when(pl.program_id(0) % tiles_per_seq == 0)
    def _():
        zs_ref[0:HALO, :] = jnp.zeros((HALO, CONV_WIDTH), F32)

    wob_ref[...] = wo_ref[...].astype(BF16)

    xb = x_ref[...].astype(BF16)
    for c in range(CONV_WIDTH // tc):
        cols = slice(c * tc, (c + 1) * tc)
        gate_b = _dot_nt(xb, w_ref[c * tc:(c + 1) * tc, :])
        gate_c = _dot_nt(xb, w_ref[CONV_WIDTH + c * tc:CONV_WIDTH + (c + 1) * tc, :])
        x_c = _dot_nt(xb, w_ref[2 * CONV_WIDTH + c * tc:2 * CONV_WIDTH + (c + 1) * tc, :])
        y = _causal_conv3(gate_c * x_c, zs_ref, cw_ref[:, cols], cols)
        o_ref[:, cols] = (gate_b * y).astype(BF16)


def _attn_kernel(qt_ref, kn_ref, kr_ref, vt_ref, o_ref, m_ref, acc_ref, *, tq, tk, heads, lookahead):
    qi = pl.program_id(1)
    per_q = tq // tk
    trip_blocks = 2 * per_q
    m_ref[...] = jnp.full(m_ref.shape, NEG, F32)
    acc_ref[...] = jnp.zeros(acc_ref.shape, F32)

    def run(chains):
        def scores(chain):
            g, j, c0, _ = chain
            start = pl.multiple_of(j * tk, tk)
            k = jnp.concatenate([kn_ref[0, g, pl.ds(start, tk), :], kr_ref[0, pl.ds(start, tk), :]], axis=1)
            return _dot(k, qt_ref[0, g, :, c0:])

        ahead = [scores(ch) for ch in chains[:lookahead]]
        for i, (g, j, c0, mask) in enumerate(chains):
            s = ahead.pop(0)
            if i + lookahead < len(chains):
                ahead.append(scores(chains[i + lookahead]))
            if mask is not None:
                s = jnp.where(mask, s, NEG)
            m = m_ref[g, :, c0:]
            m_new = jnp.maximum(m, jnp.max(s, axis=0, keepdims=True))
            p = jnp.exp2(s - m_new)
            a = jnp.exp2(m - m_new)
            m_ref[g, :, c0:] = m_new
            acc_ref[g, :, c0:] = a * acc_ref[g, :, c0:] + _dot(vt_ref[0, g, j], p.astype(BF16))

    def full_blocks(first, count):
        return [(g, first + b, 0, None) for b in range(count) for g in range(heads)]

    def below_diagonal(t, c):
        run(full_blocks(trip_blocks * t, trip_blocks))
        return c

    n_below = qi * per_q
    lax.fori_loop(0, n_below // trip_blocks, below_diagonal, 0)

    @pl.when(n_below % trip_blocks != 0)
    def _():
        run(full_blocks(n_below - per_q, per_q))

    diag = []
    for b in range(per_q):
        c0 = b * tk
        key = lax.broadcasted_iota(jnp.int32, (tk, tq - c0), 0)
        qry = lax.broadcasted_iota(jnp.int32, (tk, tq - c0), 1)
        diag += [(g, per_q * qi + b, c0, key <= qry) for g in range(heads)]
    run(diag)
    for g in range(heads):
        out_t = acc_ref[g, 0:V_DIM, :] / acc_ref[g, V_DIM:V_DIM + 1, :]
        o_ref[0, :, V_DIM * g:V_DIM * g + V_DIM] = out_t.T.astype(BF16)


def _outproj_kernel(a_ref, c_ref, x_ref, w_ref, g_ref, b_ref, o32_ref, o16_ref, *, tm, rc):
    for r in range(tm // rc):
        rows = slice(r * rc, (r + 1) * rc)
        mixed = jnp.concatenate([a_ref[rows, :], c_ref[rows, :]], axis=1)
        y = _dot(mixed, w_ref[...]) + ALPHA * x_ref[rows, :]
        out = _layer_norm(y, g_ref[...], b_ref[...])
        o32_ref[rows, :] = out
        o16_ref[rows, :] = out.astype(BF16)


def _ffn_up_kernel(x_ref, wg_ref, wu_ref, cwg_ref, cwu_ref, wd_ref, o_ref, wdb_ref,
                   wgb_ref, wub_ref, gs_ref, us_ref, *, tn, tiles_per_seq):
    mi = pl.program_id(1)

    @pl.when(mi == 0)
    def _():
        wgb_ref[...] = wg_ref[...].astype(BF16)
        wub_ref[...] = wu_ref[...].astype(BF16)

    @pl.when(mi % tiles_per_seq == 0)
    def _():
        gs_ref[0:HALO, :] = jnp.zeros((HALO, tn), F32)
        us_ref[0:HALO, :] = jnp.zeros((HALO, tn), F32)

    wdb_ref[...] = wd_ref[...].astype(BF16)

    xb = x_ref[...]
    cols = slice(0, tn)
    g = _causal_conv3(_dot(xb, wgb_ref[...]), gs_ref, cwg_ref[...], cols)
    u = _causal_conv3(_dot(xb, wub_ref[...]), us_ref, cwu_ref[...], cols)
    o_ref[...] = (g / (1.0 + jnp.exp(-g)) * u).astype(BF16)


def _ffn_down_kernel(h_ref, w_ref, x_ref, g_ref, b_ref, o_ref, *, tm, rc):
    for r in range(tm // rc):
        rows = slice(r * rc, (r + 1) * rc)
        y = _dot(h_ref[rows, :], w_ref[...]) + ALPHA * x_ref[rows, :]
        o_ref[rows, :] = _layer_norm(y, g_ref[...], b_ref[...])


def _params(n_grid_axes):
    return pltpu.CompilerParams(dimension_semantics=("arbitrary",) * n_grid_axes, vmem_limit_bytes=VMEM_LIMIT)


def _layer(x2, pos, w_in, q_norm_g, w_uq, kv_norm_g, w_ukv, conv_w, w_out, ln1_g, ln1_b,
           w_gate_up, ffn_conv_w, w_down, ln2_g, ln2_b, batch, seq):
    T = batch * seq
    for tile in (TM_LATENT, TQ_ATTN, TM_CONVMIX, TM_OUTPROJ, TM_FFN_UP, TM_FFN_DOWN):
        assert seq % tile == 0, (seq, tile)
    assert TM_LATENT % TK_ATTN == 0 and TQ_ATTN % TK_ATTN == 0
    assert x2.shape == (T, D_MODEL) and w_in.shape == (D_MODEL, LAT_ROWS - LANES + ROPE + 3 * CONV_WIDTH)

    half = ROPE // 2
    gq = (q_norm_g * Q_SCALE).reshape(1, Q_RANK)
    gkv = kv_norm_g.reshape(1, KV_RANK)
    inv_freq = (ROPE_THETA ** (-jnp.arange(0, ROPE, 2, dtype=F32) / ROPE)).reshape(half, 1)
    posr = pos.reshape(1, T)

    tm, tk = TM_LATENT, TK_ATTN
    nblk = seq // tm
    full = lambda shape: pl.BlockSpec(shape, lambda *_: (0,) * len(shape), pipeline_mode=pl.Buffered(1))
    n1 = T // tm
    w_in_t = w_in.T
    conv_row0 = Q_RANK + KV_RANK + ROPE
    n_wc = 3 * CONV_WIDTH // (WCHUNK * n1)
    assert conv_row0 % WCHUNK == 0 and 3 * CONV_WIDTH == n_wc * WCHUNK * n1
    wchunk = lambda n: pl.BlockSpec((WCHUNK, D_MODEL), lambda i: (conv_row0 // WCHUNK + n_wc * i + n, 0))
    qt, kn, kr, vt, wconv = pl.pallas_call(
        functools.partial(_latent_kernel, tm=tm, tk=tk, n_wc=n_wc),
        grid=(n1,),
        in_specs=[
            pl.BlockSpec((tm, D_MODEL), lambda i: (i, 0)),
            pl.BlockSpec((1, tm), lambda i: (0, i)),
            full((half, 1)),
            full((LAT_ROWS, D_MODEL)),
            full((1, Q_RANK)), full((1, KV_RANK)),
            full((Q_RANK, MLA_HEADS * QK_DIM)),
            full((KV_RANK, MLA_HEADS * (NOPE + V_DIM))),
            *[wchunk(n) for n in range(n_wc)],
        ],
        out_specs=[
            pl.BlockSpec((1, MLA_HEADS, QK_PAD, tm), lambda i: (i // nblk, 0, 0, i % nblk)),
            pl.BlockSpec((1, MLA_HEADS, tm, NOPE), lambda i: (i // nblk, 0, i % nblk, 0)),
            pl.BlockSpec((1, tm, ROPE_LANES), lambda i: (i // nblk, i % nblk, 0)),
            pl.BlockSpec((1, MLA_HEADS, tm // tk, V_ROWS, tk), lambda i: (i // nblk, 0, i % nblk, 0, 0)),
            pl.BlockSpec((n_wc * WCHUNK, D_MODEL), lambda i: (i, 0)),
        ],
        out_shape=[
            jax.ShapeDtypeStruct((batch, MLA_HEADS, QK_PAD, seq), BF16),
            jax.ShapeDtypeStruct((batch, MLA_HEADS, seq, NOPE), BF16),
            jax.ShapeDtypeStruct((batch, seq, ROPE_LANES), BF16),
            jax.ShapeDtypeStruct((batch, MLA_HEADS, seq // tk, V_ROWS, tk), BF16),
            jax.ShapeDtypeStruct((3 * CONV_WIDTH, D_MODEL), BF16),
        ],
        scratch_shapes=[pltpu.VMEM((LAT_ROWS, D_MODEL), BF16),
                        pltpu.VMEM((MLA_HEADS * QK_DIM, Q_RANK), BF16),
                        pltpu.VMEM((KV_RANK, MLA_HEADS * NOPE), BF16),
                        pltpu.VMEM((MLA_HEADS * V_DIM, KV_RANK), BF16)],
        compiler_params=_params(1),
        name="latent",
    )(x2, posr, inv_freq, w_in_t, gq, gkv, w_uq, w_ukv, *([w_in_t] * n_wc))

    tm2 = TM_CONVMIX
    n2 = T // tm2
    wo_rows = D_MODEL // n2
    conv_out, w_out_b = pl.pallas_call(
        functools.partial(_convmix_kernel, tc=TC_CONVMIX, tiles_per_seq=seq // tm2),
        grid=(n2,),
        in_specs=[
            pl.BlockSpec((tm2, D_MODEL), lambda i: (i, 0)),
            full((3 * CONV_WIDTH, D_MODEL)),
            full((3, CONV_WIDTH)),
            pl.BlockSpec((wo_rows, D_MODEL), lambda i: (i, 0)),
        ],
        out_specs=[pl.BlockSpec((tm2, CONV_WIDTH), lambda i: (i, 0)),
                   pl.BlockSpec((wo_rows, D_MODEL), lambda i: (i, 0))],
        out_shape=[jax.ShapeDtypeStruct((T, CONV_WIDTH), BF16),
                   jax.ShapeDtypeStruct((D_MODEL, D_MODEL), BF16)],
        scratch_shapes=[pltpu.VMEM((2 * HALO, CONV_WIDTH), F32)],
        compiler_params=_params(1),
        name="convmix",
    )(x2, wconv, conv_w, w_out)

    tq = TQ_ATTN
    attn = pl.pallas_call(
        functools.partial(_attn_kernel, tq=tq, tk=tk, heads=MLA_HEADS, lookahead=ATTN_LOOKAHEAD),
        grid=(batch, seq // tq),
        in_specs=[
            pl.BlockSpec((1, MLA_HEADS, QK_PAD, tq), lambda b, i: (b, 0, 0, i)),
            pl.BlockSpec((1, MLA_HEADS, seq, NOPE), lambda b, i: (b, 0, 0, 0)),
            pl.BlockSpec((1, seq, ROPE_LANES), lambda b, i: (b, 0, 0)),
            pl.BlockSpec((1, MLA_HEADS, seq // tk, V_ROWS, tk), lambda b, i: (b, 0, 0, 0, 0)),
        ],
        out_specs=pl.BlockSpec((1, tq, MLA_WIDTH), lambda b, i: (b, i, 0)),
        out_shape=jax.ShapeDtypeStruct((batch, seq, MLA_WIDTH), BF16),
        scratch_shapes=[pltpu.VMEM((MLA_HEADS, 1, tq), F32), pltpu.VMEM((MLA_HEADS, V_ROWS, tq), F32)],
        compiler_params=_params(2),
        name="attn",
    )(qt, kn, kr, vt)
    attn = attn.reshape(T, MLA_WIDTH)

    tm4 = TM_OUTPROJ
    x1, x1b = pl.pallas_call(
        functools.partial(_outproj_kernel, tm=tm4, rc=RC_OUTPROJ),
        grid=(T // tm4,),
        in_specs=[
            pl.BlockSpec((tm4, MLA_WIDTH), lambda i: (i, 0)),
            pl.BlockSpec((tm4, CONV_WIDTH), lambda i: (i, 0)),
            pl.BlockSpec((tm4, D_MODEL), lambda i: (i, 0)),
            full((D_MODEL, D_MODEL)), full((1, D_MODEL)), full((1, D_MODEL)),
        ],
        out_specs=[pl.BlockSpec((tm4, D_MODEL), lambda i: (i, 0)), pl.BlockSpec((tm4, D_MODEL), lambda i: (i, 0))],
        out_shape=[jax.ShapeDtypeStruct((T, D_MODEL), F32), jax.ShapeDtypeStruct((T, D_MODEL), BF16)],
        compiler_params=_params(1),
        name="outproj",
    )(attn, conv_out, x2, w_out_b, ln1_g.reshape(1, D_MODEL), ln1_b.reshape(1, D_MODEL))

    tm5, tn5 = TM_FFN_UP, TN_FFN_UP
    nn5 = FFN_DIM // tn5
    nm5 = T // tm5
    wd_rows = FFN_DIM // (nn5 * nm5)
    h2, w_down_b = pl.pallas_call(
        functools.partial(_ffn_up_kernel, tn=tn5, tiles_per_seq=seq // tm5),
        grid=(nn5, nm5),
        in_specs=[
            pl.BlockSpec((tm5, D_MODEL), lambda n, m: (m, 0)),
            pl.BlockSpec((D_MODEL, tn5), lambda n, m: (0, n)),
            pl.BlockSpec((D_MODEL, tn5), lambda n, m: (0, n + nn5)),
            pl.BlockSpec((3, tn5), lambda n, m: (0, n)),
            pl.BlockSpec((3, tn5), lambda n, m: (0, n + nn5)),
            pl.BlockSpec((wd_rows, D_MODEL), lambda n, m: (n * nm5 + m, 0)),
        ],
        out_specs=[pl.BlockSpec((tm5, tn5), lambda n, m: (m, n)),
                   pl.BlockSpec((wd_rows, D_MODEL), lambda n, m: (n * nm5 + m, 0))],
        out_shape=[jax.ShapeDtypeStruct((T, FFN_DIM), BF16),
                   jax.ShapeDtypeStruct((FFN_DIM, D_MODEL), BF16)],
        scratch_shapes=[pltpu.VMEM((D_MODEL, tn5), BF16), pltpu.VMEM((D_MODEL, tn5), BF16),
                        pltpu.VMEM((2 * HALO, tn5), F32), pltpu.VMEM((2 * HALO, tn5), F32)],
        compiler_params=_params(2),
        name="ffn_up",
    )(x1b, w_gate_up, w_gate_up, ffn_conv_w, ffn_conv_w, w_down)

    tm6 = TM_FFN_DOWN
    out = pl.pallas_call(
        functools.partial(_ffn_down_kernel, tm=tm6, rc=RC_FFN_DOWN),
        grid=(T // tm6,),
        in_specs=[
            pl.BlockSpec((tm6, FFN_DIM), lambda m: (m, 0)),
            full((FFN_DIM, D_MODEL)),
            pl.BlockSpec((tm6, D_MODEL), lambda m: (m, 0)),
            full((1, D_MODEL)), full((1, D_MODEL)),
        ],
        out_specs=pl.BlockSpec((tm6, D_MODEL), lambda m: (m, 0)),
        out_shape=jax.ShapeDtypeStruct((T, D_MODEL), F32),
        compiler_params=_params(1),
        name="ffn_down",
    )(h2, w_down_b, x1, ln2_g.reshape(1, D_MODEL), ln2_b.reshape(1, D_MODEL))
    return out


def kernel(x, positions, w_in, q_norm_g, w_uq, kv_norm_g, w_ukv, conv_w, w_out, ln1_g, ln1_b,
           w_gate_up, ffn_conv_w, w_down, ln2_g, ln2_b):
    batch, seq, _ = x.shape
    x2 = x.reshape(batch * seq, D_MODEL)
    for l in range(DEPTH):
        x2 = _layer(x2, positions, w_in[l], q_norm_g[l], w_uq[l], kv_norm_g[l], w_ukv[l], conv_w[l],
                    w_out[l], ln1_g[l], ln1_b[l], w_gate_up[l], ffn_conv_w[l], w_down[l], ln2_g[l], ln2_b[l],
                    batch, seq)
    return x2.reshape(batch, seq, D_MODEL)
```

```python
import functools
import math

import jax
import jax.numpy as jnp
from jax import lax
from jax.experimental import pallas as pl
from jax.experimental.pallas import tpu as pltpu

LANES = 128
F32_SUBLANES = 8
BF16_SUBLANES = 2 * F32_SUBLANES
V7X_VMEM_BYTES = 64 * 1024 * 1024
VMEM_LIMIT = V7X_VMEM_BYTES - 8 * 1024 * 1024

D_MODEL = 2048
MLA_HEADS = 8
NOPE = 128
ROPE = 64
V_DIM = 128
V_ROWS = V_DIM + BF16_SUBLANES
MLA_WIDTH = MLA_HEADS * V_DIM
Q_RANK = 512
KV_RANK = 256
CONV_WIDTH = 1024
FFN_DIM = 5632
ROPE_THETA = 10000.0
LN_EPS = 1e-5
RMS_EPS = 1e-6
DEPTH = 1
ALPHA = (2.0 * DEPTH) ** 0.25
QK_DIM = NOPE + ROPE
Q_SCALE = (QK_DIM ** -0.5) * math.log2(math.e)
ROPE_LANES = 2 * ROPE
QK_PAD = NOPE + ROPE_LANES
LAT_ROWS = Q_RANK + KV_RANK + LANES
HALO = F32_SUBLANES
WCHUNK = 64
NEG = -1e30

TM_LATENT = 512
TK_ATTN = 256
TQ_ATTN = 512
ATTN_LOOKAHEAD = 2
TM_CONVMIX, TC_CONVMIX = 1024, 256
TM_OUTPROJ, RC_OUTPROJ = 512, 128
TM_FFN_UP, TN_FFN_UP = 1024, 512
TM_FFN_DOWN, RC_FFN_DOWN = 512, 128

F32 = jnp.float32
BF16 = jnp.bfloat16


def _dot(a, b):
    return jnp.dot(a, b, preferred_element_type=F32)


def _dot_nt(a, b):
    return lax.dot_general(a, b, (((1,), (1,)), ((), ())), preferred_element_type=F32)


def _rms(x, g):
    return x * lax.rsqrt(jnp.mean(x * x, axis=-1, keepdims=True) + RMS_EPS) * g


def _layer_norm(y, g, b):
    mu = jnp.mean(y, axis=-1, keepdims=True)
    d = y - mu
    var = jnp.mean(d * d, axis=-1, keepdims=True)
    return d * lax.rsqrt(var + LN_EPS) * g + b


def _causal_conv3(z, hs_ref, cw, cols):
    n = z.shape[0]
    w0, w1, w2 = cw[0:1, :], cw[1:2, :], cw[2:3, :]
    y = w2 * z + w1 * pltpu.roll(z, 1, axis=0) + w0 * pltpu.roll(z, 2, axis=0)
    head = z[0:HALO]
    hs_ref[HALO:2 * HALO, cols] = head
    y_head = w2 * head + w1 * hs_ref[HALO - 1:2 * HALO - 1, cols] + w0 * hs_ref[HALO - 2:2 * HALO - 2, cols]
    hs_ref[0:HALO, cols] = z[n - HALO:n]
    return jnp.concatenate([y_head, y[HALO:]], axis=0)


def _latent_kernel(x_ref, posr_ref, invf_ref, win_ref, gq_ref, gkv_ref, wuq_ref, wukv_ref,
                   qt_ref, kn_ref, kr_ref, vt_ref, wlat_ref, wqt_ref, wk_ref, wvt_ref, *, tm, tk):
    half = ROPE // 2
    latent = Q_RANK + KV_RANK

    @pl.when(pl.program_id(0) == 0)
    def _():
        wlat_ref[...] = win_ref[...].astype(BF16)
        wqt_ref[...] = wuq_ref[...].T.astype(BF16)
        for hh in range(MLA_HEADS):
            k_cols = slice((NOPE + V_DIM) * hh, (NOPE + V_DIM) * hh + NOPE)
            v_cols = slice((NOPE + V_DIM) * hh + NOPE, (NOPE + V_DIM) * (hh + 1))
            wk_ref[:, NOPE * hh:NOPE * (hh + 1)] = wukv_ref[:, k_cols].astype(BF16)
            wvt_ref[V_DIM * hh:V_DIM * (hh + 1), :] = wukv_ref[:, v_cols].T.astype(BF16)

    xb = x_ref[...].astype(BF16)
    h = _dot_nt(xb, wlat_ref[...])
    cqn = _rms(h[:, :Q_RANK], gq_ref[...]).astype(BF16)
    ckvn = _rms(h[:, Q_RANK:latent], gkv_ref[...]).astype(BF16)

    ang_t = invf_ref[...] * posr_ref[...].astype(F32)
    cos_t, sin_t = jnp.cos(ang_t), jnp.sin(ang_t)
    table = jnp.concatenate([cos_t, cos_t, sin_t, sin_t], axis=0).T
    k_r = h[:, latent:latent + ROPE]
    kr_ref[0] = (jnp.concatenate([k_r, k_r], axis=1) * table).astype(BF16)

    qt = _dot_nt(wqt_ref[...], cqn)
    for hh in range(MLA_HEADS):
        base = QK_DIM * hh
        r1 = qt[base + NOPE:base + NOPE + half]
        r2 = qt[base + NOPE + half:base + QK_DIM]
        qr1 = (r1 * cos_t - r2 * sin_t).astype(BF16)
        qr2 = (r2 * cos_t + r1 * sin_t).astype(BF16)
        qt_ref[0, hh, 0:NOPE, :] = qt[base:base + NOPE].astype(BF16)
        qt_ref[0, hh, NOPE:NOPE + half, :] = qr1
        qt_ref[0, hh, NOPE + half:NOPE + 2 * half, :] = qr2
        qt_ref[0, hh, NOPE + 2 * half:NOPE + 3 * half, :] = qr2
        qt_ref[0, hh, NOPE + 3 * half:QK_PAD, :] = -qr1

    kn = _dot(ckvn, wk_ref[...])
    vt = _dot_nt(wvt_ref[...], ckvn)
    ones_row = (lax.broadcasted_iota(jnp.int32, (V_ROWS - V_DIM, tk), 0) == 0).astype(BF16)
    for hh in range(MLA_HEADS):
        kn_ref[0, hh] = kn[:, NOPE * hh:NOPE * hh + NOPE].astype(BF16)
        for c in range(tm // tk):
            vt_ref[0, hh, c, 0:V_DIM, :] = vt[V_DIM * hh:V_DIM * hh + V_DIM, tk * c:tk * c + tk].astype(BF16)
            vt_ref[0, hh, c, V_DIM:V_ROWS, :] = ones_row

def _convmix_kernel(x_ref, w_ref, cw_ref, wo_ref, o_ref, wob_ref, zs_ref, *, tc, tiles_per_seq):
    @pl.when(pl.program_id(0) % tiles_per_seq == 0)
    def _():
        zs_ref[0:HALO, :] = jnp.zeros((HALO, CONV_WIDTH), F32)

    wob_ref[...] = wo_ref[...].astype(BF16)

    xb = x_ref[...].astype(BF16)
    for c in range(CONV_WIDTH // tc):
        cols = slice(c * tc, (c + 1) * tc)
        gate_b = _dot_nt(xb, w_ref[c * tc:(c + 1) * tc, :])
        gate_c = _dot_nt(xb, w_ref[CONV_WIDTH + c * tc:CONV_WIDTH + (c + 1) * tc, :])
        x_c = _dot_nt(xb, w_ref[2 * CONV_WIDTH + c * tc:2 * CONV_WIDTH + (c + 1) * tc, :])
        y = _causal_conv3(gate_c * x_c, zs_ref, cw_ref[:, cols], cols)
        o_ref[:, cols] = (gate_b * y).astype(BF16)


def _attn_kernel(qt_ref, kn_ref, kr_ref, vt_ref, *refs, tq, tk, heads, lookahead, n_wc):
    wc_refs = refs[:n_wc]
    o_ref, wconv_ref, m_ref, acc_ref = refs[n_wc:]
    qi = pl.program_id(1)
    per_q = tq // tk
    trip_blocks = 2 * per_q
    m_ref[...] = jnp.full(m_ref.shape, NEG, F32)
    acc_ref[...] = jnp.zeros(acc_ref.shape, F32)

    for n, wc_ref in enumerate(wc_refs):
        wconv_ref[n * WCHUNK:(n + 1) * WCHUNK, :] = wc_ref[...].astype(BF16)

    def run(chains):
        def scores(chain):
            g, j, c0, _ = chain
            start = pl.multiple_of(j * tk, tk)
            k = jnp.concatenate([kn_ref[0, g, pl.ds(start, tk), :], kr_ref[0, pl.ds(start, tk), :]], axis=1)
            return _dot(k, qt_ref[0, g, :, c0:])

        ahead = [scores(ch) for ch in chains[:lookahead]]
        for i, (g, j, c0, mask) in enumerate(chains):
            s = ahead.pop(0)
            if i + lookahead < len(chains):
                ahead.append(scores(chains[i + lookahead]))
            if mask is not None:
                s = jnp.where(mask, s, NEG)
            m = m_ref[g, :, c0:]
            m_new = jnp.maximum(m, jnp.max(s, axis=0, keepdims=True))
            p = jnp.exp2(s - m_new)
            a = jnp.exp2(m - m_new)
            m_ref[g, :, c0:] = m_new
            acc_ref[g, :, c0:] = a * acc_ref[g, :, c0:] + _dot(vt_ref[0, g, j], p.astype(BF16))

    def full_blocks(first, count):
        return [(g, first + b, 0, None) for b in range(count) for g in range(heads)]

    def below_diagonal(t, c):
        run(full_blocks(trip_blocks * t, trip_blocks))
        return c

    n_below = qi * per_q
    lax.fori_loop(0, n_below // trip_blocks, below_diagonal, 0)

    @pl.when(n_below % trip_blocks != 0)
    def _():
        run(full_blocks(n_below - per_q, per_q))

    diag = []
    for b in range(per_q):
        c0 = b * tk
        key = lax.broadcasted_iota(jnp.int32, (tk, tq - c0), 0)
        qry = lax.broadcasted_iota(jnp.int32, (tk, tq - c0), 1)
        diag += [(g, per_q * qi + b, c0, key <= qry) for g in range(heads)]
    run(diag)
    for g in range(heads):
        out_t = acc_ref[g, 0:V_DIM, :] / acc_ref[g, V_DIM:V_DIM + 1, :]
        o_ref[0, :, V_DIM * g:V_DIM * g + V_DIM] = out_t.T.astype(BF16)


def _outproj_kernel(a_ref, c_ref, x_ref, w_ref, g_ref, b_ref, o32_ref, o16_ref, *, tm, rc):
    for r in range(tm // rc):
        rows = slice(r * rc, (r + 1) * rc)
        mixed = jnp.concatenate([a_ref[rows, :], c_ref[rows, :]], axis=1)
        y = _dot(mixed, w_ref[...]) + ALPHA * x_ref[rows, :]
        out = _layer_norm(y, g_ref[...], b_ref[...])
        o32_ref[rows, :] = out
        o16_ref[rows, :] = out.astype(BF16)


def _ffn_up_kernel(x_ref, wg_ref, wu_ref, cwg_ref, cwu_ref, wd_ref, o_ref, wdb_ref,
                   wgb_ref, wub_ref, gs_ref, us_ref, *, tn, tiles_per_seq):
    mi = pl.program_id(1)

    @pl.when(mi == 0)
    def _():
        wgb_ref[...] = wg_ref[...].astype(BF16)
        wub_ref[...] = wu_ref[...].astype(BF16)

    @pl.when(mi % tiles_per_seq == 0)
    def _():
        gs_ref[0:HALO, :] = jnp.zeros((HALO, tn), F32)
        us_ref[0:HALO, :] = jnp.zeros((HALO, tn), F32)

    wdb_ref[...] = wd_ref[...].astype(BF16)

    xb = x_ref[...]
    cols = slice(0, tn)
    g = _causal_conv3(_dot(xb, wgb_ref[...]), gs_ref, cwg_ref[...], cols)
    u = _causal_conv3(_dot(xb, wub_ref[...]), us_ref, cwu_ref[...], cols)
    o_ref[...] = (g / (1.0 + jnp.exp(-g)) * u).astype(BF16)


def _ffn_down_kernel(h_ref, w_ref, x_ref, g_ref, b_ref, o_ref, *, tm, rc):
    for r in range(tm // rc):
        rows = slice(r * rc, (r + 1) * rc)
        y = _dot(h_ref[rows, :], w_ref[...]) + ALPHA * x_ref[rows, :]
        o_ref[rows, :] = _layer_norm(y, g_ref[...], b_ref[...])


def _params(n_grid_axes):
    return pltpu.CompilerParams(dimension_semantics=("arbitrary",) * n_grid_axes, vmem_limit_bytes=VMEM_LIMIT)


def _layer(x2, pos, w_in, q_norm_g, w_uq, kv_norm_g, w_ukv, conv_w, w_out, ln1_g, ln1_b,
           w_gate_up, ffn_conv_w, w_down, ln2_g, ln2_b, batch, seq):
    T = batch * seq
    for tile in (TM_LATENT, TQ_ATTN, TM_CONVMIX, TM_OUTPROJ, TM_FFN_UP, TM_FFN_DOWN):
        assert seq % tile == 0, (seq, tile)
    assert TM_LATENT % TK_ATTN == 0 and TQ_ATTN % TK_ATTN == 0
    assert x2.shape == (T, D_MODEL) and w_in.shape == (D_MODEL, LAT_ROWS - LANES + ROPE + 3 * CONV_WIDTH)

    half = ROPE // 2
    gq = (q_norm_g * Q_SCALE).reshape(1, Q_RANK)
    gkv = kv_norm_g.reshape(1, KV_RANK)
    inv_freq = (ROPE_THETA ** (-jnp.arange(0, ROPE, 2, dtype=F32) / ROPE)).reshape(half, 1)
    posr = pos.reshape(1, T)

    tm, tk = TM_LATENT, TK_ATTN
    nblk = seq // tm
    full = lambda shape: pl.BlockSpec(shape, lambda *_: (0,) * len(shape), pipeline_mode=pl.Buffered(1))
    n1 = T // tm
    w_in_t = w_in.T
    qt, kn, kr, vt = pl.pallas_call(
        functools.partial(_latent_kernel, tm=tm, tk=tk),
        grid=(n1,),
        in_specs=[
            pl.BlockSpec((tm, D_MODEL), lambda i: (i, 0)),
            pl.BlockSpec((1, tm), lambda i: (0, i)),
            full((half, 1)),
            full((LAT_ROWS, D_MODEL)),
            full((1, Q_RANK)), full((1, KV_RANK)),
            full((Q_RANK, MLA_HEADS * QK_DIM)),
            full((KV_RANK, MLA_HEADS * (NOPE + V_DIM))),
        ],
        out_specs=[
            pl.BlockSpec((1, MLA_HEADS, QK_PAD, tm), lambda i: (i // nblk, 0, 0, i % nblk)),
            pl.BlockSpec((1, MLA_HEADS, tm, NOPE), lambda i: (i // nblk, 0, i % nblk, 0)),
            pl.BlockSpec((1, tm, ROPE_LANES), lambda i: (i // nblk, i % nblk, 0)),
            pl.BlockSpec((1, MLA_HEADS, tm // tk, V_ROWS, tk), lambda i: (i // nblk, 0, i % nblk, 0, 0)),
        ],
        out_shape=[
            jax.ShapeDtypeStruct((batch, MLA_HEADS, QK_PAD, seq), BF16),
            jax.ShapeDtypeStruct((batch, MLA_HEADS, seq, NOPE), BF16),
            jax.ShapeDtypeStruct((batch, seq, ROPE_LANES), BF16),
            jax.ShapeDtypeStruct((batch, MLA_HEADS, seq // tk, V_ROWS, tk), BF16),
        ],
        scratch_shapes=[pltpu.VMEM((LAT_ROWS, D_MODEL), BF16),
                        pltpu.VMEM((MLA_HEADS * QK_DIM, Q_RANK), BF16),
                        pltpu.VMEM((KV_RANK, MLA_HEADS * NOPE), BF16),
                        pltpu.VMEM((MLA_HEADS * V_DIM, KV_RANK), BF16)],
        compiler_params=_params(1),
        name="latent",
    )(x2, posr, inv_freq, w_in_t, gq, gkv, w_uq, w_ukv)

    tq = TQ_ATTN
    n_attn = batch * (seq // tq)
    conv_row0 = Q_RANK + KV_RANK + ROPE
    n_wc = 3 * CONV_WIDTH // (WCHUNK * n_attn)
    assert conv_row0 % WCHUNK == 0 and 3 * CONV_WIDTH == n_wc * WCHUNK * n_attn
    step = lambda b, i: b * (seq // tq) + i
    wchunk = lambda n: pl.BlockSpec((WCHUNK, D_MODEL),
                                    lambda b, i: (conv_row0 // WCHUNK + n_wc * step(b, i) + n, 0))
    attn, wconv = pl.pallas_call(
        functools.partial(_attn_kernel, tq=tq, tk=tk, heads=MLA_HEADS, lookahead=ATTN_LOOKAHEAD, n_wc=n_wc),
        grid=(batch, seq // tq),
        in_specs=[
            pl.BlockSpec((1, MLA_HEADS, QK_PAD, tq), lambda b, i: (b, 0, 0, i)),
            pl.BlockSpec((1, MLA_HEADS, seq, NOPE), lambda b, i: (b, 0, 0, 0)),
            pl.BlockSpec((1, seq, ROPE_LANES), lambda b, i: (b, 0, 0)),
            pl.BlockSpec((1, MLA_HEADS, seq // tk, V_ROWS, tk), lambda b, i: (b, 0, 0, 0, 0)),
            *[wchunk(n) for n in range(n_wc)],
        ],
        out_specs=[pl.BlockSpec((1, tq, MLA_WIDTH), lambda b, i: (b, i, 0)),
                   pl.BlockSpec((n_wc * WCHUNK, D_MODEL), lambda b, i: (step(b, i), 0))],
        out_shape=[jax.ShapeDtypeStruct((batch, seq, MLA_WIDTH), BF16),
                   jax.ShapeDtypeStruct((3 * CONV_WIDTH, D_MODEL), BF16)],
        scratch_shapes=[pltpu.VMEM((MLA_HEADS, 1, tq), F32), pltpu.VMEM((MLA_HEADS, V_ROWS, tq), F32)],
        compiler_params=_params(2),
        name="attn",
    )(qt, kn, kr, vt, *([w_in_t] * n_wc))
    attn = attn.reshape(T, MLA_WIDTH)

    tm2 = TM_CONVMIX
    n2 = T // tm2
    wo_rows = D_MODEL // n2
    conv_out, w_out_b = pl.pallas_call(
        functools.partial(_convmix_kernel, tc=TC_CONVMIX, tiles_per_seq=seq // tm2),
        grid=(n2,),
        in_specs=[
            pl.BlockSpec((tm2, D_MODEL), lambda i: (i, 0)),
            full((3 * CONV_WIDTH, D_MODEL)),
            full((3, CONV_WIDTH)),
            pl.BlockSpec((wo_rows, D_MODEL), lambda i: (i, 0)),
        ],
        out_specs=[pl.BlockSpec((tm2, CONV_WIDTH), lambda i: (i, 0)),
                   pl.BlockSpec((wo_rows, D_MODEL), lambda i: (i, 0))],
        out_shape=[jax.ShapeDtypeStruct((T, CONV_WIDTH), BF16),
                   jax.ShapeDtypeStruct((D_MODEL, D_MODEL), BF16)],
        scratch_shapes=[pltpu.VMEM((2 * HALO, CONV_WIDTH), F32)],
        compiler_params=_params(1),
        name="convmix",
    )(x2, wconv, conv_w, w_out)

    tm4 = TM_OUTPROJ
    x1, x1b = pl.pallas_call(
        functools.partial(_outproj_kernel, tm=tm4, rc=RC_OUTPROJ),
        grid=(T // tm4,),
        in_specs=[
            pl.BlockSpec((tm4, MLA_WIDTH), lambda i: (i, 0)),
            pl.BlockSpec((tm4, CONV_WIDTH), lambda i: (i, 0)),
            pl.BlockSpec((tm4, D_MODEL), lambda i: (i, 0)),
            full((D_MODEL, D_MODEL)), full((1, D_MODEL)), full((1, D_MODEL)),
        ],
        out_specs=[pl.BlockSpec((tm4, D_MODEL), lambda i: (i, 0)), pl.BlockSpec((tm4, D_MODEL), lambda i: (i, 0))],
        out_shape=[jax.ShapeDtypeStruct((T, D_MODEL), F32), jax.ShapeDtypeStruct((T, D_MODEL), BF16)],
        compiler_params=_params(1),
        name="outproj",
    )(attn, conv_out, x2, w_out_b, ln1_g.reshape(1, D_MODEL), ln1_b.reshape(1, D_MODEL))

    tm5, tn5 = TM_FFN_UP, TN_FFN_UP
    nn5 = FFN_DIM // tn5
    nm5 = T // tm5
    wd_rows = FFN_DIM // (nn5 * nm5)
    h2, w_down_b = pl.pallas_call(
        functools.partial(_ffn_up_kernel, tn=tn5, tiles_per_seq=seq // tm5),
        grid=(nn5, nm5),
        in_specs=[
            pl.BlockSpec((tm5, D_MODEL), lambda n, m: (m, 0)),
            pl.BlockSpec((D_MODEL, tn5), lambda n, m: (0, n)),
            pl.BlockSpec((D_MODEL, tn5), lambda n, m: (0, n + nn5)),
            pl.BlockSpec((3, tn5), lambda n, m: (0, n)),
            pl.BlockSpec((3, tn5), lambda n, m: (0, n + nn5)),
            pl.BlockSpec((wd_rows, D_MODEL), lambda n, m: (n * nm5 + m, 0)),
        ],
        out_specs=[pl.BlockSpec((tm5, tn5), lambda n, m: (m, n)),
                   pl.BlockSpec((wd_rows, D_MODEL), lambda n, m: (n * nm5 + m, 0))],
        out_shape=[jax.ShapeDtypeStruct((T, FFN_DIM), BF16),
                   jax.ShapeDtypeStruct((FFN_DIM, D_MODEL), BF16)],
        scratch_shapes=[pltpu.VMEM((D_MODEL, tn5), BF16), pltpu.VMEM((D_MODEL, tn5), BF16),
                        pltpu.VMEM((2 * HALO, tn5), F32), pltpu.VMEM((2 * HALO, tn5), F32)],
        compiler_params=_params(2),
        name="ffn_up",
    )(x1b, w_gate_up, w_gate_up, ffn_conv_w, ffn_conv_w, w_down)

    tm6 = TM_FFN_DOWN
    out = pl.pallas_call(
        functools.partial(_ffn_down_kernel, tm=tm6, rc=RC_FFN_DOWN),
        grid=(T // tm6,),
        in_specs=[
            pl.BlockSpec((tm6, FFN_DIM), lambda m: (m, 0)),
            full((FFN_DIM, D_MODEL)),
            pl.BlockSpec((tm6, D_MODEL), lambda m: (m, 0)),
            full((1, D_MODEL)), full((1, D_MODEL)),
        ],
        out_specs=pl.BlockSpec((tm6, D_MODEL), lambda m: (m, 0)),
        out_shape=jax.ShapeDtypeStruct((T, D_MODEL), F32),
        compiler_params=_params(1),
        name="ffn_down",
    )(h2, w_down_b, x1, ln2_g.reshape(1, D_MODEL), ln2_b.reshape(1, D_MODEL))
    return out


def kernel(x, positions, w_in, q_norm_g, w_uq, kv_norm_g, w_ukv, conv_w, w_out, ln1_g, ln1_b,
           w_gate_up, ffn_conv_w, w_down, ln2_g, ln2_b):
    batch, seq, _ = x.shape
    x2 = x.reshape(batch * seq, D_MODEL)
    for l in range(DEPTH):
        x2 = _layer(x2, positions, w_in[l], q_norm_g[l], w_uq[l], kv_norm_g[l], w_ukv[l], conv_w[l],
                    w_out[l], ln1_g[l], ln1_b[l], w_gate_up[l], ffn_conv_w[l], w_down[l], ln2_g[l], ln2_b[l],
                    batch, seq)
    return x2.reshape(batch, seq, D_MODEL)
```

```python
import functools
import math

import jax
import jax.numpy as jnp
from jax import lax
from jax.experimental import pallas as pl
from jax.experimental.pallas import tpu as pltpu

LANES = 128
F32_SUBLANES = 8
BF16_SUBLANES = 2 * F32_SUBLANES
V7X_VMEM_BYTES = 64 * 1024 * 1024
VMEM_LIMIT = V7X_VMEM_BYTES - 8 * 1024 * 1024

D_MODEL = 2048
MLA_HEADS = 8
NOPE = 128
ROPE = 64
V_DIM = 128
V_ROWS = V_DIM + BF16_SUBLANES
MLA_WIDTH = MLA_HEADS * V_DIM
Q_RANK = 512
KV_RANK = 256
CONV_WIDTH = 1024
FFN_DIM = 5632
ROPE_THETA = 10000.0
LN_EPS = 1e-5
RMS_EPS = 1e-6
DEPTH = 1
ALPHA = (2.0 * DEPTH) ** 0.25
QK_DIM = NOPE + ROPE
Q_SCALE = (QK_DIM ** -0.5) * math.log2(math.e)
ROPE_LANES = 2 * ROPE
QK_PAD = NOPE + ROPE_LANES
LAT_ROWS = Q_RANK + KV_RANK + LANES
HALO = F32_SUBLANES
WCHUNK = 64
NEG = -1e30

TM_LATENT = 512
TK_ATTN = 256
TQ_ATTN = 512
ATTN_LOOKAHEAD = 2
TM_CONVMIX, TC_CONVMIX = 1024, 256
TM_OUTPROJ, RC_OUTPROJ = 512, 128
TM_FFN_UP, TN_FFN_UP = 1024, 512
TM_FFN_DOWN, RC_FFN_DOWN = 512, 128

F32 = jnp.float32
BF16 = jnp.bfloat16


def _dot(a, b):
    return jnp.dot(a, b, preferred_element_type=F32)


def _dot_nt(a, b):
    return lax.dot_general(a, b, (((1,), (1,)), ((), ())), preferred_element_type=F32)


def _rms(x, g):
    return x * lax.rsqrt(jnp.mean(x * x, axis=-1, keepdims=True) + RMS_EPS) * g


def _layer_norm(y, g, b):
    mu = jnp.mean(y, axis=-1, keepdims=True)
    d = y - mu
    var = jnp.mean(d * d, axis=-1, keepdims=True)
    return d * lax.rsqrt(var + LN_EPS) * g + b


def _causal_conv3(z, hs_ref, cw, cols):
    n = z.shape[0]
    w0, w1, w2 = cw[0:1, :], cw[1:2, :], cw[2:3, :]
    y = w2 * z + w1 * pltpu.roll(z, 1, axis=0) + w0 * pltpu.roll(z, 2, axis=0)
    head = z[0:HALO]
    hs_ref[HALO:2 * HALO, cols] = head
    y_head = w2 * head + w1 * hs_ref[HALO - 1:2 * HALO - 1, cols] + w0 * hs_ref[HALO - 2:2 * HALO - 2, cols]
    hs_ref[0:HALO, cols] = z[n - HALO:n]
    return jnp.concatenate([y_head, y[HALO:]], axis=0)


def _latent_kernel(x_ref, posr_ref, invf_ref, win_ref, gq_ref, gkv_ref, wuq_ref, wukv_ref,
                   qt_ref, kn_ref, kr_ref, vt_ref, wlat_ref, wqt_ref, wk_ref, wvt_ref, *, tm, tk):
    half = ROPE // 2
    latent = Q_RANK + KV_RANK

    @pl.when(pl.program_id(0) == 0)
    def _():
        wlat_ref[...] = win_ref[...].astype(BF16)
        wqt_ref[...] = wuq_ref[...].T.astype(BF16)
        for hh in range(MLA_HEADS):
            k_cols = slice((NOPE + V_DIM) * hh, (NOPE + V_DIM) * hh + NOPE)
            v_cols = slice((NOPE + V_DIM) * hh + NOPE, (NOPE + V_DIM) * (hh + 1))
            wk_ref[:, NOPE * hh:NOPE * (hh + 1)] = wukv_ref[:, k_cols].astype(BF16)
            wvt_ref[V_DIM * hh:V_DIM * (hh + 1), :] = wukv_ref[:, v_cols].T.astype(BF16)

    xb = x_ref[...].astype(BF16)
    h = _dot_nt(xb, wlat_ref[...])
    cqn = _rms(h[:, :Q_RANK], gq_ref[...] * Q_SCALE).astype(BF16)
    ckvn = _rms(h[:, Q_RANK:latent], gkv_ref[...]).astype(BF16)

    ang_t = invf_ref[...] * posr_ref[...].astype(F32)
    cos_t, sin_t = jnp.cos(ang_t), jnp.sin(ang_t)
    table = jnp.concatenate([cos_t, cos_t, sin_t, sin_t], axis=0).T
    k_r = h[:, latent:latent + ROPE]
    kr_ref[0] = (jnp.concatenate([k_r, k_r], axis=1) * table).astype(BF16)

    qt = _dot_nt(wqt_ref[...], cqn)
    for hh in range(MLA_HEADS):
        base = QK_DIM * hh
        r1 = qt[base + NOPE:base + NOPE + half]
        r2 = qt[base + NOPE + half:base + QK_DIM]
        qr1 = (r1 * cos_t - r2 * sin_t).astype(BF16)
        qr2 = (r2 * cos_t + r1 * sin_t).astype(BF16)
        qt_ref[0, hh, 0:NOPE, :] = qt[base:base + NOPE].astype(BF16)
        qt_ref[0, hh, NOPE:NOPE + half, :] = qr1
        qt_ref[0, hh, NOPE + half:NOPE + 2 * half, :] = qr2
        qt_ref[0, hh, NOPE + 2 * half:NOPE + 3 * half, :] = qr2
        qt_ref[0, hh, NOPE + 3 * half:QK_PAD, :] = -qr1

    kn = _dot(ckvn, wk_ref[...])
    vt = _dot_nt(wvt_ref[...], ckvn)
    ones_row = (lax.broadcasted_iota(jnp.int32, (V_ROWS - V_DIM, tk), 0) == 0).astype(BF16)
    for hh in range(MLA_HEADS):
        kn_ref[0, hh] = kn[:, NOPE * hh:NOPE * hh + NOPE].astype(BF16)
        for c in range(tm // tk):
            vt_ref[0, hh, c, 0:V_DIM, :] = vt[V_DIM * hh:V_DIM * hh + V_DIM, tk * c:tk * c + tk].astype(BF16)
            vt_ref[0, hh, c, V_DIM:V_ROWS, :] = ones_row


def _convmix_kernel(x_ref, w_ref, cw_ref, wo_ref, o_ref, wob_ref, zs_ref, *, tc, tiles_per_seq):
    @pl.when(pl.program_id(0) % tiles_per_seq == 0)
    def _():
        zs_ref[0:HALO, :] = jnp.zeros((HALO, CONV_WIDTH), F32)

    wob_ref[...] = wo_ref[...].astype(BF16)

    xb = x_ref[...].astype(BF16)
    for c in range(CONV_WIDTH // tc):
        cols = slice(c * tc, (c + 1) * tc)
        gate_b = _dot_nt(xb, w_ref[c * tc:(c + 1) * tc, :])
        gate_c = _dot_nt(xb, w_ref[CONV_WIDTH + c * tc:CONV_WIDTH + (c + 1) * tc, :])
        x_c = _dot_nt(xb, w_ref[2 * CONV_WIDTH + c * tc:2 * CONV_WIDTH + (c + 1) * tc, :])
        y = _causal_conv3(gate_c * x_c, zs_ref, cw_ref[:, cols], cols)
        o_ref[:, cols] = (gate_b * y).astype(BF16)


def _attn_kernel(qt_ref, kn_ref, kr_ref, vt_ref, *refs, tq, tk, heads, lookahead, n_wc):
    wc_refs = refs[:n_wc]
    o_ref, wconv_ref, m_ref, acc_ref = refs[n_wc:]
    qi = pl.program_id(1)
    per_q = tq // tk
    trip_blocks = 2 * per_q
    m_ref[...] = jnp.full(m_ref.shape, NEG, F32)
    acc_ref[...] = jnp.zeros(acc_ref.shape, F32)

    for n, wc_ref in enumerate(wc_refs):
        wconv_ref[n * WCHUNK:(n + 1) * WCHUNK, :] = wc_ref[...].astype(BF16)

    def run(chains):
        def scores(chain):
            g, j, c0, _ = chain
            start = pl.multiple_of(j * tk, tk)
            k = jnp.concatenate([kn_ref[0, g, pl.ds(start, tk), :], kr_ref[0, pl.ds(start, tk), :]], axis=1)
            return _dot(k, qt_ref[0, g, :, c0:])

        ahead = [scores(ch) for ch in chains[:lookahead]]
        for i, (g, j, c0, mask) in enumerate(chains):
            s = ahead.pop(0)
            if i + lookahead < len(chains):
                ahead.append(scores(chains[i + lookahead]))
            if mask is not None:
                s = jnp.where(mask, s, NEG)
            m = m_ref[g, :, c0:]
            m_new = jnp.maximum(m, jnp.max(s, axis=0, keepdims=True))
            p = jnp.exp2(s - m_new)
            a = jnp.exp2(m - m_new)
            m_ref[g, :, c0:] = m_new
            acc_ref[g, :, c0:] = a * acc_ref[g, :, c0:] + _dot(vt_ref[0, g, j], p.astype(BF16))

    def full_blocks(first, count):
        return [(g, first + b, 0, None) for b in range(count) for g in range(heads)]

    def below_diagonal(t, c):
        run(full_blocks(trip_blocks * t, trip_blocks))
        return c

    n_below = qi * per_q
    lax.fori_loop(0, n_below // trip_blocks, below_diagonal, 0)

    @pl.when(n_below % trip_blocks != 0)
    def _():
        run(full_blocks(n_below - per_q, per_q))

    diag = []
    for b in range(per_q):
        c0 = b * tk
        key = lax.broadcasted_iota(jnp.int32, (tk, tq - c0), 0)
        qry = lax.broadcasted_iota(jnp.int32, (tk, tq - c0), 1)
        diag += [(g, per_q * qi + b, c0, key <= qry) for g in range(heads)]
    run(diag)
    for g in range(heads):
        out_t = acc_ref[g, 0:V_DIM, :] / acc_ref[g, V_DIM:V_DIM + 1, :]
        o_ref[0, :, V_DIM * g:V_DIM * g + V_DIM] = out_t.T.astype(BF16)


def _outproj_kernel(a_ref, c_ref, x_ref, w_ref, g_ref, b_ref, o32_ref, o16_ref, *, tm, rc):
    for r in range(tm // rc):
        rows = slice(r * rc, (r + 1) * rc)
        mixed = jnp.concatenate([a_ref[rows, :], c_ref[rows, :]], axis=1)
        y = _dot(mixed, w_ref[...]) + ALPHA * x_ref[rows, :]
        out = _layer_norm(y, g_ref[...], b_ref[...])
        o32_ref[rows, :] = out
        o16_ref[rows, :] = out.astype(BF16)


def _ffn_up_kernel(x_ref, wg_ref, wu_ref, cwg_ref, cwu_ref, wd_ref, o_ref, wdb_ref,
                   wgb_ref, wub_ref, gs_ref, us_ref, *, tn, tiles_per_seq):
    mi = pl.program_id(1)

    @pl.when(mi == 0)
    def _():
        wgb_ref[...] = wg_ref[...].astype(BF16)
        wub_ref[...] = wu_ref[...].astype(BF16)

    @pl.when(mi % tiles_per_seq == 0)
    def _():
        gs_ref[0:HALO, :] = jnp.zeros((HALO, tn), F32)
        us_ref[0:HALO, :] = jnp.zeros((HALO, tn), F32)

    wdb_ref[...] = wd_ref[...].astype(BF16)

    xb = x_ref[...]
    cols = slice(0, tn)
    g = _causal_conv3(_dot(xb, wgb_ref[...]), gs_ref, cwg_ref[...], cols)
    u = _causal_conv3(_dot(xb, wub_ref[...]), us_ref, cwu_ref[...], cols)
    o_ref[...] = (g / (1.0 + jnp.exp(-g)) * u).astype(BF16)


def _ffn_down_kernel(h_ref, w_ref, x_ref, g_ref, b_ref, o_ref, *, tm, rc):
    for r in range(tm // rc):
        rows = slice(r * rc, (r + 1) * rc)
        y = _dot(h_ref[rows, :], w_ref[...]) + ALPHA * x_ref[rows, :]
        o_ref[rows, :] = _layer_norm(y, g_ref[...], b_ref[...])


def _params(n_grid_axes):
    return pltpu.CompilerParams(dimension_semantics=("arbitrary",) * n_grid_axes, vmem_limit_bytes=VMEM_LIMIT)


def _layer(x2, pos, w_in, q_norm_g, w_uq, kv_norm_g, w_ukv, conv_w, w_out, ln1_g, ln1_b,
           w_gate_up, ffn_conv_w, w_down, ln2_g, ln2_b, batch, seq):
    T = batch * seq
    for tile in (TM_LATENT, TQ_ATTN, TM_CONVMIX, TM_OUTPROJ, TM_FFN_UP, TM_FFN_DOWN):
        assert seq % tile == 0, (seq, tile)
    assert TM_LATENT % TK_ATTN == 0 and TQ_ATTN % TK_ATTN == 0
    assert x2.shape == (T, D_MODEL) and w_in.shape == (D_MODEL, LAT_ROWS - LANES + ROPE + 3 * CONV_WIDTH)

    half = ROPE // 2
    gq = q_norm_g.reshape(1, Q_RANK)
    gkv = kv_norm_g.reshape(1, KV_RANK)
    inv_freq = (ROPE_THETA ** (-jnp.arange(0, ROPE, 2, dtype=F32) / ROPE)).reshape(half, 1)
    posr = pos.reshape(1, T)

    tm, tk = TM_LATENT, TK_ATTN
    nblk = seq // tm
    full = lambda shape: pl.BlockSpec(shape, lambda *_: (0,) * len(shape), pipeline_mode=pl.Buffered(1))
    n1 = T // tm
    w_in_t = w_in.T
    qt, kn, kr, vt = pl.pallas_call(
        functools.partial(_latent_kernel, tm=tm, tk=tk),
        grid=(n1,),
        in_specs=[
            pl.BlockSpec((tm, D_MODEL), lambda i: (i, 0)),
            pl.BlockSpec((1, tm), lambda i: (0, i)),
            full((half, 1)),
            full((LAT_ROWS, D_MODEL)),
            full((1, Q_RANK)), full((1, KV_RANK)),
            full((Q_RANK, MLA_HEADS * QK_DIM)),
            full((KV_RANK, MLA_HEADS * (NOPE + V_DIM))),
        ],
        out_specs=[
            pl.BlockSpec((1, MLA_HEADS, QK_PAD, tm), lambda i: (i // nblk, 0, 0, i % nblk)),
            pl.BlockSpec((1, MLA_HEADS, tm, NOPE), lambda i: (i // nblk, 0, i % nblk, 0)),
            pl.BlockSpec((1, tm, ROPE_LANES), lambda i: (i // nblk, i % nblk, 0)),
            pl.BlockSpec((1, MLA_HEADS, tm // tk, V_ROWS, tk), lambda i: (i // nblk, 0, i % nblk, 0, 0)),
        ],
        out_shape=[
            jax.ShapeDtypeStruct((batch, MLA_HEADS, QK_PAD, seq), BF16),
            jax.ShapeDtypeStruct((batch, MLA_HEADS, seq, NOPE), BF16),
            jax.ShapeDtypeStruct((batch, seq, ROPE_LANES), BF16),
            jax.ShapeDtypeStruct((batch, MLA_HEADS, seq // tk, V_ROWS, tk), BF16),
        ],
        scratch_shapes=[pltpu.VMEM((LAT_ROWS, D_MODEL), BF16),
                        pltpu.VMEM((MLA_HEADS * QK_DIM, Q_RANK), BF16),
                        pltpu.VMEM((KV_RANK, MLA_HEADS * NOPE), BF16),
                        pltpu.VMEM((MLA_HEADS * V_DIM, KV_RANK), BF16)],
        compiler_params=_params(1),
        name="latent",
    )(x2, posr, inv_freq, w_in_t, gq, gkv, w_uq, w_ukv)

    tq = TQ_ATTN
    n_attn = batch * (seq // tq)
    conv_row0 = Q_RANK + KV_RANK + ROPE
    n_wc = 3 * CONV_WIDTH // (WCHUNK * n_attn)
    assert conv_row0 % WCHUNK == 0 and 3 * CONV_WIDTH == n_wc * WCHUNK * n_attn
    step = lambda b, i: b * (seq // tq) + i
    wchunk = lambda n: pl.BlockSpec((WCHUNK, D_MODEL),
                                    lambda b, i: (conv_row0 // WCHUNK + n_wc * step(b, i) + n, 0))
    attn, wconv = pl.pallas_call(
        functools.partial(_attn_kernel, tq=tq, tk=tk, heads=MLA_HEADS, lookahead=ATTN_LOOKAHEAD, n_wc=n_wc),
        grid=(batch, seq // tq),
        in_specs=[
            pl.BlockSpec((1, MLA_HEADS, QK_PAD, tq), lambda b, i: (b, 0, 0, i)),
            pl.BlockSpec((1, MLA_HEADS, seq, NOPE), lambda b, i: (b, 0, 0, 0)),
            pl.BlockSpec((1, seq, ROPE_LANES), lambda b, i: (b, 0, 0)),
            pl.BlockSpec((1, MLA_HEADS, seq // tk, V_ROWS, tk), lambda b, i: (b, 0, 0, 0, 0)),
            *[wchunk(n) for n in range(n_wc)],
        ],
        out_specs=[pl.BlockSpec((1, tq, MLA_WIDTH), lambda b, i: (b, i, 0)),
                   pl.BlockSpec((n_wc * WCHUNK, D_MODEL), lambda b, i: (step(b, i), 0))],
        out_shape=[jax.ShapeDtypeStruct((batch, seq, MLA_WIDTH), BF16),
                   jax.ShapeDtypeStruct((3 * CONV_WIDTH, D_MODEL), BF16)],
        scratch_shapes=[pltpu.VMEM((MLA_HEADS, 1, tq), F32), pltpu.VMEM((MLA_HEADS, V_ROWS, tq), F32)],
        compiler_params=_params(2),
        name="attn",
    )(qt, kn, kr, vt, *([w_in_t] * n_wc))
    attn = attn.reshape(T, MLA_WIDTH)

    tm2 = TM_CONVMIX
    n2 = T // tm2
    wo_rows = D_MODEL // n2
    conv_out, w_out_b = pl.pallas_call(
        functools.partial(_convmix_kernel, tc=TC_CONVMIX, tiles_per_seq=seq // tm2),
        grid=(n2,),
        in_specs=[
            pl.BlockSpec((tm2, D_MODEL), lambda i: (i, 0)),
            full((3 * CONV_WIDTH, D_MODEL)),
            full((3, CONV_WIDTH)),
            pl.BlockSpec((wo_rows, D_MODEL), lambda i: (i, 0)),
        ],
        out_specs=[pl.BlockSpec((tm2, CONV_WIDTH), lambda i: (i, 0)),
                   pl.BlockSpec((wo_rows, D_MODEL), lambda i: (i, 0))],
        out_shape=[jax.ShapeDtypeStruct((T, CONV_WIDTH), BF16),
                   jax.ShapeDtypeStruct((D_MODEL, D_MODEL), BF16)],
        scratch_shapes=[pltpu.VMEM((2 * HALO, CONV_WIDTH), F32)],
        compiler_params=_params(1),
        name="convmix",
    )(x2, wconv, conv_w, w_out)

    tm4 = TM_OUTPROJ
    x1, x1b = pl.pallas_call(
        functools.partial(_outproj_kernel, tm=tm4, rc=RC_OUTPROJ),
        grid=(T // tm4,),
        in_specs=[
            pl.BlockSpec((tm4, MLA_WIDTH), lambda i: (i, 0)),
            pl.BlockSpec((tm4, CONV_WIDTH), lambda i: (i, 0)),
            pl.BlockSpec((tm4, D_MODEL), lambda i: (i, 0)),
            full((D_MODEL, D_MODEL)), full((1, D_MODEL)), full((1, D_MODEL)),
        ],
        out_specs=[pl.BlockSpec((tm4, D_MODEL), lambda i: (i, 0)), pl.BlockSpec((tm4, D_MODEL), lambda i: (i, 0))],
        out_shape=[jax.ShapeDtypeStruct((T, D_MODEL), F32), jax.ShapeDtypeStruct((T, D_MODEL), BF16)],
        compiler_params=_params(1),
        name="outproj",
    )(attn, conv_out, x2, w_out_b, ln1_g.reshape(1, D_MODEL), ln1_b.reshape(1, D_MODEL))

    tm5, tn5 = TM_FFN_UP, TN_FFN_UP
    nn5 = FFN_DIM // tn5
    nm5 = T // tm5
    wd_rows = FFN_DIM // (nn5 * nm5)
    h2, w_down_b = pl.pallas_call(
        functools.partial(_ffn_up_kernel, tn=tn5, tiles_per_seq=seq // tm5),
        grid=(nn5, nm5),
        in_specs=[
            pl.BlockSpec((tm5, D_MODEL), lambda n, m: (m, 0)),
            pl.BlockSpec((D_MODEL, tn5), lambda n, m: (0, n)),
            pl.BlockSpec((D_MODEL, tn5), lambda n, m: (0, n + nn5)),
            pl.BlockSpec((3, tn5), lambda n, m: (0, n)),
            pl.BlockSpec((3, tn5), lambda n, m: (0, n + nn5)),
            pl.BlockSpec((wd_rows, D_MODEL), lambda n, m: (n * nm5 + m, 0)),
        ],
        out_specs=[pl.BlockSpec((tm5, tn5), lambda n, m: (m, n)),
                   pl.BlockSpec((wd_rows, D_MODEL), lambda n, m: (n * nm5 + m, 0))],
        out_shape=[jax.ShapeDtypeStruct((T, FFN_DIM), BF16),
                   jax.ShapeDtypeStruct((FFN_DIM, D_MODEL), BF16)],
        scratch_shapes=[pltpu.VMEM((D_MODEL, tn5), BF16), pltpu.VMEM((D_MODEL, tn5), BF16),
                        pltpu.VMEM((2 * HALO, tn5), F32), pltpu.VMEM((2 * HALO, tn5), F32)],
        compiler_params=_params(2),
        name="ffn_up",
    )(x1b, w_gate_up, w_gate_up, ffn_conv_w, ffn_conv_w, w_down)

    tm6 = TM_FFN_DOWN
    out = pl.pallas_call(
        functools.partial(_ffn_down_kernel, tm=tm6, rc=RC_FFN_DOWN),
        grid=(T // tm6,),
        in_specs=[
            pl.BlockSpec((tm6, FFN_DIM), lambda m: (m, 0)),
            full((FFN_DIM, D_MODEL)),
            pl.BlockSpec((tm6, D_MODEL), lambda m: (m, 0)),
            full((1, D_MODEL)), full((1, D_MODEL)),
        ],
        out_specs=pl.BlockSpec((tm6, D_MODEL), lambda m: (m, 0)),
        out_shape=jax.ShapeDtypeStruct((T, D_MODEL), F32),
        compiler_params=_params(1),
        name="ffn_down",
    )(h2, w_down_b, x1, ln2_g.reshape(1, D_MODEL), ln2_b.reshape(1, D_MODEL))
    return out


def kernel(x, positions, w_in, q_norm_g, w_uq, kv_norm_g, w_ukv, conv_w, w_out, ln1_g, ln1_b,
           w_gate_up, ffn_conv_w, w_down, ln2_g, ln2_b):
    batch, seq, _ = x.shape
    x2 = x.reshape(batch * seq, D_MODEL)
    for l in range(DEPTH):
        x2 = _layer(x2, positions, w_in[l], q_norm_g[l], w_uq[l], kv_norm_g[l], w_ukv[l], conv_w[l],
                    w_out[l], ln1_g[l], ln1_b[l], w_gate_up[l], ffn_conv_w[l], w_down[l], ln2_g[l], ln2_b[l],
                    batch, seq)
    return x2.reshape(batch, seq, D_MODEL)
```

```python
import functools
import math

import jax
import jax.numpy as jnp
from jax import lax
from jax.experimental import pallas as pl
from jax.experimental.pallas import tpu as pltpu

LANES = 128
F32_SUBLANES = 8
BF16_SUBLANES = 2 * F32_SUBLANES
V7X_VMEM_BYTES = 64 * 1024 * 1024
VMEM_LIMIT = V7X_VMEM_BYTES - 8 * 1024 * 1024

D_MODEL = 2048
MLA_HEADS = 8
NOPE = 128
ROPE = 64
V_DIM = 128
V_ROWS = V_DIM + BF16_SUBLANES
MLA_WIDTH = MLA_HEADS * V_DIM
Q_RANK = 512
KV_RANK = 256
CONV_WIDTH = 1024
FFN_DIM = 5632
ROPE_THETA = 10000.0
LN_EPS = 1e-5
RMS_EPS = 1e-6
DEPTH = 1
ALPHA = (2.0 * DEPTH) ** 0.25
QK_DIM = NOPE + ROPE
Q_SCALE = (QK_DIM ** -0.5) * math.log2(math.e)
ROPE_LANES = 2 * ROPE
QK_PAD = NOPE + ROPE_LANES
LAT_ROWS = Q_RANK + KV_RANK + LANES
HALO = F32_SUBLANES
WCHUNK = 64
NEG = -1e30

TM_LATENT = 512
TK_ATTN = 256
TQ_ATTN = 512
ATTN_LOOKAHEAD = 2
TM_CONVMIX, TC_CONVMIX = 1024, 256
TM_OUTPROJ, RC_OUTPROJ = 512, 128
TM_FFN_UP, TN_FFN_UP = 1024, 512
TM_FFN_DOWN, RC_FFN_DOWN = 512, 128

F32 = jnp.float32
BF16 = jnp.bfloat16


def _dot(a, b):
    return jnp.dot(a, b, preferred_element_type=F32)


def _dot_nt(a, b):
    return lax.dot_general(a, b, (((1,), (1,)), ((), ())), preferred_element_type=F32)


def _rms(x, g):
    return x * lax.rsqrt(jnp.mean(x * x, axis=-1, keepdims=True) + RMS_EPS) * g


def _layer_norm(y, g, b):
    mu = jnp.mean(y, axis=-1, keepdims=True)
    d = y - mu
    var = jnp.mean(d * d, axis=-1, keepdims=True)
    return d * lax.rsqrt(var + LN_EPS) * g + b


def _causal_conv3(z, hs_ref, cw, cols):
    n = z.shape[0]
    w0, w1, w2 = cw[0:1, :], cw[1:2, :], cw[2:3, :]
    y = w2 * z + w1 * pltpu.roll(z, 1, axis=0) + w0 * pltpu.roll(z, 2, axis=0)
    head = z[0:HALO]
    hs_ref[HALO:2 * HALO, cols] = head
    y_head = w2 * head + w1 * hs_ref[HALO - 1:2 * HALO - 1, cols] + w0 * hs_ref[HALO - 2:2 * HALO - 2, cols]
    hs_ref[0:HALO, cols] = z[n - HALO:n]
    return jnp.concatenate([y_head, y[HALO:]], axis=0)


def _latent_kernel(x_ref, posr_ref, invf_ref, win_ref, gq_ref, gkv_ref, wuq_ref, wukv_ref,
                   qt_ref, kn_ref, kr_ref, vt_ref, wlat_ref, wqt_ref, wk_ref, wvt_ref, *, tm, tk):
    half = ROPE // 2
    latent = Q_RANK + KV_RANK

    @pl.when(pl.program_id(0) == 0)
    def _():
        wlat_ref[...] = win_ref[...].astype(BF16)
        wqt_ref[...] = wuq_ref[...].T.astype(BF16)
        for hh in range(MLA_HEADS):
            k_cols = slice((NOPE + V_DIM) * hh, (NOPE + V_DIM) * hh + NOPE)
            v_cols = slice((NOPE + V_DIM) * hh + NOPE, (NOPE + V_DIM) * (hh + 1))
            wk_ref[:, NOPE * hh:NOPE * (hh + 1)] = wukv_ref[:, k_cols].astype(BF16)
            wvt_ref[V_DIM * hh:V_DIM * (hh + 1), :] = wukv_ref[:, v_cols].T.astype(BF16)

    xb = x_ref[...].astype(BF16)
    h = _dot_nt(xb, wlat_ref[...])
    cqn = _rms(h[:, :Q_RANK], gq_ref[...] * Q_SCALE).astype(BF16)
    ckvn = _rms(h[:, Q_RANK:latent], gkv_ref[...]).astype(BF16)

    ang_t = invf_ref[...] * posr_ref[...].astype(F32)
    cos_t, sin_t = jnp.cos(ang_t), jnp.sin(ang_t)
    table = jnp.concatenate([cos_t, cos_t, sin_t, sin_t], axis=0).T
    k_r = h[:, latent:latent + ROPE]
    kr_ref[0] = (jnp.concatenate([k_r, k_r], axis=1) * table).astype(BF16)

    qt = _dot_nt(wqt_ref[...], cqn)
    for hh in range(MLA_HEADS):
        base = QK_DIM * hh
        r1 = qt[base + NOPE:base + NOPE + half]
        r2 = qt[base + NOPE + half:base + QK_DIM]
        qr1 = (r1 * cos_t - r2 * sin_t).astype(BF16)
        qr2 = (r2 * cos_t + r1 * sin_t).astype(BF16)
        qt_ref[0, hh, 0:NOPE, :] = qt[base:base + NOPE].astype(BF16)
        qt_ref[0, hh, NOPE:NOPE + half, :] = qr1
        qt_ref[0, hh, NOPE + half:QK_DIM, :] = qr2

    kn = _dot(ckvn, wk_ref[...])
    vt = _dot_nt(wvt_ref[...], ckvn)
    ones_row = (lax.broadcasted_iota(jnp.int32, (V_ROWS - V_DIM, tk), 0) == 0).astype(BF16)
    for hh in range(MLA_HEADS):
        kn_ref[0, hh] = kn[:, NOPE * hh:NOPE * hh + NOPE].astype(BF16)
        for c in range(tm // tk):
            vt_ref[0, hh, c, 0:V_DIM, :] = vt[V_DIM * hh:V_DIM * hh + V_DIM, tk * c:tk * c + tk].astype(BF16)
            vt_ref[0, hh, c, V_DIM:V_ROWS, :] = ones_row


def _convmix_kernel(x_ref, w_ref, cw_ref, wo_ref, o_ref, wob_ref, zs_ref, *, tc, tiles_per_seq):
    @pl.when(pl.program_id(0) % tiles_per_seq == 0)
    def _():
        zs_ref[0:HALO, :] = jnp.zeros((HALO, CONV_WIDTH), F32)

    wob_ref[...] = wo_ref[...].astype(BF16)

    xb = x_ref[...].astype(BF16)
    for c in range(CONV_WIDTH // tc):
        cols = slice(c * tc, (c + 1) * tc)
        gate_b = _dot_nt(xb, w_ref[c * tc:(c + 1) * tc, :])
        gate_c = _dot_nt(xb, w_ref[CONV_WIDTH + c * tc:CONV_WIDTH + (c + 1) * tc, :])
        x_c = _dot_nt(xb, w_ref[2 * CONV_WIDTH + c * tc:2 * CONV_WIDTH + (c + 1) * tc, :])
        y = _causal_conv3(gate_c * x_c, zs_ref, cw_ref[:, cols], cols)
        o_ref[:, cols] = (gate_b * y).astype(BF16)


def _attn_kernel(qt_ref, kn_ref, kr_ref, vt_ref, *refs, tq, tk, heads, lookahead, n_wc):
    wc_refs = refs[:n_wc]
    o_ref, wconv_ref, m_ref, acc_ref, qf_ref = refs[n_wc:]
    qi = pl.program_id(1)
    per_q = tq // tk
    trip_blocks = 2 * per_q
    half = ROPE // 2
    m_ref[...] = jnp.full(m_ref.shape, NEG, F32)
    acc_ref[...] = jnp.zeros(acc_ref.shape, F32)
    for g in range(heads):
        qf_ref[g, 0:QK_DIM, :] = qt_ref[0, g]
        qf_ref[g, QK_DIM:QK_DIM + half, :] = qt_ref[0, g, NOPE + half:QK_DIM, :]
        qf_ref[g, QK_DIM + half:QK_PAD, :] = -qt_ref[0, g, NOPE:NOPE + half, :]

    for n, wc_ref in enumerate(wc_refs):
        wconv_ref[n * WCHUNK:(n + 1) * WCHUNK, :] = wc_ref[...].astype(BF16)

    def run(chains):
        def scores(chain):
            g, j, c0, _ = chain
            start = pl.multiple_of(j * tk, tk)
            k = jnp.concatenate([kn_ref[0, g, pl.ds(start, tk), :], kr_ref[0, pl.ds(start, tk), :]], axis=1)
            return _dot(k, qf_ref[g, :, c0:])

        ahead = [scores(ch) for ch in chains[:lookahead]]
        for i, (g, j, c0, mask) in enumerate(chains):
            s = ahead.pop(0)
            if i + lookahead < len(chains):
                ahead.append(scores(chains[i + lookahead]))
            if mask is not None:
                s = jnp.where(mask, s, NEG)
            m = m_ref[g, :, c0:]
            m_new = jnp.maximum(m, jnp.max(s, axis=0, keepdims=True))
            p = jnp.exp2(s - m_new)
            a = jnp.exp2(m - m_new)
            m_ref[g, :, c0:] = m_new
            acc_ref[g, :, c0:] = a * acc_ref[g, :, c0:] + _dot(vt_ref[0, g, j], p.astype(BF16))

    def full_blocks(first, count):
        return [(g, first + b, 0, None) for b in range(count) for g in range(heads)]

    def below_diagonal(t, c):
        run(full_blocks(trip_blocks * t, trip_blocks))
        return c

    n_below = qi * per_q
    lax.fori_loop(0, n_below // trip_blocks, below_diagonal, 0)

    @pl.when(n_below % trip_blocks != 0)
    def _():
        run(full_blocks(n_below - per_q, per_q))

    diag = []
    for b in range(per_q):
        c0 = b * tk
        key = lax.broadcasted_iota(jnp.int32, (tk, tq - c0), 0)
        qry = lax.broadcasted_iota(jnp.int32, (tk, tq - c0), 1)
        diag += [(g, per_q * qi + b, c0, key <= qry) for g in range(heads)]
    run(diag)
    for g in range(heads):
        out_t = acc_ref[g, 0:V_DIM, :] / acc_ref[g, V_DIM:V_DIM + 1, :]
        o_ref[0, :, V_DIM * g:V_DIM * g + V_DIM] = out_t.T.astype(BF16)


def _outproj_kernel(a_ref, c_ref, x_ref, w_ref, g_ref, b_ref, o32_ref, o16_ref, *, tm, rc):
    for r in range(tm // rc):
        rows = slice(r * rc, (r + 1) * rc)
        mixed = jnp.concatenate([a_ref[rows, :], c_ref[rows, :]], axis=1)
        y = _dot(mixed, w_ref[...]) + ALPHA * x_ref[rows, :]
        out = _layer_norm(y, g_ref[...], b_ref[...])
        o32_ref[rows, :] = out
        o16_ref[rows, :] = out.astype(BF16)


def _ffn_up_kernel(x_ref, wg_ref, wu_ref, cwg_ref, cwu_ref, wd_ref, o_ref, wdb_ref,
                   wgb_ref, wub_ref, gs_ref, us_ref, *, tn, tiles_per_seq):
    mi = pl.program_id(1)

    @pl.when(mi == 0)
    def _():
        wgb_ref[...] = wg_ref[...].astype(BF16)
        wub_ref[...] = wu_ref[...].astype(BF16)

    @pl.when(mi % tiles_per_seq == 0)
    def _():
        gs_ref[0:HALO, :] = jnp.zeros((HALO, tn), F32)
        us_ref[0:HALO, :] = jnp.zeros((HALO, tn), F32)

    wdb_ref[...] = wd_ref[...].astype(BF16)

    xb = x_ref[...]
    cols = slice(0, tn)
    g = _causal_conv3(_dot(xb, wgb_ref[...]), gs_ref, cwg_ref[...], cols)
    u = _causal_conv3(_dot(xb, wub_ref[...]), us_ref, cwu_ref[...], cols)
    o_ref[...] = (g / (1.0 + jnp.exp(-g)) * u).astype(BF16)


def _ffn_down_kernel(h_ref, w_ref, x_ref, g_ref, b_ref, o_ref, *, tm, rc):
    for r in range(tm // rc):
        rows = slice(r * rc, (r + 1) * rc)
        y = _dot(h_ref[rows, :], w_ref[...]) + ALPHA * x_ref[rows, :]
        o_ref[rows, :] = _layer_norm(y, g_ref[...], b_ref[...])


def _params(n_grid_axes):
    return pltpu.CompilerParams(dimension_semantics=("arbitrary",) * n_grid_axes, vmem_limit_bytes=VMEM_LIMIT)


def _layer(x2, pos, w_in, q_norm_g, w_uq, kv_norm_g, w_ukv, conv_w, w_out, ln1_g, ln1_b,
           w_gate_up, ffn_conv_w, w_down, ln2_g, ln2_b, batch, seq):
    T = batch * seq
    for tile in (TM_LATENT, TQ_ATTN, TM_CONVMIX, TM_OUTPROJ, TM_FFN_UP, TM_FFN_DOWN):
        assert seq % tile == 0, (seq, tile)
    assert TM_LATENT % TK_ATTN == 0 and TQ_ATTN % TK_ATTN == 0
    assert x2.shape == (T, D_MODEL) and w_in.shape == (D_MODEL, LAT_ROWS - LANES + ROPE + 3 * CONV_WIDTH)

    half = ROPE // 2
    gq = q_norm_g.reshape(1, Q_RANK)
    gkv = kv_norm_g.reshape(1, KV_RANK)
    inv_freq = (ROPE_THETA ** (-jnp.arange(0, ROPE, 2, dtype=F32) / ROPE)).reshape(half, 1)
    posr = pos.reshape(1, T)

    tm, tk = TM_LATENT, TK_ATTN
    nblk = seq // tm
    full = lambda shape: pl.BlockSpec(shape, lambda *_: (0,) * len(shape), pipeline_mode=pl.Buffered(1))
    n1 = T // tm
    w_in_t = w_in.T
    qt, kn, kr, vt = pl.pallas_call(
        functools.partial(_latent_kernel, tm=tm, tk=tk),
        grid=(n1,),
        in_specs=[
            pl.BlockSpec((tm, D_MODEL), lambda i: (i, 0)),
            pl.BlockSpec((1, tm), lambda i: (0, i)),
            full((half, 1)),
            full((LAT_ROWS, D_MODEL)),
            full((1, Q_RANK)), full((1, KV_RANK)),
            full((Q_RANK, MLA_HEADS * QK_DIM)),
            full((KV_RANK, MLA_HEADS * (NOPE + V_DIM))),
        ],
        out_specs=[
            pl.BlockSpec((1, MLA_HEADS, QK_DIM, tm), lambda i: (i // nblk, 0, 0, i % nblk)),
            pl.BlockSpec((1, MLA_HEADS, tm, NOPE), lambda i: (i // nblk, 0, i % nblk, 0)),
            pl.BlockSpec((1, tm, ROPE_LANES), lambda i: (i // nblk, i % nblk, 0)),
            pl.BlockSpec((1, MLA_HEADS, tm // tk, V_ROWS, tk), lambda i: (i // nblk, 0, i % nblk, 0, 0)),
        ],
        out_shape=[
            jax.ShapeDtypeStruct((batch, MLA_HEADS, QK_DIM, seq), BF16),
            jax.ShapeDtypeStruct((batch, MLA_HEADS, seq, NOPE), BF16),
            jax.ShapeDtypeStruct((batch, seq, ROPE_LANES), BF16),
            jax.ShapeDtypeStruct((batch, MLA_HEADS, seq // tk, V_ROWS, tk), BF16),
        ],
        scratch_shapes=[pltpu.VMEM((LAT_ROWS, D_MODEL), BF16),
                        pltpu.VMEM((MLA_HEADS * QK_DIM, Q_RANK), BF16),
                        pltpu.VMEM((KV_RANK, MLA_HEADS * NOPE), BF16),
                        pltpu.VMEM((MLA_HEADS * V_DIM, KV_RANK), BF16)],
        compiler_params=_params(1),
        name="latent",
    )(x2, posr, inv_freq, w_in_t, gq, gkv, w_uq, w_ukv)

    tq = TQ_ATTN
    n_attn = batch * (seq // tq)
    conv_row0 = Q_RANK + KV_RANK + ROPE
    n_wc = 3 * CONV_WIDTH // (WCHUNK * n_attn)
    assert conv_row0 % WCHUNK == 0 and 3 * CONV_WIDTH == n_wc * WCHUNK * n_attn
    step = lambda b, i: b * (seq // tq) + i
    wchunk = lambda n: pl.BlockSpec((WCHUNK, D_MODEL),
                                    lambda b, i: (conv_row0 // WCHUNK + n_wc * step(b, i) + n, 0))
    attn, wconv = pl.pallas_call(
        functools.partial(_attn_kernel, tq=tq, tk=tk, heads=MLA_HEADS, lookahead=ATTN_LOOKAHEAD, n_wc=n_wc),
        grid=(batch, seq // tq),
        in_specs=[
            pl.BlockSpec((1, MLA_HEADS, QK_DIM, tq), lambda b, i: (b, 0, 0, i)),
            pl.BlockSpec((1, MLA_HEADS, seq, NOPE), lambda b, i: (b, 0, 0, 0)),
            pl.BlockSpec((1, seq, ROPE_LANES), lambda b, i: (b, 0, 0)),
            pl.BlockSpec((1, MLA_HEADS, seq // tk, V_ROWS, tk), lambda b, i: (b, 0, 0, 0, 0)),
            *[wchunk(n) for n in range(n_wc)],
        ],
        out_specs=[pl.BlockSpec((1, tq, MLA_WIDTH), lambda b, i: (b, i, 0)),
                   pl.BlockSpec((n_wc * WCHUNK, D_MODEL), lambda b, i: (step(b, i), 0))],
        out_shape=[jax.ShapeDtypeStruct((batch, seq, MLA_WIDTH), BF16),
                   jax.ShapeDtypeStruct((3 * CONV_WIDTH, D_MODEL), BF16)],
        scratch_shapes=[pltpu.VMEM((MLA_HEADS, 1, tq), F32), pltpu.VMEM((MLA_HEADS, V_ROWS, tq), F32),
                        pltpu.VMEM((MLA_HEADS, QK_PAD, tq), BF16)],
        compiler_params=_params(2),
        name="attn",
    )(qt, kn, kr, vt, *([w_in_t] * n_wc))
    attn = attn.reshape(T, MLA_WIDTH)

    tm2 = TM_CONVMIX
    n2 = T // tm2
    wo_rows = D_MODEL // n2
    conv_out, w_out_b = pl.pallas_call(
        functools.partial(_convmix_kernel, tc=TC_CONVMIX, tiles_per_seq=seq // tm2),
        grid=(n2,),
        in_specs=[
            pl.BlockSpec((tm2, D_MODEL), lambda i: (i, 0)),
            full((3 * CONV_WIDTH, D_MODEL)),
            full((3, CONV_WIDTH)),
            pl.BlockSpec((wo_rows, D_MODEL), lambda i: (i, 0)),
        ],
        out_specs=[pl.BlockSpec((tm2, CONV_WIDTH), lambda i: (i, 0)),
                   pl.BlockSpec((wo_rows, D_MODEL), lambda i: (i, 0))],
        out_shape=[jax.ShapeDtypeStruct((T, CONV_WIDTH), BF16),
                   jax.ShapeDtypeStruct((D_MODEL, D_MODEL), BF16)],
        scratch_shapes=[pltpu.VMEM((2 * HALO, CONV_WIDTH), F32)],
        compiler_params=_params(1),
        name="convmix",
    )(x2, wconv, conv_w, w_out)

    tm4 = TM_OUTPROJ
    x1, x1b = pl.pallas_call(
        functools.partial(_outproj_kernel, tm=tm4, rc=RC_OUTPROJ),
        grid=(T // tm4,),
        in_specs=[
            pl.BlockSpec((tm4, MLA_WIDTH), lambda i: (i, 0)),
            pl.BlockSpec((tm4, CONV_WIDTH), lambda i: (i, 0)),
            pl.BlockSpec((tm4, D_MODEL), lambda i: (i, 0)),
            full((D_MODEL, D_MODEL)), full((1, D_MODEL)), full((1, D_MODEL)),
        ],
        out_specs=[pl.BlockSpec((tm4, D_MODEL), lambda i: (i, 0)), pl.BlockSpec((tm4, D_MODEL), lambda i: (i, 0))],
        out_shape=[jax.ShapeDtypeStruct((T, D_MODEL), F32), jax.ShapeDtypeStruct((T, D_MODEL), BF16)],
        compiler_params=_params(1),
        name="outproj",
    )(attn, conv_out, x2, w_out_b, ln1_g.reshape(1, D_MODEL), ln1_b.reshape(1, D_MODEL))

    tm5, tn5 = TM_FFN_UP, TN_FFN_UP
    nn5 = FFN_DIM // tn5
    nm5 = T // tm5
    wd_rows = FFN_DIM // (nn5 * nm5)
    h2, w_down_b = pl.pallas_call(
        functools.partial(_ffn_up_kernel, tn=tn5, tiles_per_seq=seq // tm5),
        grid=(nn5, nm5),
        in_specs=[
            pl.BlockSpec((tm5, D_MODEL), lambda n, m: (m, 0)),
            pl.BlockSpec((D_MODEL, tn5), lambda n, m: (0, n)),
            pl.BlockSpec((D_MODEL, tn5), lambda n, m: (0, n + nn5)),
            pl.BlockSpec((3, tn5), lambda n, m: (0, n)),
            pl.BlockSpec((3, tn5), lambda n, m: (0, n + nn5)),
            pl.BlockSpec((wd_rows, D_MODEL), lambda n, m: (n * nm5 + m, 0)),
        ],
        out_specs=[pl.BlockSpec((tm5, tn5), lambda n, m: (m, n)),
                   pl.BlockSpec((wd_rows, D_MODEL), lambda n, m: (n * nm5 + m, 0))],
        out_shape=[jax.ShapeDtypeStruct((T, FFN_DIM), BF16),
                   jax.ShapeDtypeStruct((FFN_DIM, D_MODEL), BF16)],
        scratch_shapes=[pltpu.VMEM((D_MODEL, tn5), BF16), pltpu.VMEM((D_MODEL, tn5), BF16),
                        pltpu.VMEM((2 * HALO, tn5), F32), pltpu.VMEM((2 * HALO, tn5), F32)],
        compiler_params=_params(2),
        name="ffn_up",
    )(x1b, w_gate_up, w_gate_up, ffn_conv_w, ffn_conv_w, w_down)

    tm6 = TM_FFN_DOWN
    out = pl.pallas_call(
        functools.partial(_ffn_down_kernel, tm=tm6, rc=RC_FFN_DOWN),
        grid=(T // tm6,),
        in_specs=[
            pl.BlockSpec((tm6, FFN_DIM), lambda m: (m, 0)),
            full((FFN_DIM, D_MODEL)),
            pl.BlockSpec((tm6, D_MODEL), lambda m: (m, 0)),
            full((1, D_MODEL)), full((1, D_MODEL)),
        ],
        out_specs=pl.BlockSpec((tm6, D_MODEL), lambda m: (m, 0)),
        out_shape=jax.ShapeDtypeStruct((T, D_MODEL), F32),
        compiler_params=_params(1),
        name="ffn_down",
    )(h2, w_down_b, x1, ln2_g.reshape(1, D_MODEL), ln2_b.reshape(1, D_MODEL))
    return out


def kernel(x, positions, w_in, q_norm_g, w_uq, kv_norm_g, w_ukv, conv_w, w_out, ln1_g, ln1_b,
           w_gate_up, ffn_conv_w, w_down, ln2_g, ln2_b):
    batch, seq, _ = x.shape
    x2 = x.reshape(batch * seq, D_MODEL)
    for l in range(DEPTH):
        x2 = _layer(x2, positions, w_in[l], q_norm_g[l], w_uq[l], kv_norm_g[l], w_ukv[l], conv_w[l],
                    w_out[l], ln1_g[l], ln1_b[l], w_gate_up[l], ffn_conv_w[l], w_down[l], ln2_g[l], ln2_b[l],
                    batch, seq)
    return x2.reshape(batch, seq, D_MODEL)
```

```python
import functools
import math

import jax
import jax.numpy as jnp
from jax import lax
from jax.experimental import pallas as pl
from jax.experimental.pallas import tpu as pltpu

LANES = 128
F32_SUBLANES = 8
BF16_SUBLANES = 2 * F32_SUBLANES
V7X_VMEM_BYTES = 64 * 1024 * 1024
VMEM_LIMIT = V7X_VMEM_BYTES - 8 * 1024 * 1024

D_MODEL = 2048
MLA_HEADS = 8
NOPE = 128
ROPE = 64
V_DIM = 128
V_ROWS = V_DIM + BF16_SUBLANES
MLA_WIDTH = MLA_HEADS * V_DIM
Q_RANK = 512
KV_RANK = 256
CONV_WIDTH = 1024
FFN_DIM = 5632
ROPE_THETA = 10000.0
LN_EPS = 1e-5
RMS_EPS = 1e-6
DEPTH = 1
ALPHA = (2.0 * DEPTH) ** 0.25
QK_DIM = NOPE + ROPE
Q_SCALE = (QK_DIM ** -0.5) * math.log2(math.e)
ROPE_LANES = 2 * ROPE
QK_PAD = NOPE + ROPE_LANES
LAT_ROWS = Q_RANK + KV_RANK + LANES
HALO = F32_SUBLANES
WCHUNK = 64
NEG = -1e30

TM_LATENT = 512
TK_ATTN = 256
TQ_ATTN = 512
ATTN_LOOKAHEAD = 2
TM_CONVMIX, TC_CONVMIX = 1024, 256
TM_OUTPROJ, RC_OUTPROJ = 512, 128
TM_FFN_UP, TN_FFN_UP = 1024, 512
TM_FFN_DOWN, RC_FFN_DOWN = 512, 128

F32 = jnp.float32
BF16 = jnp.bfloat16


def _dot(a, b):
    return jnp.dot(a, b, preferred_element_type=F32)


def _dot_nt(a, b):
    return lax.dot_general(a, b, (((1,), (1,)), ((), ())), preferred_element_type=F32)


def _rms(x, g):
    return x * lax.rsqrt(jnp.mean(x * x, axis=-1, keepdims=True) + RMS_EPS) * g


def _layer_norm(y, g, b):
    mu = jnp.mean(y, axis=-1, keepdims=True)
    d = y - mu
    var = jnp.mean(d * d, axis=-1, keepdims=True)
    return d * lax.rsqrt(var + LN_EPS) * g + b


def _causal_conv3(z, hs_ref, cw, cols):
    n = z.shape[0]
    w0, w1, w2 = cw[0:1, :], cw[1:2, :], cw[2:3, :]
    y = w2 * z + w1 * pltpu.roll(z, 1, axis=0) + w0 * pltpu.roll(z, 2, axis=0)
    head = z[0:HALO]
    hs_ref[HALO:2 * HALO, cols] = head
    y_head = w2 * head + w1 * hs_ref[HALO - 1:2 * HALO - 1, cols] + w0 * hs_ref[HALO - 2:2 * HALO - 2, cols]
    hs_ref[0:HALO, cols] = z[n - HALO:n]
    return jnp.concatenate([y_head, y[HALO:]], axis=0)


def _latent_kernel(x_ref, posr_ref, invf_ref, win_ref, gq_ref, gkv_ref, wuq_ref, wukv_ref,
                   qt_ref, kn_ref, kr_ref, vt_ref, wlat_ref, wqt_ref, wk_ref, wvt_ref, *, tm, tk):
    half = ROPE // 2
    latent = Q_RANK + KV_RANK

    @pl.when(pl.program_id(0) == 0)
    def _():
        wlat_ref[...] = win_ref[...].astype(BF16)
        wqt_ref[...] = wuq_ref[...].T.astype(BF16)
        for hh in range(MLA_HEADS):
            k_cols = slice((NOPE + V_DIM) * hh, (NOPE + V_DIM) * hh + NOPE)
            v_cols = slice((NOPE + V_DIM) * hh + NOPE, (NOPE + V_DIM) * (hh + 1))
            wk_ref[:, NOPE * hh:NOPE * (hh + 1)] = wukv_ref[:, k_cols].astype(BF16)
            wvt_ref[V_DIM * hh:V_DIM * (hh + 1), :] = wukv_ref[:, v_cols].T.astype(BF16)

    xb = x_ref[...].astype(BF16)
    h = _dot_nt(xb, wlat_ref[...])
    cqn = _rms(h[:, :Q_RANK], gq_ref[...] * Q_SCALE).astype(BF16)
    ckvn = _rms(h[:, Q_RANK:latent], gkv_ref[...]).astype(BF16)

    ang_t = invf_ref[...] * posr_ref[...].astype(F32)
    cos_t, sin_t = jnp.cos(ang_t), jnp.sin(ang_t)
    table = jnp.concatenate([cos_t, cos_t, sin_t, sin_t], axis=0).T
    k_r = h[:, latent:latent + ROPE]
    kr_ref[0] = (jnp.concatenate([k_r, k_r], axis=1) * table).astype(BF16)

    qt = _dot_nt(wqt_ref[...], cqn)
    for hh in range(MLA_HEADS):
        base = QK_DIM * hh
        r1 = qt[base + NOPE:base + NOPE + half]
        r2 = qt[base + NOPE + half:base + QK_DIM]
        qr1 = (r1 * cos_t - r2 * sin_t).astype(BF16)
        qr2 = (r2 * cos_t + r1 * sin_t).astype(BF16)
        qt_ref[0, hh, 0, 0:NOPE, :] = qt[base:base + NOPE].astype(BF16)
        qt_ref[0, hh, 0, NOPE:NOPE + half, :] = qr1
        qt_ref[0, hh, 0, NOPE + half:QK_DIM, :] = qr2

    kn = _dot(ckvn, wk_ref[...])
    vt = _dot_nt(wvt_ref[...], ckvn)
    ones_row = (lax.broadcasted_iota(jnp.int32, (V_ROWS - V_DIM, tk), 0) == 0).astype(BF16)
    for hh in range(MLA_HEADS):
        kn_ref[0, hh] = kn[:, NOPE * hh:NOPE * hh + NOPE].astype(BF16)
        for c in range(tm // tk):
            vt_ref[0, hh, c, 0:V_DIM, :] = vt[V_DIM * hh:V_DIM * hh + V_DIM, tk * c:tk * c + tk].astype(BF16)
            vt_ref[0, hh, c, V_DIM:V_ROWS, :] = ones_row


def _convmix_kernel(x_ref, w_ref, cw_ref, wo_ref, o_ref, wob_ref, zs_ref, *, tc, tiles_per_seq):
    @pl.when(pl.program_id(0) % tiles_per_seq == 0)
    def _():
        zs_ref[0:HALO, :] = jnp.zeros((HALO, CONV_WIDTH), F32)

    wob_ref[...] = wo_ref[...].astype(BF16)

    xb = x_ref[...].astype(BF16)
    for c in range(CONV_WIDTH // tc):
        cols = slice(c * tc, (c + 1) * tc)
        gate_b = _dot_nt(xb, w_ref[c * tc:(c + 1) * tc, :])
        gate_c = _dot_nt(xb, w_ref[CONV_WIDTH + c * tc:CONV_WIDTH + (c + 1) * tc, :])
        x_c = _dot_nt(xb, w_ref[2 * CONV_WIDTH + c * tc:2 * CONV_WIDTH + (c + 1) * tc, :])
        y = _causal_conv3(gate_c * x_c, zs_ref, cw_ref[:, cols], cols)
        o_ref[:, cols] = (gate_b * y).astype(BF16)


def _attn_kernel(qt_ref, kn_ref, kr_ref, vt_ref, *refs, tq, tk, heads, lookahead, n_wc):
    wc_refs = refs[:n_wc]
    o_ref, wconv_ref, m_ref, acc_ref, qf_ref = refs[n_wc:]
    qi = pl.program_id(1)
    per_q = tq // tk
    trip_blocks = 2 * per_q
    half = ROPE // 2
    m_ref[...] = jnp.full(m_ref.shape, NEG, F32)
    acc_ref[...] = jnp.zeros(acc_ref.shape, F32)
    for g in range(heads):
        qf_ref[g, 0:QK_DIM, :] = qt_ref[0, g, 0]
        qf_ref[g, QK_DIM:QK_DIM + half, :] = qt_ref[0, g, 0, NOPE + half:QK_DIM, :]
        qf_ref[g, QK_DIM + half:QK_PAD, :] = -qt_ref[0, g, 0, NOPE:NOPE + half, :]

    for n, wc_ref in enumerate(wc_refs):
        wconv_ref[n * WCHUNK:(n + 1) * WCHUNK, :] = wc_ref[...].astype(BF16)

    def run(chains):
        def scores(chain):
            g, j, c0, _ = chain
            start = pl.multiple_of(j * tk, tk)
            k = jnp.concatenate([kn_ref[0, g, pl.ds(start, tk), :], kr_ref[0, pl.ds(start, tk), :]], axis=1)
            return _dot(k, qf_ref[g, :, c0:])

        ahead = [scores(ch) for ch in chains[:lookahead]]
        for i, (g, j, c0, mask) in enumerate(chains):
            s = ahead.pop(0)
            if i + lookahead < len(chains):
                ahead.append(scores(chains[i + lookahead]))
            if mask is not None:
                s = jnp.where(mask, s, NEG)
            m = m_ref[g, :, c0:]
            m_new = jnp.maximum(m, jnp.max(s, axis=0, keepdims=True))
            p = jnp.exp2(s - m_new)
            a = jnp.exp2(m - m_new)
            m_ref[g, :, c0:] = m_new
            acc_ref[g, :, c0:] = a * acc_ref[g, :, c0:] + _dot(vt_ref[0, g, j], p.astype(BF16))

    def full_blocks(first, count):
        return [(g, first + b, 0, None) for b in range(count) for g in range(heads)]

    def below_diagonal(t, c):
        run(full_blocks(trip_blocks * t, trip_blocks))
        return c

    n_below = qi * per_q
    lax.fori_loop(0, n_below // trip_blocks, below_diagonal, 0)

    @pl.when(n_below % trip_blocks != 0)
    def _():
        run(full_blocks(n_below - per_q, per_q))

    diag = []
    for b in range(per_q):
        c0 = b * tk
        key = lax.broadcasted_iota(jnp.int32, (tk, tq - c0), 0)
        qry = lax.broadcasted_iota(jnp.int32, (tk, tq - c0), 1)
        diag += [(g, per_q * qi + b, c0, key <= qry) for g in range(heads)]
    run(diag)
    for g in range(heads):
        out_t = acc_ref[g, 0:V_DIM, :] / acc_ref[g, V_DIM:V_DIM + 1, :]
        o_ref[0, :, V_DIM * g:V_DIM * g + V_DIM] = out_t.T.astype(BF16)


def _outproj_kernel(a_ref, c_ref, x_ref, w_ref, g_ref, b_ref, o32_ref, o16_ref, *, tm, rc):
    for r in range(tm // rc):
        rows = slice(r * rc, (r + 1) * rc)
        mixed = jnp.concatenate([a_ref[rows, :], c_ref[rows, :]], axis=1)
        y = _dot(mixed, w_ref[...]) + ALPHA * x_ref[rows, :]
        out = _layer_norm(y, g_ref[...], b_ref[...])
        o32_ref[rows, :] = out
        o16_ref[rows, :] = out.astype(BF16)


def _ffn_up_kernel(x_ref, wg_ref, wu_ref, cwg_ref, cwu_ref, wd_ref, o_ref, wdb_ref,
                   wgb_ref, wub_ref, gs_ref, us_ref, *, tn, tiles_per_seq):
    mi = pl.program_id(1)

    @pl.when(mi == 0)
    def _():
        wgb_ref[...] = wg_ref[...].astype(BF16)
        wub_ref[...] = wu_ref[...].astype(BF16)

    @pl.when(mi % tiles_per_seq == 0)
    def _():
        gs_ref[0:HALO, :] = jnp.zeros((HALO, tn), F32)
        us_ref[0:HALO, :] = jnp.zeros((HALO, tn), F32)

    wdb_ref[...] = wd_ref[...].astype(BF16)

    xb = x_ref[...]
    cols = slice(0, tn)
    g = _causal_conv3(_dot(xb, wgb_ref[...]), gs_ref, cwg_ref[...], cols)
    u = _causal_conv3(_dot(xb, wub_ref[...]), us_ref, cwu_ref[...], cols)
    o_ref[...] = (g / (1.0 + jnp.exp(-g)) * u).astype(BF16)


def _ffn_down_kernel(h_ref, w_ref, x_ref, g_ref, b_ref, o_ref, *, tm, rc):
    for r in range(tm // rc):
        rows = slice(r * rc, (r + 1) * rc)
        y = _dot(h_ref[rows, :], w_ref[...]) + ALPHA * x_ref[rows, :]
        o_ref[rows, :] = _layer_norm(y, g_ref[...], b_ref[...])


def _params(n_grid_axes):
    return pltpu.CompilerParams(dimension_semantics=("arbitrary",) * n_grid_axes, vmem_limit_bytes=VMEM_LIMIT)


def _layer(x2, pos, w_in, q_norm_g, w_uq, kv_norm_g, w_ukv, conv_w, w_out, ln1_g, ln1_b,
           w_gate_up, ffn_conv_w, w_down, ln2_g, ln2_b, batch, seq):
    T = batch * seq
    for tile in (TM_LATENT, TQ_ATTN, TM_CONVMIX, TM_OUTPROJ, TM_FFN_UP, TM_FFN_DOWN):
        assert seq % tile == 0, (seq, tile)
    assert TM_LATENT % TK_ATTN == 0 and TQ_ATTN % TK_ATTN == 0
    assert TQ_ATTN == TM_LATENT
    assert x2.shape == (T, D_MODEL) and w_in.shape == (D_MODEL, LAT_ROWS - LANES + ROPE + 3 * CONV_WIDTH)

    half = ROPE // 2
    gq = q_norm_g.reshape(1, Q_RANK)
    gkv = kv_norm_g.reshape(1, KV_RANK)
    inv_freq = (ROPE_THETA ** (-jnp.arange(0, ROPE, 2, dtype=F32) / ROPE)).reshape(half, 1)
    posr = pos.reshape(1, T)

    tm, tk = TM_LATENT, TK_ATTN
    nblk = seq // tm
    full = lambda shape: pl.BlockSpec(shape, lambda *_: (0,) * len(shape), pipeline_mode=pl.Buffered(1))
    n1 = T // tm
    w_in_t = w_in.T
    qt, kn, kr, vt = pl.pallas_call(
        functools.partial(_latent_kernel, tm=tm, tk=tk),
        grid=(n1,),
        in_specs=[
            pl.BlockSpec((tm, D_MODEL), lambda i: (i, 0)),
            pl.BlockSpec((1, tm), lambda i: (0, i)),
            full((half, 1)),
            full((LAT_ROWS, D_MODEL)),
            full((1, Q_RANK)), full((1, KV_RANK)),
            full((Q_RANK, MLA_HEADS * QK_DIM)),
            full((KV_RANK, MLA_HEADS * (NOPE + V_DIM))),
        ],
        out_specs=[
            pl.BlockSpec((1, MLA_HEADS, 1, QK_DIM, tm), lambda i: (i // nblk, 0, i % nblk, 0, 0)),
            pl.BlockSpec((1, MLA_HEADS, tm, NOPE), lambda i: (i // nblk, 0, i % nblk, 0)),
            pl.BlockSpec((1, tm, ROPE_LANES), lambda i: (i // nblk, i % nblk, 0)),
            pl.BlockSpec((1, MLA_HEADS, tm // tk, V_ROWS, tk), lambda i: (i // nblk, 0, i % nblk, 0, 0)),
        ],
        out_shape=[
            jax.ShapeDtypeStruct((batch, MLA_HEADS, seq // tm, QK_DIM, tm), BF16),
            jax.ShapeDtypeStruct((batch, MLA_HEADS, seq, NOPE), BF16),
            jax.ShapeDtypeStruct((batch, seq, ROPE_LANES), BF16),
            jax.ShapeDtypeStruct((batch, MLA_HEADS, seq // tk, V_ROWS, tk), BF16),
        ],
        scratch_shapes=[pltpu.VMEM((LAT_ROWS, D_MODEL), BF16),
                        pltpu.VMEM((MLA_HEADS * QK_DIM, Q_RANK), BF16),
                        pltpu.VMEM((KV_RANK, MLA_HEADS * NOPE), BF16),
                        pltpu.VMEM((MLA_HEADS * V_DIM, KV_RANK), BF16)],
        compiler_params=_params(1),
        name="latent",
    )(x2, posr, inv_freq, w_in_t, gq, gkv, w_uq, w_ukv)

    tq = TQ_ATTN
    n_attn = batch * (seq // tq)
    conv_row0 = Q_RANK + KV_RANK + ROPE
    n_wc = 3 * CONV_WIDTH // (WCHUNK * n_attn)
    assert conv_row0 % WCHUNK == 0 and 3 * CONV_WIDTH == n_wc * WCHUNK * n_attn
    step = lambda b, i: b * (seq // tq) + i
    wchunk = lambda n: pl.BlockSpec((WCHUNK, D_MODEL),
                                    lambda b, i: (conv_row0 // WCHUNK + n_wc * step(b, i) + n, 0))
    attn, wconv = pl.pallas_call(
        functools.partial(_attn_kernel, tq=tq, tk=tk, heads=MLA_HEADS, lookahead=ATTN_LOOKAHEAD, n_wc=n_wc),
        grid=(batch, seq // tq),
        in_specs=[
            pl.BlockSpec((1, MLA_HEADS, 1, QK_DIM, tq), lambda b, i: (b, 0, i, 0, 0)),
            pl.BlockSpec((1, MLA_HEADS, seq, NOPE), lambda b, i: (b, 0, 0, 0)),
            pl.BlockSpec((1, seq, ROPE_LANES), lambda b, i: (b, 0, 0)),
            pl.BlockSpec((1, MLA_HEADS, seq // tk, V_ROWS, tk), lambda b, i: (b, 0, 0, 0, 0)),
            *[wchunk(n) for n in range(n_wc)],
        ],
        out_specs=[pl.BlockSpec((1, tq, MLA_WIDTH), lambda b, i: (b, i, 0)),
                   pl.BlockSpec((n_wc * WCHUNK, D_MODEL), lambda b, i: (step(b, i), 0))],
        out_shape=[jax.ShapeDtypeStruct((batch, seq, MLA_WIDTH), BF16),
                   jax.ShapeDtypeStruct((3 * CONV_WIDTH, D_MODEL), BF16)],
        scratch_shapes=[pltpu.VMEM((MLA_HEADS, 1, tq), F32), pltpu.VMEM((MLA_HEADS, V_ROWS, tq), F32),
                        pltpu.VMEM((MLA_HEADS, QK_PAD, tq), BF16)],
        compiler_params=_params(2),
        name="attn",
    )(qt, kn, kr, vt, *([w_in_t] * n_wc))
    attn = attn.reshape(T, MLA_WIDTH)

    tm2 = TM_CONVMIX
    n2 = T // tm2
    wo_rows = D_MODEL // n2
    conv_out, w_out_b = pl.pallas_call(
        functools.partial(_convmix_kernel, tc=TC_CONVMIX, tiles_per_seq=seq // tm2),
        grid=(n2,),
        in_specs=[
            pl.BlockSpec((tm2, D_MODEL), lambda i: (i, 0)),
            full((3 * CONV_WIDTH, D_MODEL)),
            full((3, CONV_WIDTH)),
            pl.BlockSpec((wo_rows, D_MODEL), lambda i: (i, 0)),
        ],
        out_specs=[pl.BlockSpec((tm2, CONV_WIDTH), lambda i: (i, 0)),
                   pl.BlockSpec((wo_rows, D_MODEL), lambda i: (i, 0))],
        out_shape=[jax.ShapeDtypeStruct((T, CONV_WIDTH), BF16),
                   jax.ShapeDtypeStruct((D_MODEL, D_MODEL), BF16)],
        scratch_shapes=[pltpu.VMEM((2 * HALO, CONV_WIDTH), F32)],
        compiler_params=_params(1),
        name="convmix",
    )(x2, wconv, conv_w, w_out)

    tm4 = TM_OUTPROJ
    x1, x1b = pl.pallas_call(
        functools.partial(_outproj_kernel, tm=tm4, rc=RC_OUTPROJ),
        grid=(T // tm4,),
        in_specs=[
            pl.BlockSpec((tm4, MLA_WIDTH), lambda i: (i, 0)),
            pl.BlockSpec((tm4, CONV_WIDTH), lambda i: (i, 0)),
            pl.BlockSpec((tm4, D_MODEL), lambda i: (i, 0)),
            full((D_MODEL, D_MODEL)), full((1, D_MODEL)), full((1, D_MODEL)),
        ],
        out_specs=[pl.BlockSpec((tm4, D_MODEL), lambda i: (i, 0)), pl.BlockSpec((tm4, D_MODEL), lambda i: (i, 0))],
        out_shape=[jax.ShapeDtypeStruct((T, D_MODEL), F32), jax.ShapeDtypeStruct((T, D_MODEL), BF16)],
        compiler_params=_params(1),
        name="outproj",
    )(attn, conv_out, x2, w_out_b, ln1_g.reshape(1, D_MODEL), ln1_b.reshape(1, D_MODEL))

    tm5, tn5 = TM_FFN_UP, TN_FFN_UP
    nn5 = FFN_DIM // tn5
    nm5 = T // tm5
    wd_rows = FFN_DIM // (nn5 * nm5)
    h2, w_down_b = pl.pallas_call(
        functools.partial(_ffn_up_kernel, tn=tn5, tiles_per_seq=seq // tm5),
        grid=(nn5, nm5),
        in_specs=[
            pl.BlockSpec((tm5, D_MODEL), lambda n, m: (m, 0)),
            pl.BlockSpec((D_MODEL, tn5), lambda n, m: (0, n)),
            pl.BlockSpec((D_MODEL, tn5), lambda n, m: (0, n + nn5)),
            pl.BlockSpec((3, tn5), lambda n, m: (0, n)),
            pl.BlockSpec((3, tn5), lambda n, m: (0, n + nn5)),
            pl.BlockSpec((wd_rows, D_MODEL), lambda n, m: (n * nm5 + m, 0)),
        ],
        out_specs=[pl.BlockSpec((tm5, tn5), lambda n, m: (m, n)),
                   pl.BlockSpec((wd_rows, D_MODEL), lambda n, m: (n * nm5 + m, 0))],
        out_shape=[jax.ShapeDtypeStruct((T, FFN_DIM), BF16),
                   jax.ShapeDtypeStruct((FFN_DIM, D_MODEL), BF16)],
        scratch_shapes=[pltpu.VMEM((D_MODEL, tn5), BF16), pltpu.VMEM((D_MODEL, tn5), BF16),
                        pltpu.VMEM((2 * HALO, tn5), F32), pltpu.VMEM((2 * HALO, tn5), F32)],
        compiler_params=_params(2),
        name="ffn_up",
    )(x1b, w_gate_up, w_gate_up, ffn_conv_w, ffn_conv_w, w_down)

    tm6 = TM_FFN_DOWN
    out = pl.pallas_call(
        functools.partial(_ffn_down_kernel, tm=tm6, rc=RC_FFN_DOWN),
        grid=(T // tm6,),
        in_specs=[
            pl.BlockSpec((tm6, FFN_DIM), lambda m: (m, 0)),
            full((FFN_DIM, D_MODEL)),
            pl.BlockSpec((tm6, D_MODEL), lambda m: (m, 0)),
            full((1, D_MODEL)), full((1, D_MODEL)),
        ],
        out_specs=pl.BlockSpec((tm6, D_MODEL), lambda m: (m, 0)),
        out_shape=jax.ShapeDtypeStruct((T, D_MODEL), F32),
        compiler_params=_params(1),
        name="ffn_down",
    )(h2, w_down_b, x1, ln2_g.reshape(1, D_MODEL), ln2_b.reshape(1, D_MODEL))
    return out


def kernel(x, positions, w_in, q_norm_g, w_uq, kv_norm_g, w_ukv, conv_w, w_out, ln1_g, ln1_b,
           w_gate_up, ffn_conv_w, w_down, ln2_g, ln2_b):
    batch, seq, _ = x.shape
    x2 = x.reshape(batch * seq, D_MODEL)
    for l in range(DEPTH):
        x2 = _layer(x2, positions, w_in[l], q_norm_g[l], w_uq[l], kv_norm_g[l], w_ukv[l], conv_w[l],
                    w_out[l], ln1_g[l], ln1_b[l], w_gate_up[l], ffn_conv_w[l], w_down[l], ln2_g[l], ln2_b[l],
                    batch, seq)
    return x2.reshape(batch, seq, D_MODEL)
```

```python
import functools
import math

import jax
import jax.numpy as jnp
from jax import lax
from jax.experimental import pallas as pl
from jax.experimental.pallas import tpu as pltpu

LANES = 128
F32_SUBLANES = 8
BF16_SUBLANES = 2 * F32_SUBLANES
V7X_VMEM_BYTES = 64 * 1024 * 1024
VMEM_LIMIT = V7X_VMEM_BYTES - 8 * 1024 * 1024

D_MODEL = 2048
MLA_HEADS = 8
NOPE = 128
ROPE = 64
V_DIM = 128
V_ROWS = V_DIM + BF16_SUBLANES
MLA_WIDTH = MLA_HEADS * V_DIM
Q_RANK = 512
KV_RANK = 256
CONV_WIDTH = 1024
FFN_DIM = 5632
ROPE_THETA = 10000.0
LN_EPS = 1e-5
RMS_EPS = 1e-6
DEPTH = 1
ALPHA = (2.0 * DEPTH) ** 0.25
QK_DIM = NOPE + ROPE
Q_SCALE = (QK_DIM ** -0.5) * math.log2(math.e)
ROPE_LANES = 2 * ROPE
QK_PAD = NOPE + ROPE_LANES
LAT_ROWS = Q_RANK + KV_RANK + LANES
HALO = F32_SUBLANES
WCHUNK = 64
NEG = -1e30

TM_LATENT = 512
TK_ATTN = 256
TQ_ATTN = 512
ATTN_LOOKAHEAD = 2
TM_CONVMIX, TC_CONVMIX = 1024, 256
TM_OUTPROJ, RC_OUTPROJ = 512, 128
TM_FFN_UP, TN_FFN_UP = 1024, 512
TM_FFN_DOWN, RC_FFN_DOWN = 512, 128

F32 = jnp.float32
BF16 = jnp.bfloat16


def _dot(a, b):
    return jnp.dot(a, b, preferred_element_type=F32)


def _dot_nt(a, b):
    return lax.dot_general(a, b, (((1,), (1,)), ((), ())), preferred_element_type=F32)


def _rms(x, g):
    return x * lax.rsqrt(jnp.mean(x * x, axis=-1, keepdims=True) + RMS_EPS) * g


def _layer_norm(y, g, b):
    mu = jnp.mean(y, axis=-1, keepdims=True)
    d = y - mu
    var = jnp.mean(d * d, axis=-1, keepdims=True)
    return d * lax.rsqrt(var + LN_EPS) * g + b


def _causal_conv3(z, hs_ref, cw, cols):
    n = z.shape[0]
    w0, w1, w2 = cw[0:1, :], cw[1:2, :], cw[2:3, :]
    y = w2 * z + w1 * pltpu.roll(z, 1, axis=0) + w0 * pltpu.roll(z, 2, axis=0)
    head = z[0:HALO]
    hs_ref[HALO:2 * HALO, cols] = head
    y_head = w2 * head + w1 * hs_ref[HALO - 1:2 * HALO - 1, cols] + w0 * hs_ref[HALO - 2:2 * HALO - 2, cols]
    hs_ref[0:HALO, cols] = z[n - HALO:n]
    return jnp.concatenate([y_head, y[HALO:]], axis=0)


def _latent_kernel(x_ref, posr_ref, invf_ref, win_ref, gq_ref, gkv_ref, wuq_ref, wukv_ref,
                   qt_ref, kn_ref, kr_ref, vt_ref, wlat_ref, wqt_ref, wk_ref, wvt_ref, *, tm, tk):
    half = ROPE // 2
    latent = Q_RANK + KV_RANK

    @pl.when(pl.program_id(0) == 0)
    def _():
        wlat_ref[...] = win_ref[...].astype(BF16)
        wqt_ref[...] = wuq_ref[...].T.astype(BF16)
        for hh in range(MLA_HEADS):
            k_cols = slice((NOPE + V_DIM) * hh, (NOPE + V_DIM) * hh + NOPE)
            v_cols = slice((NOPE + V_DIM) * hh + NOPE, (NOPE + V_DIM) * (hh + 1))
            wk_ref[:, NOPE * hh:NOPE * (hh + 1)] = wukv_ref[:, k_cols].astype(BF16)
            wvt_ref[V_DIM * hh:V_DIM * (hh + 1), :] = wukv_ref[:, v_cols].T.astype(BF16)

    xb = x_ref[...].astype(BF16)
    h = _dot_nt(xb, wlat_ref[...])
    cqn = _rms(h[:, :Q_RANK], gq_ref[...] * Q_SCALE).astype(BF16)
    ckvn = _rms(h[:, Q_RANK:latent], gkv_ref[...]).astype(BF16)

    ang_t = invf_ref[...] * posr_ref[...].astype(F32)
    cos_t, sin_t = jnp.cos(ang_t), jnp.sin(ang_t)
    table = jnp.concatenate([cos_t, cos_t, sin_t, sin_t], axis=0).T
    k_r = h[:, latent:latent + ROPE]
    kr_ref[0] = (jnp.concatenate([k_r, k_r], axis=1) * table).astype(BF16)

    qt = _dot_nt(wqt_ref[...], cqn)
    for hh in range(MLA_HEADS):
        base = QK_DIM * hh
        r1 = qt[base + NOPE:base + NOPE + half]
        r2 = qt[base + NOPE + half:base + QK_DIM]
        qr1 = (r1 * cos_t - r2 * sin_t).astype(BF16)
        qr2 = (r2 * cos_t + r1 * sin_t).astype(BF16)
        qt_ref[0, hh, 0:NOPE, :] = qt[base:base + NOPE].astype(BF16)
        qt_ref[0, hh, NOPE:NOPE + half, :] = qr1
        qt_ref[0, hh, NOPE + half:QK_DIM, :] = qr2

    kn = _dot(ckvn, wk_ref[...])
    vt = _dot_nt(wvt_ref[...], ckvn)
    ones_row = (lax.broadcasted_iota(jnp.int32, (V_ROWS - V_DIM, tk), 0) == 0).astype(BF16)
    for hh in range(MLA_HEADS):
        kn_ref[0, hh] = kn[:, NOPE * hh:NOPE * hh + NOPE].astype(BF16)
        for c in range(tm // tk):
            vt_ref[0, hh, c, 0:V_DIM, :] = vt[V_DIM * hh:V_DIM * hh + V_DIM, tk * c:tk * c + tk].astype(BF16)
            vt_ref[0, hh, c, V_DIM:V_ROWS, :] = ones_row


def _convmix_kernel(x_ref, w_ref, cw_ref, wo_ref, o_ref, wob_ref, zs_ref, *, tc, tiles_per_seq):
    @pl.when(pl.program_id(0) % tiles_per_seq == 0)
    def _():
        zs_ref[0:HALO, :] = jnp.zeros((HALO, CONV_WIDTH), F32)

    wob_ref[...] = wo_ref[...].astype(BF16)

    xb = x_ref[...].astype(BF16)
    for c in range(CONV_WIDTH // tc):
        cols = slice(c * tc, (c + 1) * tc)
        gate_b = _dot_nt(xb, w_ref[c * tc:(c + 1) * tc, :])
        gate_c = _dot_nt(xb, w_ref[CONV_WIDTH + c * tc:CONV_WIDTH + (c + 1) * tc, :])
        x_c = _dot_nt(xb, w_ref[2 * CONV_WIDTH + c * tc:2 * CONV_WIDTH + (c + 1) * tc, :])
        y = _causal_conv3(gate_c * x_c, zs_ref, cw_ref[:, cols], cols)
        o_ref[:, cols] = (gate_b * y).astype(BF16)


def _attn_kernel(qt_ref, kn_ref, kr_ref, vt_ref, *refs, tq, tk, heads, lookahead, n_wc):
    wc_refs = refs[:n_wc]
    wo_ref = refs[n_wc]
    o_ref, wconv_ref, wob_ref, m_ref, acc_ref, qf_ref = refs[n_wc + 1:]
    wob_ref[...] = wo_ref[...].astype(BF16)
    qi = pl.program_id(1)
    per_q = tq // tk
    trip_blocks = 2 * per_q
    half = ROPE // 2
    m_ref[...] = jnp.full(m_ref.shape, NEG, F32)
    acc_ref[...] = jnp.zeros(acc_ref.shape, F32)
    for g in range(heads):
        qf_ref[g, 0:QK_DIM, :] = qt_ref[0, g]
        qf_ref[g, QK_DIM:QK_DIM + half, :] = qt_ref[0, g, NOPE + half:QK_DIM, :]
        qf_ref[g, QK_DIM + half:QK_PAD, :] = -qt_ref[0, g, NOPE:NOPE + half, :]

    for n, wc_ref in enumerate(wc_refs):
        wconv_ref[n * WCHUNK:(n + 1) * WCHUNK, :] = wc_ref[...].astype(BF16)

    def run(chains):
        def scores(chain):
            g, j, c0, _ = chain
            start = pl.multiple_of(j * tk, tk)
            k = jnp.concatenate([kn_ref[0, g, pl.ds(start, tk), :], kr_ref[0, pl.ds(start, tk), :]], axis=1)
            return _dot(k, qf_ref[g, :, c0:])

        ahead = [scores(ch) for ch in chains[:lookahead]]
        for i, (g, j, c0, mask) in enumerate(chains):
            s = ahead.pop(0)
            if i + lookahead < len(chains):
                ahead.append(scores(chains[i + lookahead]))
            if mask is not None:
                s = jnp.where(mask, s, NEG)
            m = m_ref[g, :, c0:]
            m_new = jnp.maximum(m, jnp.max(s, axis=0, keepdims=True))
            p = jnp.exp2(s - m_new)
            a = jnp.exp2(m - m_new)
            m_ref[g, :, c0:] = m_new
            acc_ref[g, :, c0:] = a * acc_ref[g, :, c0:] + _dot(vt_ref[0, g, j], p.astype(BF16))

    def full_blocks(first, count):
        return [(g, first + b, 0, None) for b in range(count) for g in range(heads)]

    def below_diagonal(t, c):
        run(full_blocks(trip_blocks * t, trip_blocks))
        return c

    n_below = qi * per_q
    lax.fori_loop(0, n_below // trip_blocks, below_diagonal, 0)

    @pl.when(n_below % trip_blocks != 0)
    def _():
        run(full_blocks(n_below - per_q, per_q))

    diag = []
    for b in range(per_q):
        c0 = b * tk
        key = lax.broadcasted_iota(jnp.int32, (tk, tq - c0), 0)
        qry = lax.broadcasted_iota(jnp.int32, (tk, tq - c0), 1)
        diag += [(g, per_q * qi + b, c0, key <= qry) for g in range(heads)]
    run(diag)
    for g in range(heads):
        out_t = acc_ref[g, 0:V_DIM, :] / acc_ref[g, V_DIM:V_DIM + 1, :]
        o_ref[0, :, V_DIM * g:V_DIM * g + V_DIM] = out_t.T.astype(BF16)


def _outproj_kernel(a_ref, c_ref, x_ref, w_ref, g_ref, b_ref, o32_ref, o16_ref, *, tm, rc):
    for r in range(tm // rc):
        rows = slice(r * rc, (r + 1) * rc)
        mixed = jnp.concatenate([a_ref[rows, :], c_ref[rows, :]], axis=1)
        y = _dot(mixed, w_ref[...]) + ALPHA * x_ref[rows, :]
        out = _layer_norm(y, g_ref[...], b_ref[...])
        o32_ref[rows, :] = out
        o16_ref[rows, :] = out.astype(BF16)


def _mix_outproj_kernel(x_ref, a_ref, w_ref, cw_ref, wo_ref, g_ref, b_ref, o32_ref, o16_ref, zs_ref, cs_ref, *,
                        tm, tc, rc, tiles_per_seq):
    @pl.when(pl.program_id(0) % tiles_per_seq == 0)
    def _():
        zs_ref[0:HALO, :] = jnp.zeros((HALO, CONV_WIDTH), F32)

    xb = x_ref[...].astype(BF16)
    for c in range(CONV_WIDTH // tc):
        cols = slice(c * tc, (c + 1) * tc)
        gate_b = _dot_nt(xb, w_ref[c * tc:(c + 1) * tc, :])
        gate_c = _dot_nt(xb, w_ref[CONV_WIDTH + c * tc:CONV_WIDTH + (c + 1) * tc, :])
        x_c = _dot_nt(xb, w_ref[2 * CONV_WIDTH + c * tc:2 * CONV_WIDTH + (c + 1) * tc, :])
        y = _causal_conv3(gate_c * x_c, zs_ref, cw_ref[:, cols], cols)
        cs_ref[:, cols] = (gate_b * y).astype(BF16)
    for r in range(tm // rc):
        rows = slice(r * rc, (r + 1) * rc)
        mixed = jnp.concatenate([a_ref[rows, :], cs_ref[rows, :]], axis=1)
        y = _dot(mixed, wo_ref[...]) + ALPHA * x_ref[rows, :]
        out = _layer_norm(y, g_ref[...], b_ref[...])
        o32_ref[rows, :] = out
        o16_ref[rows, :] = out.astype(BF16)


def _ffn_up_kernel(x_ref, wg_ref, wu_ref, cwg_ref, cwu_ref, wd_ref, o_ref, wdb_ref,
                   wgb_ref, wub_ref, gs_ref, us_ref, *, tn, tiles_per_seq):
    mi = pl.program_id(1)

    @pl.when(mi == 0)
    def _():
        wgb_ref[...] = wg_ref[...].astype(BF16)
        wub_ref[...] = wu_ref[...].astype(BF16)

    @pl.when(mi % tiles_per_seq == 0)
    def _():
        gs_ref[0:HALO, :] = jnp.zeros((HALO, tn), F32)
        us_ref[0:HALO, :] = jnp.zeros((HALO, tn), F32)

    wdb_ref[...] = wd_ref[...].astype(BF16)

    xb = x_ref[...]
    cols = slice(0, tn)
    g = _causal_conv3(_dot(xb, wgb_ref[...]), gs_ref, cwg_ref[...], cols)
    u = _causal_conv3(_dot(xb, wub_ref[...]), us_ref, cwu_ref[...], cols)
    o_ref[...] = (g / (1.0 + jnp.exp(-g)) * u).astype(BF16)


def _ffn_down_kernel(h_ref, w_ref, x_ref, g_ref, b_ref, o_ref, *, tm, rc):
    for r in range(tm // rc):
        rows = slice(r * rc, (r + 1) * rc)
        y = _dot(h_ref[rows, :], w_ref[...]) + ALPHA * x_ref[rows, :]
        o_ref[rows, :] = _layer_norm(y, g_ref[...], b_ref[...])


def _params(n_grid_axes):
    return pltpu.CompilerParams(dimension_semantics=("arbitrary",) * n_grid_axes, vmem_limit_bytes=VMEM_LIMIT)


def _layer(x2, pos, w_in, q_norm_g, w_uq, kv_norm_g, w_ukv, conv_w, w_out, ln1_g, ln1_b,
           w_gate_up, ffn_conv_w, w_down, ln2_g, ln2_b, batch, seq):
    T = batch * seq
    for tile in (TM_LATENT, TQ_ATTN, TM_CONVMIX, TM_OUTPROJ, TM_FFN_UP, TM_FFN_DOWN):
        assert seq % tile == 0, (seq, tile)
    assert TM_LATENT % TK_ATTN == 0 and TQ_ATTN % TK_ATTN == 0
    assert x2.shape == (T, D_MODEL) and w_in.shape == (D_MODEL, LAT_ROWS - LANES + ROPE + 3 * CONV_WIDTH)

    half = ROPE // 2
    gq = q_norm_g.reshape(1, Q_RANK)
    gkv = kv_norm_g.reshape(1, KV_RANK)
    inv_freq = (ROPE_THETA ** (-jnp.arange(0, ROPE, 2, dtype=F32) / ROPE)).reshape(half, 1)
    posr = pos.reshape(1, T)

    tm, tk = TM_LATENT, TK_ATTN
    nblk = seq // tm
    full = lambda shape: pl.BlockSpec(shape, lambda *_: (0,) * len(shape), pipeline_mode=pl.Buffered(1))
    n1 = T // tm
    w_in_t = w_in.T
    qt, kn, kr, vt = pl.pallas_call(
        functools.partial(_latent_kernel, tm=tm, tk=tk),
        grid=(n1,),
        in_specs=[
            pl.BlockSpec((tm, D_MODEL), lambda i: (i, 0)),
            pl.BlockSpec((1, tm), lambda i: (0, i)),
            full((half, 1)),
            full((LAT_ROWS, D_MODEL)),
            full((1, Q_RANK)), full((1, KV_RANK)),
            full((Q_RANK, MLA_HEADS * QK_DIM)),
            full((KV_RANK, MLA_HEADS * (NOPE + V_DIM))),
        ],
        out_specs=[
            pl.BlockSpec((1, MLA_HEADS, QK_DIM, tm), lambda i: (i // nblk, 0, 0, i % nblk)),
            pl.BlockSpec((1, MLA_HEADS, tm, NOPE), lambda i: (i // nblk, 0, i % nblk, 0)),
            pl.BlockSpec((1, tm, ROPE_LANES), lambda i: (i // nblk, i % nblk, 0)),
            pl.BlockSpec((1, MLA_HEADS, tm // tk, V_ROWS, tk), lambda i: (i // nblk, 0, i % nblk, 0, 0)),
        ],
        out_shape=[
            jax.ShapeDtypeStruct((batch, MLA_HEADS, QK_DIM, seq), BF16),
            jax.ShapeDtypeStruct((batch, MLA_HEADS, seq, NOPE), BF16),
            jax.ShapeDtypeStruct((batch, seq, ROPE_LANES), BF16),
            jax.ShapeDtypeStruct((batch, MLA_HEADS, seq // tk, V_ROWS, tk), BF16),
        ],
        scratch_shapes=[pltpu.VMEM((LAT_ROWS, D_MODEL), BF16),
                        pltpu.VMEM((MLA_HEADS * QK_DIM, Q_RANK), BF16),
                        pltpu.VMEM((KV_RANK, MLA_HEADS * NOPE), BF16),
                        pltpu.VMEM((MLA_HEADS * V_DIM, KV_RANK), BF16)],
        compiler_params=_params(1),
        name="latent",
    )(x2, posr, inv_freq, w_in_t, gq, gkv, w_uq, w_ukv)

    tq = TQ_ATTN
    n_attn = batch * (seq // tq)
    conv_row0 = Q_RANK + KV_RANK + ROPE
    n_wc = 3 * CONV_WIDTH // (WCHUNK * n_attn)
    assert conv_row0 % WCHUNK == 0 and 3 * CONV_WIDTH == n_wc * WCHUNK * n_attn
    step = lambda b, i: b * (seq // tq) + i
    wchunk = lambda n: pl.BlockSpec((WCHUNK, D_MODEL),
                                    lambda b, i: (conv_row0 // WCHUNK + n_wc * step(b, i) + n, 0))
    attn, wconv, w_out_b = pl.pallas_call(
        functools.partial(_attn_kernel, tq=tq, tk=tk, heads=MLA_HEADS, lookahead=ATTN_LOOKAHEAD, n_wc=n_wc),
        grid=(batch, seq // tq),
        in_specs=[
            pl.BlockSpec((1, MLA_HEADS, QK_DIM, tq), lambda b, i: (b, 0, 0, i)),
            pl.BlockSpec((1, MLA_HEADS, seq, NOPE), lambda b, i: (b, 0, 0, 0)),
            pl.BlockSpec((1, seq, ROPE_LANES), lambda b, i: (b, 0, 0)),
            pl.BlockSpec((1, MLA_HEADS, seq // tk, V_ROWS, tk), lambda b, i: (b, 0, 0, 0, 0)),
            *[wchunk(n) for n in range(n_wc)],
            pl.BlockSpec((D_MODEL // n_attn, D_MODEL), lambda b, i: (step(b, i), 0)),
        ],
        out_specs=[pl.BlockSpec((1, tq, MLA_WIDTH), lambda b, i: (b, i, 0)),
                   pl.BlockSpec((n_wc * WCHUNK, D_MODEL), lambda b, i: (step(b, i), 0)),
                   pl.BlockSpec((D_MODEL // n_attn, D_MODEL), lambda b, i: (step(b, i), 0))],
        out_shape=[jax.ShapeDtypeStruct((batch, seq, MLA_WIDTH), BF16),
                   jax.ShapeDtypeStruct((3 * CONV_WIDTH, D_MODEL), BF16),
                   jax.ShapeDtypeStruct((D_MODEL, D_MODEL), BF16)],
        scratch_shapes=[pltpu.VMEM((MLA_HEADS, 1, tq), F32), pltpu.VMEM((MLA_HEADS, V_ROWS, tq), F32),
                        pltpu.VMEM((MLA_HEADS, QK_PAD, tq), BF16)],
        compiler_params=_params(2),
        name="attn",
    )(qt, kn, kr, vt, *([w_in_t] * n_wc), w_out)
    attn = attn.reshape(T, MLA_WIDTH)

    tm4 = TM_OUTPROJ
    x1, x1b = pl.pallas_call(
        functools.partial(_mix_outproj_kernel, tm=tm4, tc=TC_CONVMIX, rc=RC_OUTPROJ, tiles_per_seq=seq // tm4),
        grid=(T // tm4,),
        in_specs=[
            pl.BlockSpec((tm4, D_MODEL), lambda i: (i, 0)),
            pl.BlockSpec((tm4, MLA_WIDTH), lambda i: (i, 0)),
            full((3 * CONV_WIDTH, D_MODEL)),
            full((3, CONV_WIDTH)),
            full((D_MODEL, D_MODEL)), full((1, D_MODEL)), full((1, D_MODEL)),
        ],
        out_specs=[pl.BlockSpec((tm4, D_MODEL), lambda i: (i, 0)), pl.BlockSpec((tm4, D_MODEL), lambda i: (i, 0))],
        out_shape=[jax.ShapeDtypeStruct((T, D_MODEL), F32), jax.ShapeDtypeStruct((T, D_MODEL), BF16)],
        scratch_shapes=[pltpu.VMEM((2 * HALO, CONV_WIDTH), F32), pltpu.VMEM((tm4, CONV_WIDTH), BF16)],
        compiler_params=_params(1),
        name="mix_outproj",
    )(x2, attn, wconv, conv_w, w_out_b, ln1_g.reshape(1, D_MODEL), ln1_b.reshape(1, D_MODEL))

    tm5, tn5 = TM_FFN_UP, TN_FFN_UP
    nn5 = FFN_DIM // tn5
    nm5 = T // tm5
    wd_rows = FFN_DIM // (nn5 * nm5)
    h2, w_down_b = pl.pallas_call(
        functools.partial(_ffn_up_kernel, tn=tn5, tiles_per_seq=seq // tm5),
        grid=(nn5, nm5),
        in_specs=[
            pl.BlockSpec((tm5, D_MODEL), lambda n, m: (m, 0)),
            pl.BlockSpec((D_MODEL, tn5), lambda n, m: (0, n)),
            pl.BlockSpec((D_MODEL, tn5), lambda n, m: (0, n + nn5)),
            pl.BlockSpec((3, tn5), lambda n, m: (0, n)),
            pl.BlockSpec((3, tn5), lambda n, m: (0, n + nn5)),
            pl.BlockSpec((wd_rows, D_MODEL), lambda n, m: (n * nm5 + m, 0)),
        ],
        out_specs=[pl.BlockSpec((tm5, tn5), lambda n, m: (m, n)),
                   pl.BlockSpec((wd_rows, D_MODEL), lambda n, m: (n * nm5 + m, 0))],
        out_shape=[jax.ShapeDtypeStruct((T, FFN_DIM), BF16),
                   jax.ShapeDtypeStruct((FFN_DIM, D_MODEL), BF16)],
        scratch_shapes=[pltpu.VMEM((D_MODEL, tn5), BF16), pltpu.VMEM((D_MODEL, tn5), BF16),
                        pltpu.VMEM((2 * HALO, tn5), F32), pltpu.VMEM((2 * HALO, tn5), F32)],
        compiler_params=_params(2),
        name="ffn_up",
    )(x1b, w_gate_up, w_gate_up, ffn_conv_w, ffn_conv_w, w_down)

    tm6 = TM_FFN_DOWN
    out = pl.pallas_call(
        functools.partial(_ffn_down_kernel, tm=tm6, rc=RC_FFN_DOWN),
        grid=(T // tm6,),
        in_specs=[
            pl.BlockSpec((tm6, FFN_DIM), lambda m: (m, 0)),
            full((FFN_DIM, D_MODEL)),
            pl.BlockSpec((tm6, D_MODEL), lambda m: (m, 0)),
            full((1, D_MODEL)), full((1, D_MODEL)),
        ],
        out_specs=pl.BlockSpec((tm6, D_MODEL), lambda m: (m, 0)),
        out_shape=jax.ShapeDtypeStruct((T, D_MODEL), F32),
        compiler_params=_params(1),
        name="ffn_down",
    )(h2, w_down_b, x1, ln2_g.reshape(1, D_MODEL), ln2_b.reshape(1, D_MODEL))
    return out


def kernel(x, positions, w_in, q_norm_g, w_uq, kv_norm_g, w_ukv, conv_w, w_out, ln1_g, ln1_b,
           w_gate_up, ffn_conv_w, w_down, ln2_g, ln2_b):
    batch, seq, _ = x.shape
    x2 = x.reshape(batch * seq, D_MODEL)
    for l in range(DEPTH):
        x2 = _layer(x2, positions, w_in[l], q_norm_g[l], w_uq[l], kv_norm_g[l], w_ukv[l], conv_w[l],
                    w_out[l], ln1_g[l], ln1_b[l], w_gate_up[l], ffn_conv_w[l], w_down[l], ln2_g[l], ln2_b[l],
                    batch, seq)
    return x2.reshape(batch, seq, D_MODEL)
```

```python
import functools
import math

import jax
import jax.numpy as jnp
from jax import lax
from jax.experimental import pallas as pl
from jax.experimental.pallas import tpu as pltpu

LANES = 128
F32_SUBLANES = 8
BF16_SUBLANES = 2 * F32_SUBLANES
V7X_VMEM_BYTES = 64 * 1024 * 1024
VMEM_LIMIT = V7X_VMEM_BYTES - 8 * 1024 * 1024

D_MODEL = 2048
MLA_HEADS = 8
NOPE = 128
ROPE = 64
V_DIM = 128
V_ROWS = V_DIM + BF16_SUBLANES
MLA_WIDTH = MLA_HEADS * V_DIM
Q_RANK = 512
KV_RANK = 256
CONV_WIDTH = 1024
FFN_DIM = 5632
ROPE_THETA = 10000.0
LN_EPS = 1e-5
RMS_EPS = 1e-6
DEPTH = 1
ALPHA = (2.0 * DEPTH) ** 0.25
QK_DIM = NOPE + ROPE
Q_SCALE = (QK_DIM ** -0.5) * math.log2(math.e)
ROPE_LANES = 2 * ROPE
QK_PAD = NOPE + ROPE_LANES
LAT_ROWS = Q_RANK + KV_RANK + LANES
HALO = F32_SUBLANES
WCHUNK = 64
NEG = -1e30

TM_LATENT = 512
TK_ATTN = 256
TQ_ATTN = 512
ATTN_LOOKAHEAD = 2
TC_CONVMIX = 256
TM_OUTPROJ, RC_OUTPROJ = 512, 128
TM_FFN_UP, TN_FFN_UP = 1024, 512
TM_FFN_DOWN, RC_FFN_DOWN = 512, 128

F32 = jnp.float32
BF16 = jnp.bfloat16


def _dot(a, b):
    return jnp.dot(a, b, preferred_element_type=F32)


def _dot_nt(a, b):
    return lax.dot_general(a, b, (((1,), (1,)), ((), ())), preferred_element_type=F32)


def _rms(x, g):
    return x * lax.rsqrt(jnp.mean(x * x, axis=-1, keepdims=True) + RMS_EPS) * g


def _layer_norm(y, g, b):
    mu = jnp.mean(y, axis=-1, keepdims=True)
    d = y - mu
    var = jnp.mean(d * d, axis=-1, keepdims=True)
    return d * lax.rsqrt(var + LN_EPS) * g + b


def _causal_conv3(z, hs_ref, cw, cols):
    n = z.shape[0]
    w0, w1, w2 = cw[0:1, :], cw[1:2, :], cw[2:3, :]
    y = w2 * z + w1 * pltpu.roll(z, 1, axis=0) + w0 * pltpu.roll(z, 2, axis=0)
    head = z[0:HALO]
    hs_ref[HALO:2 * HALO, cols] = head
    y_head = w2 * head + w1 * hs_ref[HALO - 1:2 * HALO - 1, cols] + w0 * hs_ref[HALO - 2:2 * HALO - 2, cols]
    hs_ref[0:HALO, cols] = z[n - HALO:n]
    return jnp.concatenate([y_head, y[HALO:]], axis=0)


def _latent_kernel(x_ref, posr_ref, invf_ref, win_ref, gq_ref, gkv_ref, wuq_ref, wukv_ref,
                   qt_ref, kn_ref, kr_ref, vt_ref, wlat_ref, wqt_ref, wk_ref, wvt_ref, *, tm, tk):
    half = ROPE // 2
    latent = Q_RANK + KV_RANK

    @pl.when(pl.program_id(0) == 0)
    def _():
        wlat_ref[...] = win_ref[...].astype(BF16)
        wqt_ref[...] = wuq_ref[...].T.astype(BF16)
        for hh in range(MLA_HEADS):
            k_cols = slice((NOPE + V_DIM) * hh, (NOPE + V_DIM) * hh + NOPE)
            v_cols = slice((NOPE + V_DIM) * hh + NOPE, (NOPE + V_DIM) * (hh + 1))
            wk_ref[:, NOPE * hh:NOPE * (hh + 1)] = wukv_ref[:, k_cols].astype(BF16)
            wvt_ref[V_DIM * hh:V_DIM * (hh + 1), :] = wukv_ref[:, v_cols].T.astype(BF16)

    xb = x_ref[...].astype(BF16)
    h = _dot_nt(xb, wlat_ref[...])
    cqn = _rms(h[:, :Q_RANK], gq_ref[...] * Q_SCALE).astype(BF16)
    ckvn = _rms(h[:, Q_RANK:latent], gkv_ref[...]).astype(BF16)

    ang_t = invf_ref[...] * posr_ref[...].astype(F32)
    cos_t, sin_t = jnp.cos(ang_t), jnp.sin(ang_t)
    table = jnp.concatenate([cos_t, cos_t, sin_t, sin_t], axis=0).T
    k_r = h[:, latent:latent + ROPE]
    kr_ref[0] = (jnp.concatenate([k_r, k_r], axis=1) * table).astype(BF16)

    qt = _dot_nt(wqt_ref[...], cqn)
    for hh in range(MLA_HEADS):
        base = QK_DIM * hh
        r1 = qt[base + NOPE:base + NOPE + half]
        r2 = qt[base + NOPE + half:base + QK_DIM]
        qr1 = (r1 * cos_t - r2 * sin_t).astype(BF16)
        qr2 = (r2 * cos_t + r1 * sin_t).astype(BF16)
        qt_ref[0, hh, 0:NOPE, :] = qt[base:base + NOPE].astype(BF16)
        qt_ref[0, hh, NOPE:NOPE + half, :] = qr1
        qt_ref[0, hh, NOPE + half:QK_DIM, :] = qr2

    kn = _dot(ckvn, wk_ref[...])
    vt = _dot_nt(wvt_ref[...], ckvn)
    ones_row = (lax.broadcasted_iota(jnp.int32, (V_ROWS - V_DIM, tk), 0) == 0).astype(BF16)
    for hh in range(MLA_HEADS):
        kn_ref[0, hh] = kn[:, NOPE * hh:NOPE * hh + NOPE].astype(BF16)
        for c in range(tm // tk):
            vt_ref[0, hh, c, 0:V_DIM, :] = vt[V_DIM * hh:V_DIM * hh + V_DIM, tk * c:tk * c + tk].astype(BF16)
            vt_ref[0, hh, c, V_DIM:V_ROWS, :] = ones_row


def _attn_kernel(qt_ref, kn_ref, kr_ref, vt_ref, *refs, tq, tk, heads, lookahead, n_wc):
    wc_refs = refs[:n_wc]
    wo_ref = refs[n_wc]
    o_ref, wconv_ref, wob_ref, m_ref, acc_ref, qf_ref = refs[n_wc + 1:]
    wob_ref[...] = wo_ref[...].astype(BF16)
    qi = pl.program_id(1)
    per_q = tq // tk
    trip_blocks = 2 * per_q
    half = ROPE // 2
    m_ref[...] = jnp.full(m_ref.shape, NEG, F32)
    acc_ref[...] = jnp.zeros(acc_ref.shape, F32)
    for g in range(heads):
        qf_ref[g, 0:QK_DIM, :] = qt_ref[0, g]
        qf_ref[g, QK_DIM:QK_DIM + half, :] = qt_ref[0, g, NOPE + half:QK_DIM, :]
        qf_ref[g, QK_DIM + half:QK_PAD, :] = -qt_ref[0, g, NOPE:NOPE + half, :]

    for n, wc_ref in enumerate(wc_refs):
        wconv_ref[n * WCHUNK:(n + 1) * WCHUNK, :] = wc_ref[...].astype(BF16)

    def run(chains):
        def scores(chain):
            g, j, c0, _ = chain
            start = pl.multiple_of(j * tk, tk)
            k = jnp.concatenate([kn_ref[0, g, pl.ds(start, tk), :], kr_ref[0, pl.ds(start, tk), :]], axis=1)
            return _dot(k, qf_ref[g, :, c0:])

        ahead = [scores(ch) for ch in chains[:lookahead]]
        for i, (g, j, c0, mask) in enumerate(chains):
            s = ahead.pop(0)
            if i + lookahead < len(chains):
                ahead.append(scores(chains[i + lookahead]))
            if mask is not None:
                s = jnp.where(mask, s, NEG)
            m = m_ref[g, :, c0:]
            m_new = jnp.maximum(m, jnp.max(s, axis=0, keepdims=True))
            p = jnp.exp2(s - m_new)
            a = jnp.exp2(m - m_new)
            m_ref[g, :, c0:] = m_new
            acc_ref[g, :, c0:] = a * acc_ref[g, :, c0:] + _dot(vt_ref[0, g, j], p.astype(BF16))

    def full_blocks(first, count):
        return [(g, first + b, 0, None) for b in range(count) for g in range(heads)]

    def below_diagonal(t, c):
        run(full_blocks(trip_blocks * t, trip_blocks))
        return c

    n_below = qi * per_q
    lax.fori_loop(0, n_below // trip_blocks, below_diagonal, 0)

    @pl.when(n_below % trip_blocks != 0)
    def _():
        run(full_blocks(n_below - per_q, per_q))

    diag = []
    for b in range(per_q):
        c0 = b * tk
        key = lax.broadcasted_iota(jnp.int32, (tk, tq - c0), 0)
        qry = lax.broadcasted_iota(jnp.int32, (tk, tq - c0), 1)
        diag += [(g, per_q * qi + b, c0, key <= qry) for g in range(heads)]
    run(diag)
    for g in range(heads):
        out_t = acc_ref[g, 0:V_DIM, :] / acc_ref[g, V_DIM:V_DIM + 1, :]
        o_ref[0, :, V_DIM * g:V_DIM * g + V_DIM] = out_t.T.astype(BF16)


def _mix_outproj_kernel(x_ref, a_ref, w_ref, cw_ref, wo_ref, g_ref, b_ref, o32_ref, o16_ref, zs_ref, cs_ref, *,
                        tm, tc, rc, tiles_per_seq):
    @pl.when(pl.program_id(0) % tiles_per_seq == 0)
    def _():
        zs_ref[0:HALO, :] = jnp.zeros((HALO, CONV_WIDTH), F32)

    xb = x_ref[...].astype(BF16)
    for c in range(CONV_WIDTH // tc):
        cols = slice(c * tc, (c + 1) * tc)
        gate_b = _dot_nt(xb, w_ref[c * tc:(c + 1) * tc, :])
        gate_c = _dot_nt(xb, w_ref[CONV_WIDTH + c * tc:CONV_WIDTH + (c + 1) * tc, :])
        x_c = _dot_nt(xb, w_ref[2 * CONV_WIDTH + c * tc:2 * CONV_WIDTH + (c + 1) * tc, :])
        y = _causal_conv3(gate_c * x_c, zs_ref, cw_ref[:, cols], cols)
        cs_ref[:, cols] = (gate_b * y).astype(BF16)
    for r in range(tm // rc):
        rows = slice(r * rc, (r + 1) * rc)
        mixed = jnp.concatenate([a_ref[rows, :], cs_ref[rows, :]], axis=1)
        y = _dot(mixed, wo_ref[...]) + ALPHA * x_ref[rows, :]
        out = _layer_norm(y, g_ref[...], b_ref[...])
        o32_ref[rows, :] = out
        o16_ref[rows, :] = out.astype(BF16)


def _ffn_up_kernel(x_ref, wg_ref, wu_ref, cwg_ref, cwu_ref, wd_ref, o_ref, wdb_ref,
                   wgb_ref, wub_ref, gs_ref, us_ref, *, tn, tiles_per_seq):
    mi = pl.program_id(1)

    @pl.when(mi == 0)
    def _():
        wgb_ref[...] = wg_ref[...].astype(BF16)
        wub_ref[...] = wu_ref[...].astype(BF16)

    @pl.when(mi % tiles_per_seq == 0)
    def _():
        gs_ref[0:HALO, :] = jnp.zeros((HALO, tn), F32)
        us_ref[0:HALO, :] = jnp.zeros((HALO, tn), F32)

    wdb_ref[...] = wd_ref[...].astype(BF16)

    xb = x_ref[...]
    cols = slice(0, tn)
    g = _causal_conv3(_dot(xb, wgb_ref[...]), gs_ref, cwg_ref[...], cols)
    u = _causal_conv3(_dot(xb, wub_ref[...]), us_ref, cwu_ref[...], cols)
    o_ref[...] = (g / (1.0 + jnp.exp(-g)) * u).astype(BF16)


def _ffn_down_kernel(h_ref, w_ref, x_ref, g_ref, b_ref, o_ref, *, tm, rc):
    for r in range(tm // rc):
        rows = slice(r * rc, (r + 1) * rc)
        y = _dot(h_ref[rows, :], w_ref[...]) + ALPHA * x_ref[rows, :]
        o_ref[rows, :] = _layer_norm(y, g_ref[...], b_ref[...])


def _params(n_grid_axes):
    return pltpu.CompilerParams(dimension_semantics=("arbitrary",) * n_grid_axes, vmem_limit_bytes=VMEM_LIMIT)


def _layer(x2, pos, w_in, q_norm_g, w_uq, kv_norm_g, w_ukv, conv_w, w_out, ln1_g, ln1_b,
           w_gate_up, ffn_conv_w, w_down, ln2_g, ln2_b, batch, seq):
    T = batch * seq
    for tile in (TM_LATENT, TQ_ATTN, TM_OUTPROJ, TM_FFN_UP, TM_FFN_DOWN):
        assert seq % tile == 0, (seq, tile)
    assert TM_LATENT % TK_ATTN == 0 and TQ_ATTN % TK_ATTN == 0
    assert x2.shape == (T, D_MODEL) and w_in.shape == (D_MODEL, LAT_ROWS - LANES + ROPE + 3 * CONV_WIDTH)

    half = ROPE // 2
    gq = q_norm_g.reshape(1, Q_RANK)
    gkv = kv_norm_g.reshape(1, KV_RANK)
    inv_freq = (ROPE_THETA ** (-jnp.arange(0, ROPE, 2, dtype=F32) / ROPE)).reshape(half, 1)
    posr = pos.reshape(1, T)

    tm, tk = TM_LATENT, TK_ATTN
    nblk = seq // tm
    full = lambda shape: pl.BlockSpec(shape, lambda *_: (0,) * len(shape), pipeline_mode=pl.Buffered(1))
    n1 = T // tm
    w_in_t = w_in.T
    qt, kn, kr, vt = pl.pallas_call(
        functools.partial(_latent_kernel, tm=tm, tk=tk),
        grid=(n1,),
        in_specs=[
            pl.BlockSpec((tm, D_MODEL), lambda i: (i, 0)),
            pl.BlockSpec((1, tm), lambda i: (0, i)),
            full((half, 1)),
            full((LAT_ROWS, D_MODEL)),
            full((1, Q_RANK)), full((1, KV_RANK)),
            full((Q_RANK, MLA_HEADS * QK_DIM)),
            full((KV_RANK, MLA_HEADS * (NOPE + V_DIM))),
        ],
        out_specs=[
            pl.BlockSpec((1, MLA_HEADS, QK_DIM, tm), lambda i: (i // nblk, 0, 0, i % nblk)),
            pl.BlockSpec((1, MLA_HEADS, tm, NOPE), lambda i: (i // nblk, 0, i % nblk, 0)),
            pl.BlockSpec((1, tm, ROPE_LANES), lambda i: (i // nblk, i % nblk, 0)),
            pl.BlockSpec((1, MLA_HEADS, tm // tk, V_ROWS, tk), lambda i: (i // nblk, 0, i % nblk, 0, 0)),
        ],
        out_shape=[
            jax.ShapeDtypeStruct((batch, MLA_HEADS, QK_DIM, seq), BF16),
            jax.ShapeDtypeStruct((batch, MLA_HEADS, seq, NOPE), BF16),
            jax.ShapeDtypeStruct((batch, seq, ROPE_LANES), BF16),
            jax.ShapeDtypeStruct((batch, MLA_HEADS, seq // tk, V_ROWS, tk), BF16),
        ],
        scratch_shapes=[pltpu.VMEM((LAT_ROWS, D_MODEL), BF16),
                        pltpu.VMEM((MLA_HEADS * QK_DIM, Q_RANK), BF16),
                        pltpu.VMEM((KV_RANK, MLA_HEADS * NOPE), BF16),
                        pltpu.VMEM((MLA_HEADS * V_DIM, KV_RANK), BF16)],
        compiler_params=_params(1),
        name="latent",
    )(x2, posr, inv_freq, w_in_t, gq, gkv, w_uq, w_ukv)

    tq = TQ_ATTN
    n_attn = batch * (seq // tq)
    conv_row0 = Q_RANK + KV_RANK + ROPE
    n_wc = 3 * CONV_WIDTH // (WCHUNK * n_attn)
    assert conv_row0 % WCHUNK == 0 and 3 * CONV_WIDTH == n_wc * WCHUNK * n_attn
    step = lambda b, i: b * (seq // tq) + i
    wchunk = lambda n: pl.BlockSpec((WCHUNK, D_MODEL),
                                    lambda b, i: (conv_row0 // WCHUNK + n_wc * step(b, i) + n, 0))
    attn, wconv, w_out_b = pl.pallas_call(
        functools.partial(_attn_kernel, tq=tq, tk=tk, heads=MLA_HEADS, lookahead=ATTN_LOOKAHEAD, n_wc=n_wc),
        grid=(batch, seq // tq),
        in_specs=[
            pl.BlockSpec((1, MLA_HEADS, QK_DIM, tq), lambda b, i: (b, 0, 0, i)),
            pl.BlockSpec((1, MLA_HEADS, seq, NOPE), lambda b, i: (b, 0, 0, 0)),
            pl.BlockSpec((1, seq, ROPE_LANES), lambda b, i: (b, 0, 0)),
            pl.BlockSpec((1, MLA_HEADS, seq // tk, V_ROWS, tk), lambda b, i: (b, 0, 0, 0, 0)),
            *[wchunk(n) for n in range(n_wc)],
            pl.BlockSpec((D_MODEL // n_attn, D_MODEL), lambda b, i: (step(b, i), 0)),
        ],
        out_specs=[pl.BlockSpec((1, tq, MLA_WIDTH), lambda b, i: (b, i, 0)),
                   pl.BlockSpec((n_wc * WCHUNK, D_MODEL), lambda b, i: (step(b, i), 0)),
                   pl.BlockSpec((D_MODEL // n_attn, D_MODEL), lambda b, i: (step(b, i), 0))],
        out_shape=[jax.ShapeDtypeStruct((batch, seq, MLA_WIDTH), BF16),
                   jax.ShapeDtypeStruct((3 * CONV_WIDTH, D_MODEL), BF16),
                   jax.ShapeDtypeStruct((D_MODEL, D_MODEL), BF16)],
        scratch_shapes=[pltpu.VMEM((MLA_HEADS, 1, tq), F32), pltpu.VMEM((MLA_HEADS, V_ROWS, tq), F32),
                        pltpu.VMEM((MLA_HEADS, QK_PAD, tq), BF16)],
        compiler_params=_params(2),
        name="attn",
    )(qt, kn, kr, vt, *([w_in_t] * n_wc), w_out)
    attn = attn.reshape(T, MLA_WIDTH)

    tm4 = TM_OUTPROJ
    x1, x1b = pl.pallas_call(
        functools.partial(_mix_outproj_kernel, tm=tm4, tc=TC_CONVMIX, rc=RC_OUTPROJ, tiles_per_seq=seq // tm4),
        grid=(T // tm4,),
        in_specs=[
            pl.BlockSpec((tm4, D_MODEL), lambda i: (i, 0)),
            pl.BlockSpec((tm4, MLA_WIDTH), lambda i: (i, 0)),
            full((3 * CONV_WIDTH, D_MODEL)),
            full((3, CONV_WIDTH)),
            full((D_MODEL, D_MODEL)), full((1, D_MODEL)), full((1, D_MODEL)),
        ],
        out_specs=[pl.BlockSpec((tm4, D_MODEL), lambda i: (i, 0)), pl.BlockSpec((tm4, D_MODEL), lambda i: (i, 0))],
        out_shape=[jax.ShapeDtypeStruct((T, D_MODEL), F32), jax.ShapeDtypeStruct((T, D_MODEL), BF16)],
        scratch_shapes=[pltpu.VMEM((2 * HALO, CONV_WIDTH), F32), pltpu.VMEM((tm4, CONV_WIDTH), BF16)],
        compiler_params=_params(1),
        name="mix_outproj",
    )(x2, attn, wconv, conv_w, w_out_b, ln1_g.reshape(1, D_MODEL), ln1_b.reshape(1, D_MODEL))

    tm5, tn5 = TM_FFN_UP, TN_FFN_UP
    nn5 = FFN_DIM // tn5
    nm5 = T // tm5
    wd_rows = FFN_DIM // (nn5 * nm5)
    h2, w_down_b = pl.pallas_call(
        functools.partial(_ffn_up_kernel, tn=tn5, tiles_per_seq=seq // tm5),
        grid=(nn5, nm5),
        in_specs=[
            pl.BlockSpec((tm5, D_MODEL), lambda n, m: (m, 0)),
            pl.BlockSpec((D_MODEL, tn5), lambda n, m: (0, n)),
            pl.BlockSpec((D_MODEL, tn5), lambda n, m: (0, n + nn5)),
            pl.BlockSpec((3, tn5), lambda n, m: (0, n)),
            pl.BlockSpec((3, tn5), lambda n, m: (0, n + nn5)),
            pl.BlockSpec((wd_rows, D_MODEL), lambda n, m: (n * nm5 + m, 0)),
        ],
        out_specs=[pl.BlockSpec((tm5, tn5), lambda n, m: (m, n)),
                   pl.BlockSpec((wd_rows, D_MODEL), lambda n, m: (n * nm5 + m, 0))],
        out_shape=[jax.ShapeDtypeStruct((T, FFN_DIM), BF16),
                   jax.ShapeDtypeStruct((FFN_DIM, D_MODEL), BF16)],
        scratch_shapes=[pltpu.VMEM((D_MODEL, tn5), BF16), pltpu.VMEM((D_MODEL, tn5), BF16),
                        pltpu.VMEM((2 * HALO, tn5), F32), pltpu.VMEM((2 * HALO, tn5), F32)],
        compiler_params=_params(2),
        name="ffn_up",
    )(x1b, w_gate_up, w_gate_up, ffn_conv_w, ffn_conv_w, w_down)

    tm6 = TM_FFN_DOWN
    out = pl.pallas_call(
        functools.partial(_ffn_down_kernel, tm=tm6, rc=RC_FFN_DOWN),
        grid=(T // tm6,),
        in_specs=[
            pl.BlockSpec((tm6, FFN_DIM), lambda m: (m, 0)),
            full((FFN_DIM, D_MODEL)),
            pl.BlockSpec((tm6, D_MODEL), lambda m: (m, 0)),
            full((1, D_MODEL)), full((1, D_MODEL)),
        ],
        out_specs=pl.BlockSpec((tm6, D_MODEL), lambda m: (m, 0)),
        out_shape=jax.ShapeDtypeStruct((T, D_MODEL), F32),
        compiler_params=_params(1),
        name="ffn_down",
    )(h2, w_down_b, x1, ln2_g.reshape(1, D_MODEL), ln2_b.reshape(1, D_MODEL))
    return out


def kernel(x, positions, w_in, q_norm_g, w_uq, kv_norm_g, w_ukv, conv_w, w_out, ln1_g, ln1_b,
           w_gate_up, ffn_conv_w, w_down, ln2_g, ln2_b):
    batch, seq, _ = x.shape
    x2 = x.reshape(batch * seq, D_MODEL)
    for l in range(DEPTH):
        x2 = _layer(x2, positions, w_in[l], q_norm_g[l], w_uq[l], kv_norm_g[l], w_ukv[l], conv_w[l],
                    w_out[l], ln1_g[l], ln1_b[l], w_gate_up[l], ffn_conv_w[l], w_down[l], ln2_g[l], ln2_b[l],
                    batch, seq)
    return x2.reshape(batch, seq, D_MODEL)
```
